```python
import math
import jax
import jax.numpy as jnp
from jax import lax
import numpy as np

D_MODEL = 1024
BATCH = 8
SEQ = 2048
DEPTH = 4
DEC_BATCH = 128
DEC_SEQ = 1
PAST_LEN = 16384
PAGE_SIZE = 128

N_META = 16
NORM_EPS = 1e-6
CHUNK = 64
CONV_W = 4
N_BRANCH = 3

RW_HEADS = 8
RW_HD = 64
RW_W = RW_HEADS * RW_HD
RW_DECAY_LORA = 64
RW_AAA_LORA = 64
RW_SHIFT_W = 3 * RW_W + RW_DECAY_LORA + RW_AAA_LORA
RW_GN_EPS = 64e-5

SSM_HEADS = 8
SSM_HD = 64
SSM_W = SSM_HEADS * SSM_HD
SSM_STATE = 128
SSM_GROUPS = 2
SSM_CONV_CH = SSM_W + 2 * SSM_GROUPS * SSM_STATE

GDN_HEADS = 4
GDN_HD = 128
GDN_W = GDN_HEADS * GDN_HD
GDN_CONV_CH = 3 * GDN_W

OFF_RW = 0
OFF_RW_Z = OFF_RW + RW_SHIFT_W
OFF_SSM_Z = OFF_RW_Z + RW_W
OFF_SSM_XBC = OFF_SSM_Z + SSM_W
OFF_SSM_DT = OFF_SSM_XBC + SSM_CONV_CH
OFF_GDN_QKV = OFF_SSM_DT + SSM_HEADS
OFF_GDN_Z = OFF_GDN_QKV + GDN_CONV_CH
OFF_GDN_A = OFF_GDN_Z + GDN_W
OFF_GDN_B = OFF_GDN_A + GDN_HEADS
OFF_GATE = OFF_GDN_B + GDN_HEADS
D_IN_PROJ = OFF_GATE + N_BRANCH * D_MODEL

F32 = jnp.float32

kernel_name = 'hybrid_rwkv7_mamba2_gdn_decoder_step'


def rmsnorm(x, w, eps=NORM_EPS):
    xf = x.astype(F32)
    return xf * lax.rsqrt(jnp.mean(xf * xf, axis=-1, keepdims=True) + eps) * w.astype(F32)


def l2norm(x, eps=1e-6):
    return x * lax.rsqrt(jnp.sum(x * x, axis=-1, keepdims=True) + eps)


def chunk_len(L):
    return CHUNK if L % CHUNK == 0 else L


def causal_conv(u, buf, w):
    L = u.shape[1]
    full = jnp.concatenate([buf.astype(F32), u], axis=1)
    out = full[:, 0:L] * w[0]
    for i in range(1, CONV_W):
        out = out + full[:, i:i + L] * w[i]
    return out, full[:, L:]


def run_segments(fn, seqs, state, split):
    if split:
        y0, state = fn(*[s[:, :split] for s in seqs], state)
        y1, state = fn(*[s[:, split:] for s in seqs], state)
        return jnp.concatenate([y0, y1], axis=1), state
    return fn(*seqs, state)


def wkv7_scan(r, logw, k, v, a, b, s0):
    def step(S, inp):
        r_t, lw_t, k_t, v_t, a_t, b_t = inp
        sa = jnp.einsum('bhvk,bhk->bhv', S, a_t)
        S = (S * jnp.exp(lw_t)[:, :, None, :] + sa[..., None] * b_t[:, :, None, :]
             + v_t[..., None] * k_t[:, :, None, :])
        return S, jnp.einsum('bhvk,bhk->bhv', S, r_t)
    xs = tuple(jnp.moveaxis(t, 1, 0) for t in (r, logw, k, v, a, b))
    S, ys = lax.scan(step, s0, xs)
    return jnp.moveaxis(ys, 0, 1), S


def rwkv7_mix(p_rw, wkv0, prev0, mu, w0, w2, a0, a2, k_k, k_a, r_k, gn_w, gn_b):
    Bsz, L, _ = p_rw.shape
    shifted = jnp.concatenate([prev0.astype(F32)[:, None], p_rw[:, :-1]], axis=1)
    u = p_rw + (shifted - p_rw) * mu
    r = u[..., :RW_W]
    k = u[..., RW_W:2 * RW_W]
    v = u[..., 2 * RW_W:3 * RW_W]
    wd = u[..., 3 * RW_W:3 * RW_W + RW_DECAY_LORA]
    ad = u[..., 3 * RW_W + RW_DECAY_LORA:]
    w = -jax.nn.softplus(-(w0 + jnp.tanh(wd) @ w2)) - 0.5
    log_decay = -jnp.exp(w)
    a = jax.nn.sigmoid(a0 + ad @ a2)
    hs = lambda t: t.reshape(Bsz, L, RW_HEADS, RW_HD)
    kk = l2norm(hs(k * k_k))
    k = k * (1.0 + (a - 1.0) * k_a)
    r, k, v, a, log_decay = (hs(t) for t in (r, k, v, a, log_decay))
    y, wkv1 = wkv7_scan(r, log_decay, k, v, -kk, kk * a, wkv0.astype(F32))
    mean = jnp.mean(y, axis=-1, keepdims=True)
    var = jnp.mean(jnp.square(y - mean), axis=-1, keepdims=True)
    y = ((y - mean) * lax.rsqrt(var + RW_GN_EPS)).reshape(Bsz, L, RW_W) * gn_w + gn_b
    bonus = jnp.sum(r * k * r_k, axis=-1, keepdims=True) * v
    return y + bonus.reshape(Bsz, L, RW_W), wkv1, p_rw[:, -1]


def ssd_chunked(x, dt, a, bm, cm, h0):
    Bsz, L, H, P = x.shape
    R = H // SSM_GROUPS
    c = chunk_len(L)
    n = L // c
    x = x.reshape(Bsz, n, c, SSM_GROUPS, R, P)
    dt = dt.reshape(Bsz, n, c, SSM_GROUPS, R)
    bm = bm.reshape(Bsz, n, c, SSM_GROUPS, SSM_STATE)
    cm = cm.reshape(Bsz, n, c, SSM_GROUPS, SSM_STATE)
    acs = jnp.cumsum(dt * a.reshape(SSM_GROUPS, R), axis=2)
    xdt = x * dt[..., None]
    acs_t = jnp.moveaxis(acs, 2, -1)
    tri = jnp.tril(jnp.ones((c, c), dtype=bool))
    seg = jnp.exp(jnp.where(tri, acs_t[..., :, None] - acs_t[..., None, :], -jnp.inf))
    cb = jnp.einsum('bnigs,bnjgs->bngij', cm, bm)
    y_diag = jnp.einsum('bngrij,bnjgrp->bnigrp', cb[:, :, :, None] * seg, xdt)
    decay_end = jnp.exp(acs[:, :, -1:] - acs)
    st = jnp.einsum('bncgs,bncgr,bncgrp->bngrps', bm, decay_end, xdt)
    tot = jnp.exp(acs[:, :, -1])

    def step(h, inp):
        t, s = inp
        return h * t[..., None, None] + s, h
    hN, h_prev = lax.scan(step, h0.reshape(Bsz, SSM_GROUPS, R, P, SSM_STATE),
                          (jnp.moveaxis(tot, 1, 0), jnp.moveaxis(st, 1, 0)))
    h_prev = jnp.moveaxis(h_prev, 0, 1)
    y_off = jnp.einsum('bnigs,bngrps,bnigr->bnigrp', cm, h_prev, jnp.exp(acs))
    return (y_diag + y_off).reshape(Bsz, L, H, P), hN.reshape(Bsz, H, P, SSM_STATE)


def mamba2_mix(xbc_pre, dt_raw, z, ssm0, conv0, conv_w, conv_b, dt_bias, a_log, d_skip, norm_w, split):
    Bsz, L, _ = xbc_pre.shape
    xbc, conv1 = causal_conv(xbc_pre, conv0, conv_w)
    xbc = jax.nn.silu(xbc + conv_b)
    xs = xbc[..., :SSM_W].reshape(Bsz, L, SSM_HEADS, SSM_HD)
    bm = xbc[..., SSM_W:SSM_W + SSM_GROUPS * SSM_STATE].reshape(Bsz, L, SSM_GROUPS, SSM_STATE)
    cm = xbc[..., SSM_W + SSM_GROUPS * SSM_STATE:].reshape(Bsz, L, SSM_GROUPS, SSM_STATE)
    dt = jax.nn.softplus(dt_raw + dt_bias)
    a = -jnp.exp(a_log.astype(F32))
    y, ssm1 = run_segments(lambda x_, dt_, b_, c_, h_: ssd_chunked(x_, dt_, a, b_, c_, h_),
                           (xs, dt, bm, cm), ssm0.astype(F32), split)
    y = (y + xs * d_skip[:, None]).reshape(Bsz, L, SSM_W)
    g = (y * jax.nn.silu(z)).reshape(Bsz, L, SSM_GROUPS, SSM_W // SSM_GROUPS)
    g = g * lax.rsqrt(jnp.mean(g * g, axis=-1, keepdims=True) + 1e-5)
    return g.reshape(Bsz, L, SSM_W) * norm_w, ssm1, conv1


def gdn_chunked(q, k, v, g, beta, s0):
    Bsz, L, H, D = q.shape
    c = chunk_len(L)
    n = L // c
    blk = lambda t: t.reshape(Bsz, n, c, H, D).transpose(0, 3, 1, 2, 4)
    q, k, v = blk(q), blk(k), blk(v)
    g = g.reshape(Bsz, n, c, H).transpose(0, 3, 1, 2)
    beta = beta.reshape(Bsz, n, c, H).transpose(0, 3, 1, 2)
    gcs = jnp.cumsum(g, axis=-1)
    idx = jnp.arange(c)
    incl = idx[:, None] >= idx[None, :]
    strict = idx[:, None] > idx[None, :]
    decay = jnp.exp(jnp.where(incl, gcs[..., :, None] - gcs[..., None, :], -jnp.inf))
    kb = k * beta[..., None]
    a_mat = jnp.where(strict, jnp.einsum('bhnid,bhnjd->bhnij', kb, k) * decay, 0.0)
    rhs = jnp.concatenate([v * beta[..., None], kb * jnp.exp(gcs)[..., None]], axis=-1)
    sol = lax.linalg.triangular_solve(a_mat, rhs, left_side=True, lower=True, unit_diagonal=True)
    u, w_cum = sol[..., :D], sol[..., D:]
    qk = jnp.where(incl, jnp.einsum('bhnid,bhnjd->bhnij', q, k) * decay, 0.0)
    q_dec = q * jnp.exp(gcs)[..., None]
    k_dec = k * jnp.exp(gcs[..., -1:] - gcs)[..., None]
    g_tot = jnp.exp(gcs[..., -1])

    def step(S, inp):
        u_c, w_c, qd_c, qk_c, kd_c, gt_c = inp
        v_new = u_c - jnp.einsum('bhck,bhkv->bhcv', w_c, S)
        o_c = jnp.einsum('bhck,bhkv->bhcv', qd_c, S) + jnp.einsum('bhij,bhjv->bhiv', qk_c, v_new)
        S = S * gt_c[..., None, None] + jnp.einsum('bhck,bhcv->bhkv', kd_c, v_new)
        return S, o_c
    xs = tuple(jnp.moveaxis(t, 2, 0) for t in (u, w_cum, q_dec, qk, k_dec, g_tot))
    S, o = lax.scan(step, s0, xs)
    return o.transpose(1, 0, 3, 2, 4).reshape(Bsz, L, H, D), S


def gdn_mix(qkv_pre, a_raw, b_raw, z, gdn0, conv0, conv_w, dt_bias, a_log, norm_w, split):
    Bsz, L, _ = qkv_pre.shape
    qkv, conv1 = causal_conv(qkv_pre, conv0, conv_w)
    qkv = jax.nn.silu(qkv)
    hs = lambda t: t.reshape(Bsz, L, GDN_HEADS, GDN_HD)
    q = l2norm(hs(qkv[..., :GDN_W])) * (GDN_HD ** -0.5)
    k = l2norm(hs(qkv[..., GDN_W:2 * GDN_W]))
    v = hs(qkv[..., 2 * GDN_W:])
    g = -jnp.exp(a_log.astype(F32)) * jax.nn.softplus(a_raw + dt_bias)
    beta = jax.nn.sigmoid(b_raw)
    o, gdn1 = run_segments(gdn_chunked, (q, k, v, g, beta), gdn0.astype(F32), split)
    o = o * lax.rsqrt(jnp.mean(o * o, axis=-1, keepdims=True) + 1e-6) * norm_w
    return o.reshape(Bsz, L, GDN_W) * jax.nn.silu(z), gdn1, conv1


def mixer_layer(x, st, p, split):
    wkv0, prev0, ssm0, sconv0, gdn0, gconv0 = st
    Bsz, T, _ = x.shape
    h = rmsnorm(x, p['norm_w'])
    proj = h @ p['w_in']
    rw, wkv1, prev1 = rwkv7_mix(proj[..., OFF_RW:OFF_RW + RW_SHIFT_W], wkv0, prev0, p['rw_mu'],
                                p['rw_w0'], p['rw_w2'], p['rw_a0'], p['rw_a2'], p['rw_k_k'],
                                p['rw_k_a'], p['rw_r_k'], p['rw_gn_w'], p['rw_gn_b'])
    rw = rw * jax.nn.silu(proj[..., OFF_RW_Z:OFF_RW_Z + RW_W])
    sm, ssm1, sconv1 = mamba2_mix(proj[..., OFF_SSM_XBC:OFF_SSM_XBC + SSM_CONV_CH],
                                  proj[..., OFF_SSM_DT:OFF_SSM_DT + SSM_HEADS],
                                  proj[..., OFF_SSM_Z:OFF_SSM_Z + SSM_W], ssm0, sconv0,
                                  p['ssm_conv_w'], p['ssm_conv_b'], p['ssm_dt_bias'],
                                  p['ssm_a_log'], p['ssm_d'], p['ssm_norm_w'], split)
    gd, gdn1, gconv1 = gdn_mix(proj[..., OFF_GDN_QKV:OFF_GDN_QKV + GDN_CONV_CH],
                               proj[..., OFF_GDN_A:OFF_GDN_A + GDN_HEADS],
                               proj[..., OFF_GDN_B:OFF_GDN_B + GDN_HEADS],
                               proj[..., OFF_GDN_Z:OFF_GDN_Z + GDN_W], gdn0, gconv0,
                               p['gdn_conv_w'], p['gdn_dt_bias'], p['gdn_a_log'],
                               p['gdn_norm_w'], split)
    gates = jax.nn.sigmoid(proj[..., OFF_GATE:].reshape(Bsz, T, N_BRANCH, D_MODEL))
    merged = (gates[..., 0, :] * (rw @ p['w_rw_out'])
              + gates[..., 1, :] * (sm @ p['w_ssm_out'])
              + gates[..., 2, :] * (gd @ p['w_gdn_out']))
    x = x + (merged @ p['w_out']).astype(x.dtype)
    new_st = (wkv1.astype(wkv0.dtype), prev1.astype(prev0.dtype), ssm1.astype(ssm0.dtype),
              sconv1.astype(sconv0.dtype), gdn1.astype(gdn0.dtype), gconv1.astype(gconv0.dtype))
    return x, new_st


def trunk(x, states, layer_params, final_norm_w, split):
    new_states = [[] for _ in states]
    for l in range(DEPTH):
        p = {name: arr[l] for name, arr in layer_params.items()}
        x, st = mixer_layer(x, tuple(s[l] for s in states), p, split)
        for acc, s in zip(new_states, st):
            acc.append(s)
    y = rmsnorm(x, final_norm_w).astype(x.dtype)
    return y, tuple(jnp.stack(acc) for acc in new_states)


def setup_inputs(seed: int = 0) -> dict:
    key = jax.random.key(seed)
    keys = iter(jax.random.split(key, 40))

    def nrm(shape, scale):
        return jax.random.normal(next(keys), shape, F32) * scale

    def unif(shape, lo, hi):
        return jax.random.uniform(next(keys), shape, F32, lo, hi)

    def gain(shape):
        return 1.0 + nrm(shape, 0.02)

    def dt_bias(shape):
        dt = jnp.exp(unif(shape, math.log(1e-3), math.log(1e-1)))
        return dt + jnp.log(-jnp.expm1(-dt))

    return {
        'x_prompt': nrm((BATCH, SEQ, D_MODEL), 1.0),
        'x_sample': nrm((DEC_BATCH, DEC_SEQ, D_MODEL), 1.0),
        'state_rwkv_wkv': nrm((DEPTH, DEC_BATCH, RW_HEADS, RW_HD, RW_HD), 0.3),
        'state_rwkv_shift': nrm((DEPTH, DEC_BATCH, RW_SHIFT_W), 1.0),
        'state_ssm': nrm((DEPTH, DEC_BATCH, SSM_HEADS, SSM_HD, SSM_STATE), 0.3),
        'state_ssm_conv': nrm((DEPTH, DEC_BATCH, CONV_W - 1, SSM_CONV_CH), 1.0),
        'state_gdn': nrm((DEPTH, DEC_BATCH, GDN_HEADS, GDN_HD, GDN_HD), 0.1),
        'state_gdn_conv': nrm((DEPTH, DEC_BATCH, CONV_W - 1, GDN_CONV_CH), 1.0),
        'meta_tokens': nrm((N_META, D_MODEL), 1.0),
        'norm_w': gain((DEPTH, D_MODEL)),
        'w_in': nrm((DEPTH, D_MODEL, D_IN_PROJ), D_MODEL ** -0.5),
        'rw_mu': unif((DEPTH, RW_SHIFT_W), 0.0, 1.0),
        'rw_w0': unif((DEPTH, RW_W), -6.0, 1.0),
        'rw_w2': nrm((DEPTH, RW_DECAY_LORA, RW_W), 0.1 * RW_DECAY_LORA ** -0.5),
        'rw_a0': nrm((DEPTH, RW_W), 0.1),
        'rw_a2': nrm((DEPTH, RW_AAA_LORA, RW_W), RW_AAA_LORA ** -0.5),
        'rw_k_k': 0.85 + nrm((DEPTH, RW_W), 0.02),
        'rw_k_a': gain((DEPTH, RW_W)),
        'rw_r_k': nrm((DEPTH, RW_HEADS, RW_HD), 0.1),
        'rw_gn_w': gain((DEPTH, RW_W)),
        'rw_gn_b': nrm((DEPTH, RW_W), 0.02),
        'ssm_conv_w': nrm((DEPTH, CONV_W, SSM_CONV_CH), CONV_W ** -0.5),
        'ssm_conv_b': nrm((DEPTH, SSM_CONV_CH), 0.02),
        'ssm_dt_bias': dt_bias((DEPTH, SSM_HEADS)),
        'ssm_a_log': jnp.log(unif((DEPTH, SSM_HEADS), 1.0, 16.0)),
        'ssm_d': 1.0 + nrm((DEPTH, SSM_HEADS), 0.1),
        'ssm_norm_w': gain((DEPTH, SSM_W)),
        'gdn_conv_w': nrm((DEPTH, CONV_W, GDN_CONV_CH), CONV_W ** -0.5),
        'gdn_dt_bias': dt_bias((DEPTH, GDN_HEADS)),
        'gdn_a_log': jnp.log(unif((DEPTH, GDN_HEADS), 1.0, 16.0)),
        'gdn_norm_w': gain((DEPTH, GDN_HD)),
        'w_rw_out': nrm((DEPTH, RW_W, D_MODEL), RW_W ** -0.5),
        'w_ssm_out': nrm((DEPTH, SSM_W, D_MODEL), SSM_W ** -0.5),
        'w_gdn_out': nrm((DEPTH, GDN_W, D_MODEL), GDN_W ** -0.5),
        'w_out': nrm((DEPTH, D_MODEL, D_MODEL), D_MODEL ** -0.5),
        'final_norm_w': gain((D_MODEL,)),
    }


def reference(x_prompt, x_sample, state_rwkv_wkv, state_rwkv_shift, state_ssm, state_ssm_conv,
              state_gdn, state_gdn_conv, meta_tokens, norm_w, w_in, rw_mu, rw_w0, rw_w2, rw_a0,
              rw_a2, rw_k_k, rw_k_a, rw_r_k, rw_gn_w, rw_gn_b, ssm_conv_w, ssm_conv_b,
              ssm_dt_bias, ssm_a_log, ssm_d, ssm_norm_w, gdn_conv_w, gdn_dt_bias, gdn_a_log,
              gdn_norm_w, w_rw_out, w_ssm_out, w_gdn_out, w_out, final_norm_w):
    layer_params = {
        'norm_w': norm_w, 'w_in': w_in, 'rw_mu': rw_mu, 'rw_w0': rw_w0, 'rw_w2': rw_w2,
        'rw_a0': rw_a0, 'rw_a2': rw_a2, 'rw_k_k': rw_k_k, 'rw_k_a': rw_k_a, 'rw_r_k': rw_r_k,
        'rw_gn_w': rw_gn_w, 'rw_gn_b': rw_gn_b, 'ssm_conv_w': ssm_conv_w,
        'ssm_conv_b': ssm_conv_b, 'ssm_dt_bias': ssm_dt_bias, 'ssm_a_log': ssm_a_log,
        'ssm_d': ssm_d, 'ssm_norm_w': ssm_norm_w, 'gdn_conv_w': gdn_conv_w,
        'gdn_dt_bias': gdn_dt_bias, 'gdn_a_log': gdn_a_log, 'gdn_norm_w': gdn_norm_w,
        'w_rw_out': w_rw_out, 'w_ssm_out': w_ssm_out, 'w_gdn_out': w_gdn_out, 'w_out': w_out,
    }
    sample_states = (state_rwkv_wkv, state_rwkv_shift, state_ssm, state_ssm_conv,
                     state_gdn, state_gdn_conv)
    bp = x_prompt.shape[0]
    prompt_states = tuple(jnp.zeros((DEPTH, bp) + s.shape[2:], x_prompt.dtype) for s in sample_states)
    meta = jnp.broadcast_to(meta_tokens.astype(x_prompt.dtype)[None], (bp, N_META, D_MODEL))
    x_full = jnp.concatenate([meta, x_prompt], axis=1)
    y_full, (p_wkv, p_shift, p_ssm, p_ssm_conv, p_gdn, p_gdn_conv) = trunk(
        x_full, prompt_states, layer_params, final_norm_w, N_META)
    y_prompt = y_full[:, N_META:]
    y_sample, (s_wkv, s_shift, s_ssm, s_ssm_conv, s_gdn, s_gdn_conv) = trunk(
        x_sample, sample_states, layer_params, final_norm_w, 0)
    return (y_prompt, y_sample, p_wkv, p_shift, p_ssm, p_ssm_conv, p_gdn, p_gdn_conv,
            s_wkv, s_shift, s_ssm, s_ssm_conv, s_gdn, s_gdn_conv)
```

```python
import functools
import math

import jax
import jax.numpy as jnp
from jax import lax
from jax.experimental import pallas as pl
from jax.experimental.pallas import tpu as pltpu

F32 = jnp.float32
BF16 = jnp.bfloat16
HIGHEST = lax.Precision.HIGHEST

NORM_EPS = 1e-6
RW_GN_EPS = 64e-5
SSM_NORM_EPS = 1e-5
GDN_NORM_EPS = 1e-6
L2_EPS = 1e-6
CONV_W = 4
CHUNK = 64
INV_BASE = 8
SMALL_W = 128
CONV_HALO = 8
VMEM_LIMIT = 56 * 1024 * 1024


def _mm(a, b, exact=False):
    if exact:
        return jnp.dot(a, b, preferred_element_type=F32, precision=HIGHEST)
    return jnp.dot(a.astype(BF16), b.astype(BF16), preferred_element_type=F32)


def _mm_nt(a, b, exact=False):
    dn = (((1,), (1,)), ((), ()))
    if exact:
        return lax.dot_general(a, b, dn, preferred_element_type=F32, precision=HIGHEST)
    return lax.dot_general(a.astype(BF16), b.astype(BF16), dn, preferred_element_type=F32)


def _mm_tn(a, b, exact=False):
    dn = (((0,), (0,)), ((), ()))
    if exact:
        return lax.dot_general(a, b, dn, preferred_element_type=F32, precision=HIGHEST)
    return lax.dot_general(a.astype(BF16), b.astype(BF16), dn, preferred_element_type=F32)


def _sigmoid(x):
    return 1.0 / (1.0 + jnp.exp(-x))


def _silu(x):
    return x * _sigmoid(x)


def _softplus(x):
    return jnp.maximum(x, 0.0) + jnp.log1p(jnp.exp(-jnp.abs(x)))


def _iota2(shape, dim):
    return lax.broadcasted_iota(jnp.int32, shape, dim)


def _unit_lower_inverse(l_mat):
    c = l_mat.shape[0]
    ri = _iota2((c, c), 0)
    ci = _iota2((c, c), 1)
    base_shift = int(math.log2(INV_BASE))
    eye = jnp.where(ri == ci, 1.0, 0.0).astype(F32)
    l_diag = jnp.where((ri >> base_shift) == (ci >> base_shift), l_mat, 0.0)
    x = eye + l_diag
    p = _mm(l_diag, l_diag, exact=True)
    n_terms = 2
    while n_terms < INV_BASE:
        x = x + _mm(x, p, exact=True)
        n_terms *= 2
        if n_terms < INV_BASE:
            p = _mm(p, p, exact=True)
    shift = base_shift
    while (1 << shift) < c:
        same_outer = (ri >> (shift + 1)) == (ci >> (shift + 1))
        other_inner = (ri >> shift) != (ci >> shift)
        l_off = jnp.where(same_outer, jnp.where(other_inner, l_mat, 0.0), 0.0)
        x = x + _mm(x, _mm(l_off, x, exact=True), exact=True)
        shift += 1
    return x


def _conv_chunk(ext_ref, u, w_ref, first_chunk):
    c = u.shape[0]

    @pl.when(first_chunk)
    def _():
        ext_ref[0:CONV_HALO, :] = jnp.zeros((CONV_HALO, u.shape[1]), F32)

    ext_ref[CONV_HALO:CONV_HALO + c, :] = u
    out = u * w_ref[CONV_W - 1:CONV_W, :]
    for i in range(CONV_W - 1):
        off = CONV_HALO - (CONV_W - 1) + i
        out = out + ext_ref[off:off + c, :] * w_ref[i:i + 1, :]
    tail = ext_ref[c:c + CONV_HALO, :]
    ext_ref[0:CONV_HALO, :] = tail
    return out, tail[CONV_HALO - (CONV_W - 1):, :]


def _inproj_body(x_ref, nw_ref, *refs, n_seg, n_pad, tiles_per_seq):
    w_refs = refs[:n_seg]
    o_refs = refs[n_seg:]
    x = x_ref[...]
    h = x * lax.rsqrt(jnp.mean(x * x, axis=-1, keepdims=True) + NORM_EPS) * nw_ref[...]
    if n_pad:
        first = lax.rem(pl.program_id(0), tiles_per_seq) == 0
        row = _iota2((x.shape[0], 1), 0)
        h = jnp.where(jnp.logical_and(first, row < n_pad), 0.0, h)
    hb = h.astype(BF16)
    for w_ref, o_ref in zip(w_refs, o_refs):
        o_ref[...] = jnp.dot(hb, w_ref[...], preferred_element_type=F32)


def _inproj(x, norm_w, weights, *, tm, n_pad, seq_len):
    t, d = x.shape
    n_seg = len(weights)
    grid = (t // tm,)
    in_specs = [pl.BlockSpec((tm, d), lambda i: (i, 0)),
                pl.BlockSpec((1, d), lambda i: (0, 0))]
    for w in weights:
        in_specs.append(pl.BlockSpec(w.shape, lambda i: (0, 0), pipeline_mode=pl.Buffered(1)))
    out_specs = [pl.BlockSpec((tm, w.shape[1]), lambda i: (i, 0)) for w in weights]
    out_shape = [jax.ShapeDtypeStruct((t, w.shape[1]), F32) for w in weights]
    body = functools.partial(_inproj_body, n_seg=n_seg, n_pad=n_pad,
                             tiles_per_seq=(seq_len // tm) if n_pad else 1)
    return pl.pallas_call(
        body, grid=grid, in_specs=in_specs, out_specs=out_specs, out_shape=out_shape,
        compiler_params=pltpu.CompilerParams(dimension_semantics=("arbitrary",),
                                             vmem_limit_bytes=VMEM_LIMIT),
        name="inproj",
    )(x, norm_w, *weights)


def _outproj_body(rw_ref, sm_ref, gd_ref, gate_ref, x_ref, wr_ref, ws_ref, wg_ref, wo_ref,
                  fw_ref, o_ref, *maybe_y_ref, d_model):
    merged = None
    for b, (y_ref, w_ref) in enumerate(((rw_ref, wr_ref), (sm_ref, ws_ref), (gd_ref, wg_ref))):
        g = _sigmoid(gate_ref[:, b * d_model:(b + 1) * d_model])
        term = g * jnp.dot(y_ref[...].astype(BF16), w_ref[...], preferred_element_type=F32)
        merged = term if merged is None else merged + term
    x_new = x_ref[...] + jnp.dot(merged.astype(BF16), wo_ref[...], preferred_element_type=F32)
    o_ref[...] = x_new
    if maybe_y_ref:
        ms = jnp.mean(x_new * x_new, axis=-1, keepdims=True)
        maybe_y_ref[0][...] = x_new * lax.rsqrt(ms + NORM_EPS) * fw_ref[...]


def _outproj(rw, sm, gd, gate, x, w_r, w_s, w_g, w_o, final_w, *, tm, with_final):
    t, d = x.shape
    grid = (t // tm,)
    row = lambda i: (i, 0)
    fixed = lambda i: (0, 0)
    in_specs = [pl.BlockSpec((tm, rw.shape[1]), row), pl.BlockSpec((tm, sm.shape[1]), row),
                pl.BlockSpec((tm, gd.shape[1]), row), pl.BlockSpec((tm, gate.shape[1]), row),
                pl.BlockSpec((tm, d), row),
                pl.BlockSpec(w_r.shape, fixed), pl.BlockSpec(w_s.shape, fixed),
                pl.BlockSpec(w_g.shape, fixed), pl.BlockSpec(w_o.shape, fixed),
                pl.BlockSpec((1, d), fixed)]
    n_out = 2 if with_final else 1
    out_specs = [pl.BlockSpec((tm, d), row)] * n_out
    out_shape = [jax.ShapeDtypeStruct((t, d), F32)] * n_out
    return pl.pallas_call(
        functools.partial(_outproj_body, d_model=d),
        grid=grid, in_specs=in_specs, out_specs=out_specs, out_shape=out_shape,
        compiler_params=pltpu.CompilerParams(dimension_semantics=("arbitrary",),
                                             vmem_limit_bytes=VMEM_LIMIT),
        name="outproj",
    )(rw, sm, gd, gate, x, w_r, w_s, w_g, w_o, final_w)


def _rwkv_pre(p, shifted, mu, w0, w2, a0, a2, k_k, k_a, width, lora):
    u = p + (shifted - p) * mu
    r = u[:, :width]
    k = u[:, width:2 * width]
    v = u[:, 2 * width:3 * width]
    wd = u[:, 3 * width:3 * width + lora]
    ad = u[:, 3 * width + lora:]
    w = -_softplus(-(w0 + _mm(jnp.tanh(wd), w2))) - 0.5
    log_decay = -jnp.exp(w)
    a = _sigmoid(a0 + _mm(ad, a2))
    kk_raw = k * k_k
    k_mod = k * (1.0 + (a - 1.0) * k_a)
    return r, k_mod, v, log_decay, a, kk_raw


def _l2norm(x):
    return x * lax.rsqrt(jnp.sum(x * x, axis=-1, keepdims=True) + L2_EPS)


def _rwkv_post_head(y, r, k_mod, v, z, r_k, gn_w, gn_b):
    mean = jnp.mean(y, axis=-1, keepdims=True)
    var = jnp.mean(jnp.square(y - mean), axis=-1, keepdims=True)
    yn = (y - mean) * lax.rsqrt(var + RW_GN_EPS) * gn_w + gn_b
    bonus = jnp.sum(r * k_mod * r_k, axis=-1, keepdims=True) * v
    return (yn + bonus) * _silu(z)


def _rwkv_prompt_body(p_ref, z_ref, mu_ref, w0_ref, w2_ref, a0_ref, a2_ref, kk_ref, ka_ref,
                      rk_ref, gnw_ref, gnb_ref, o_ref, wkv_ref, shift_ref, prev_scr,
                      *, n_heads, head_dim):
    c_idx = pl.program_id(1)
    width = n_heads * head_dim
    lora = w2_ref.shape[0]
    p = p_ref[...]
    c = p.shape[0]

    @pl.when(c_idx == 0)
    def _():
        wkv_ref[...] = jnp.zeros(wkv_ref.shape, F32)
        prev_scr[...] = jnp.zeros(prev_scr.shape, F32)

    row = _iota2((c, 1), 0)
    shifted = jnp.where(row == 0, prev_scr[...], pltpu.roll(p, 1, 0))
    last_row = p[c - 1:c, :]
    prev_scr[...] = last_row
    shift_ref[0] = last_row

    r, k_mod, v, log_decay, a, kk_raw = _rwkv_pre(
        p, shifted, mu_ref[...], w0_ref[...], w2_ref[...], a0_ref[...], a2_ref[...],
        kk_ref[...], ka_ref[...], width, lora)

    ri = _iota2((c, c), 0)
    ci = _iota2((c, c), 1)
    incl = ri >= ci
    strict = ri > ci
    tril = jnp.where(incl, 1.0, 0.0).astype(F32)
    cum = _mm(tril, log_decay, exact=True)
    e_pos = jnp.exp(cum)
    e_neg = jnp.exp(-cum)
    e_prev = jnp.exp(cum - log_decay)
    z = z_ref[...]

    outs = []
    for h in range(n_heads):
        sl = slice(h * head_dim, (h + 1) * head_dim)
        kk = _l2norm(kk_raw[:, sl])
        ep, en = e_pos[:, sl], e_neg[:, sl]
        a_t = -kk * e_prev[:, sl]
        b_t = kk * a[:, sl] * en
        k_t = k_mod[:, sl] * en
        r_t = r[:, sl] * ep
        v_h = v[:, sl]
        s0 = wkv_ref[0, h]
        l_ab = jnp.where(strict, _mm_nt(a_t, b_t), 0.0)
        l_ak = jnp.where(strict, _mm_nt(a_t, k_t), 0.0)
        m_rb = jnp.where(incl, _mm_nt(r_t, b_t), 0.0)
        m_rk = jnp.where(incl, _mm_nt(r_t, k_t), 0.0)
        inv = _unit_lower_inverse(l_ab)
        u = _mm(inv, _mm_nt(a_t, s0) + _mm(l_ak, v_h))
        y = _mm_nt(r_t, s0) + _mm(m_rb, u) + _mm(m_rk, v_h)
        d_end = ep[c - 1:c, :]
        wkv_ref[0, h] = s0 * d_end + _mm_tn(u, b_t * d_end) + _mm_tn(v_h, k_t * d_end)
        outs.append(_rwkv_post_head(y, r[:, sl], k_mod[:, sl], v_h, z[:, sl],
                                    rk_ref[:, sl], gnw_ref[:, sl], gnb_ref[:, sl]))
    o_ref[...] = jnp.concatenate(outs, axis=-1)


def _rwkv_prompt(p_rw, z, prm, *, batch, n_chunks, n_heads, head_dim):
    t, shift_w = p_rw.shape
    width = n_heads * head_dim
    row = lambda b, c: (b * n_chunks + c, 0)
    fixed = lambda b, c: (0, 0)
    names = ("mu", "w0", "w2", "a0", "a2", "k_k", "k_a", "r_k", "gn_w", "gn_b")
    params = [prm[n] for n in names]
    in_specs = [pl.BlockSpec((CHUNK, shift_w), row), pl.BlockSpec((CHUNK, width), row)]
    in_specs += [pl.BlockSpec(a.shape, fixed) for a in params]
    out_specs = [pl.BlockSpec((CHUNK, width), row),
                 pl.BlockSpec((1, n_heads, head_dim, head_dim), lambda b, c: (b, 0, 0, 0)),
                 pl.BlockSpec((1, 1, shift_w), lambda b, c: (b, 0, 0))]
    out_shape = [jax.ShapeDtypeStruct((t, width), F32),
                 jax.ShapeDtypeStruct((batch, n_heads, head_dim, head_dim), F32),
                 jax.ShapeDtypeStruct((batch, 1, shift_w), F32)]
    return pl.pallas_call(
        functools.partial(_rwkv_prompt_body, n_heads=n_heads, head_dim=head_dim),
        grid=(batch, n_chunks), in_specs=in_specs, out_specs=out_specs, out_shape=out_shape,
        scratch_shapes=[pltpu.VMEM((1, shift_w), F32)],
        compiler_params=pltpu.CompilerParams(dimension_semantics=("arbitrary", "arbitrary"),
                                             vmem_limit_bytes=VMEM_LIMIT),
        name="rwkv_prompt",
    )(p_rw, z, *params)


def _ssm_gated_norm(y, z, norm_w, n_groups):
    g = y * _silu(z)
    gw = g.shape[1] // n_groups
    parts = []
    for i in range(n_groups):
        gi = g[:, i * gw:(i + 1) * gw]
        parts.append(gi * lax.rsqrt(jnp.mean(gi * gi, axis=-1, keepdims=True) + SSM_NORM_EPS))
    return jnp.concatenate(parts, axis=-1) * norm_w


def _ssd_prompt_body(xbc_ref, small_ref, z_ref, cw_ref, cb_ref, dtb_row_ref, dtb_col_ref,
                     alog_row_ref, alog_col_ref, d_ref, nw_ref, o_ref, h_ref, conv_ref, ext_scr,
                     *, n_heads, head_dim, n_groups, d_state, n_pad):
    c_idx = pl.program_id(1)
    width = n_heads * head_dim
    u = xbc_ref[...]
    c = u.shape[0]

    @pl.when(c_idx == 0)
    def _():
        h_ref[...] = jnp.zeros(h_ref.shape, F32)

    conv, tail = _conv_chunk(ext_scr, u, cw_ref, c_idx == 0)
    conv_ref[0] = tail
    xbc = _silu(conv + cb_ref[...])
    xs = xbc[:, :width]
    b_all = xbc[:, width:width + n_groups * d_state]
    c_all = xbc[:, width + n_groups * d_state:]

    small = small_ref[...]
    lane = _iota2((1, SMALL_W), 1)
    sub = _iota2((SMALL_W, 1), 0)
    dt = _softplus(small + dtb_row_ref[...])
    dt_t = _softplus(small.T + dtb_col_ref[...])
    if n_pad:
        live_col = jnp.logical_or(c_idx > 0, _iota2((c, 1), 0) >= n_pad)
        live_row = jnp.logical_or(c_idx > 0, _iota2((1, c), 1) >= n_pad)
        dt = jnp.where(live_col, dt, 0.0)
        dt_t = jnp.where(live_row, dt_t, 0.0)
    a_row = jnp.where(lane < n_heads, -jnp.exp(alog_row_ref[...]), 0.0)
    a_col = jnp.where(sub < n_heads, -jnp.exp(alog_col_ref[...]), 0.0)
    ri = _iota2((c, c), 0)
    ci = _iota2((c, c), 1)
    incl = ri >= ci
    tril = jnp.where(incl, 1.0, 0.0).astype(F32)
    triu = jnp.where(ri <= ci, 1.0, 0.0).astype(F32)
    acs = _mm(tril, dt * a_row, exact=True)
    acs_t = _mm(dt_t * a_col, triu, exact=True)

    hpg = n_heads // n_groups
    outs = []
    for g in range(n_groups):
        b_g = b_all[:, g * d_state:(g + 1) * d_state]
        c_g = c_all[:, g * d_state:(g + 1) * d_state]
        cb = _mm_nt(c_g, b_g)
        for hh in range(hpg):
            h = g * hpg + hh
            a_c = acs[:, h:h + 1]
            a_r = acs_t[h:h + 1, :]
            a_last = acs[c - 1:c, h:h + 1]
            seg = jnp.exp(jnp.where(incl, a_c - a_r, -jnp.inf))
            x_h = xs[:, h * head_dim:(h + 1) * head_dim]
            xdt = x_h * dt[:, h:h + 1]
            h0 = h_ref[0, h]
            y = _mm(cb * seg, xdt) + _mm_nt(c_g, h0) * jnp.exp(a_c)
            h_ref[0, h] = h0 * jnp.exp(a_last) + _mm_tn(xdt * jnp.exp(a_last - a_c), b_g)
            outs.append(y + x_h * d_ref[:, h * head_dim:(h + 1) * head_dim])
    y_all = jnp.concatenate(outs, axis=-1)
    o_ref[...] = _ssm_gated_norm(y_all, z_ref[...], nw_ref[...], n_groups)


def _ssd_prompt(xbc, small, z, prm, *, batch, n_chunks, n_heads, head_dim, n_groups, d_state,
                n_pad):
    t, conv_ch = xbc.shape
    width = n_heads * head_dim
    row = lambda b, c: (b * n_chunks + c, 0)
    fixed = lambda b, c: (0, 0)
    names = ("conv_w", "conv_b", "dtb_row", "dtb_col", "alog_row", "alog_col", "d", "norm_w")
    params = [prm[n] for n in names]
    in_specs = [pl.BlockSpec((CHUNK, conv_ch), row), pl.BlockSpec((CHUNK, SMALL_W), row),
                pl.BlockSpec((CHUNK, width), row)]
    in_specs += [pl.BlockSpec(a.shape, fixed) for a in params]
    out_specs = [pl.BlockSpec((CHUNK, width), row),
                 pl.BlockSpec((1, n_heads, head_dim, d_state), lambda b, c: (b, 0, 0, 0)),
                 pl.BlockSpec((1, CONV_W - 1, conv_ch), lambda b, c: (b, 0, 0))]
    out_shape = [jax.ShapeDtypeStruct((t, width), F32),
                 jax.ShapeDtypeStruct((batch, n_heads, head_dim, d_state), F32),
                 jax.ShapeDtypeStruct((batch, CONV_W - 1, conv_ch), F32)]
    return pl.pallas_call(
        functools.partial(_ssd_prompt_body, n_heads=n_heads, head_dim=head_dim,
                          n_groups=n_groups, d_state=d_state, n_pad=n_pad),
        grid=(batch, n_chunks), in_specs=in_specs, out_specs=out_specs, out_shape=out_shape,
        scratch_shapes=[pltpu.VMEM((CHUNK + CONV_HALO, conv_ch), F32)],
        compiler_params=pltpu.CompilerParams(dimension_semantics=("arbitrary", "arbitrary"),
                                             vmem_limit_bytes=VMEM_LIMIT),
        name="ssd_prompt",
    )(xbc, small, z, *params)


def _gdn_gates(small, small_t, dtb_row, dtb_col, alog_row, alog_col, n_heads, a_off):
    lane = _iota2((1, SMALL_W), 1)
    sub = _iota2((SMALL_W, 1), 0)
    lo, hi = a_off, a_off + n_heads
    g = jnp.where(jnp.logical_and(lane >= lo, lane < hi),
                  -jnp.exp(alog_row) * _softplus(small + dtb_row), 0.0)
    g_t = jnp.where(jnp.logical_and(sub >= lo, sub < hi),
                    -jnp.exp(alog_col) * _softplus(small_t + dtb_col), 0.0)
    beta = _sigmoid(small)
    return g, g_t, beta


def _gdn_post_head(o, z, norm_w):
    return o * lax.rsqrt(jnp.mean(o * o, axis=-1, keepdims=True) + GDN_NORM_EPS) * norm_w * _silu(z)


def _gdn_prompt_body(qkv_ref, small_ref, z_ref, cw_ref, dtb_row_ref, dtb_col_ref, alog_row_ref,
                     alog_col_ref, nw_ref, o_ref, s_ref, conv_ref, ext_scr,
                     *, n_heads, head_dim, a_off, b_off):
    c_idx = pl.program_id(1)
    width = n_heads * head_dim
    u = qkv_ref[...]
    c = u.shape[0]

    @pl.when(c_idx == 0)
    def _():
        s_ref[...] = jnp.zeros(s_ref.shape, F32)

    conv, tail = _conv_chunk(ext_scr, u, cw_ref, c_idx == 0)
    conv_ref[0] = tail
    qkv = _silu(conv)
    small = small_ref[...]
    g, g_t, beta = _gdn_gates(small, small.T, dtb_row_ref[...], dtb_col_ref[...],
                              alog_row_ref[...], alog_col_ref[...], n_heads, a_off)
    ri = _iota2((c, c), 0)
    ci = _iota2((c, c), 1)
    incl = ri >= ci
    strict = ri > ci
    tril = jnp.where(incl, 1.0, 0.0).astype(F32)
    triu = jnp.where(ri <= ci, 1.0, 0.0).astype(F32)
    gcs = _mm(tril, g, exact=True)
    gcs_t = _mm(g_t, triu, exact=True)
    z = z_ref[...]
    scale = head_dim ** -0.5

    outs = []
    for h in range(n_heads):
        sl = slice(h * head_dim, (h + 1) * head_dim)
        q = _l2norm(qkv[:, sl]) * scale
        k = _l2norm(qkv[:, width + h * head_dim:width + (h + 1) * head_dim])
        v = qkv[:, 2 * width + h * head_dim:2 * width + (h + 1) * head_dim]
        g_c = gcs[:, a_off + h:a_off + h + 1]
        g_r = gcs_t[a_off + h:a_off + h + 1, :]
        g_last = gcs[c - 1:c, a_off + h:a_off + h + 1]
        b_c = beta[:, b_off + h:b_off + h + 1]
        decay = jnp.exp(jnp.where(incl, g_c - g_r, -jnp.inf))
        kb = k * b_c
        a_mat = jnp.where(strict, _mm_nt(kb, k) * decay, 0.0)
        inv = _unit_lower_inverse(-a_mat)
        e_c = jnp.exp(g_c)
        u_sol = _mm(inv, v * b_c)
        w_sol = _mm(inv, kb * e_c)
        qk = jnp.where(incl, _mm_nt(q, k) * decay, 0.0)
        s0 = s_ref[0, h]
        v_new = u_sol - _mm(w_sol, s0)
        o = _mm(q * e_c, s0) + _mm(qk, v_new)
        s_ref[0, h] = s0 * jnp.exp(g_last) + _mm_tn(k * jnp.exp(g_last - g_c), v_new)
        outs.append(_gdn_post_head(o, z[:, sl], nw_ref[...]))
    o_ref[...] = jnp.concatenate(outs, axis=-1)


def _gdn_prompt(qkv, small, z, prm, *, batch, n_chunks, n_heads, head_dim, a_off, b_off):
    t, conv_ch = qkv.shape
    width = n_heads * head_dim
    row = lambda b, c: (b * n_chunks + c, 0)
    fixed = lambda b, c: (0, 0)
    names = ("conv_w", "dtb_row", "dtb_col", "alog_row", "alog_col", "norm_w")
    params = [prm[n] for n in names]
    in_specs = [pl.BlockSpec((CHUNK, conv_ch), row), pl.BlockSpec((CHUNK, SMALL_W), row),
                pl.BlockSpec((CHUNK, width), row)]
    in_specs += [pl.BlockSpec(a.shape, fixed) for a in params]
    out_specs = [pl.BlockSpec((CHUNK, width), row),
                 pl.BlockSpec((1, n_heads, head_dim, head_dim), lambda b, c: (b, 0, 0, 0)),
                 pl.BlockSpec((1, CONV_W - 1, conv_ch), lambda b, c: (b, 0, 0))]
    out_shape = [jax.ShapeDtypeStruct((t, width), F32),
                 jax.ShapeDtypeStruct((batch, n_heads, head_dim, head_dim), F32),
                 jax.ShapeDtypeStruct((batch, CONV_W - 1, conv_ch), F32)]
    return pl.pallas_call(
        functools.partial(_gdn_prompt_body, n_heads=n_heads, head_dim=head_dim,
                          a_off=a_off, b_off=b_off),
        grid=(batch, n_chunks), in_specs=in_specs, out_specs=out_specs, out_shape=out_shape,
        scratch_shapes=[pltpu.VMEM((CHUNK + CONV_HALO, conv_ch), F32)],
        compiler_params=pltpu.CompilerParams(dimension_semantics=("arbitrary", "arbitrary"),
                                             vmem_limit_bytes=VMEM_LIMIT),
        name="gdn_prompt",
    )(qkv, small, z, *params)


def _rwkv_decode_body(p_ref, z_ref, prev_ref, s_ref, mu_ref, w0_ref, w2_ref, a0_ref, a2_ref,
                      kk_ref, ka_ref, rk_ref, gnw_ref, gnb_ref, o_ref, so_ref,
                      t_scr, y_scr, row_scr, st_scr, *, n_heads, head_dim):
    h = pl.program_id(0)
    width = n_heads * head_dim
    lora = w2_ref.shape[0]
    bsz = p_ref.shape[0]

    @pl.when(h == 0)
    def _():
        r, k_mod, v, log_decay, a, kk_raw = _rwkv_pre(
            p_ref[...], prev_ref[...], mu_ref[...], w0_ref[...], w2_ref[...], a0_ref[...],
            a2_ref[...], kk_ref[...], ka_ref[...], width, lora)
        kk = jnp.concatenate(
            [_l2norm(kk_raw[:, i * head_dim:(i + 1) * head_dim]) for i in range(n_heads)], axis=-1)
        row_scr[0] = r
        row_scr[1] = k_mod
        row_scr[2] = v
        for i, arr in enumerate((r, jnp.exp(log_decay), k_mod, v, -kk, kk * a)):
            t_scr[i] = arr.T.reshape(n_heads, head_dim, bsz)

    st_scr[...] = s_ref[...].T.reshape(head_dim, head_dim, bsz)
    r_t = t_scr[0, h]
    w_t = t_scr[1, h]
    k_t = t_scr[2, h]
    a_t = t_scr[4, h]
    b_t = t_scr[5, h]

    def step(i, carry):
        s_v = st_scr[i]
        sa = jnp.sum(s_v * a_t, axis=0, keepdims=True)
        v_i = t_scr[3, h, pl.ds(i, 1), :]
        s_new = s_v * w_t + sa * b_t + v_i * k_t
        st_scr[i] = s_new
        y_scr[h, pl.ds(i, 1), :] = jnp.sum(s_new * r_t, axis=0, keepdims=True)
        return carry

    lax.fori_loop(0, head_dim, step, 0)
    so_ref[...] = st_scr[...].reshape(head_dim * head_dim, bsz).T

    @pl.when(h == n_heads - 1)
    def _():
        y = y_scr[...].reshape(width, bsz).T
        r, k_mod, v, z = row_scr[0], row_scr[1], row_scr[2], z_ref[...]
        outs = []
        for i in range(n_heads):
            sl = slice(i * head_dim, (i + 1) * head_dim)
            outs.append(_rwkv_post_head(y[:, sl], r[:, sl], k_mod[:, sl], v[:, sl], z[:, sl],
                                        rk_ref[:, sl], gnw_ref[:, sl], gnb_ref[:, sl]))
        o_ref[...] = jnp.concatenate(outs, axis=-1)


def _rwkv_decode(p_rw, z, prev, state2d, prm, *, n_heads, head_dim):
    bsz, shift_w = p_rw.shape
    width = n_heads * head_dim
    hh = head_dim * head_dim
    fixed = lambda h: (0, 0)
    names = ("mu", "w0", "w2", "a0", "a2", "k_k", "k_a", "r_k", "gn_w", "gn_b")
    params = [prm[n] for n in names]
    in_specs = [pl.BlockSpec((bsz, shift_w), fixed), pl.BlockSpec((bsz, width), fixed),
                pl.BlockSpec((bsz, shift_w), fixed), pl.BlockSpec((bsz, hh), lambda h: (0, h))]
    in_specs += [pl.BlockSpec(a.shape, fixed) for a in params]
    out_specs = [pl.BlockSpec((bsz, width), fixed), pl.BlockSpec((bsz, hh), lambda h: (0, h))]
    out_shape = [jax.ShapeDtypeStruct((bsz, width), F32),
                 jax.ShapeDtypeStruct(state2d.shape, F32)]
    return pl.pallas_call(
        functools.partial(_rwkv_decode_body, n_heads=n_heads, head_dim=head_dim),
        grid=(n_heads,), in_specs=in_specs, out_specs=out_specs, out_shape=out_shape,
        scratch_shapes=[pltpu.VMEM((6, n_heads, head_dim, bsz), F32),
                        pltpu.VMEM((n_heads, head_dim, bsz), F32),
                        pltpu.VMEM((3, bsz, width), F32),
                        pltpu.VMEM((head_dim, head_dim, bsz), F32)],
        compiler_params=pltpu.CompilerParams(dimension_semantics=("arbitrary",),
                                             vmem_limit_bytes=VMEM_LIMIT),
        name="rwkv_decode",
    )(p_rw, z, prev, state2d, *params)


def _ssd_decode_body(xbc_ref, small_ref, z_ref, conv_ref, s_ref, cw_ref, cb_ref, dtb_row_ref,
                     alog_row_ref, d_ref, nw_ref, o_ref, convo_ref, so_ref,
                     xt_scr, bc_scr, dec_scr, y_scr, st_scr,
                     *, n_heads, head_dim, n_groups, d_state):
    h = pl.program_id(0)
    width = n_heads * head_dim
    bsz = xbc_ref.shape[0]
    hpg = n_heads // n_groups

    @pl.when(h == 0)
    def _():
        u = xbc_ref[...]
        conv = u * cw_ref[CONV_W - 1:CONV_W, :]
        for i in range(CONV_W - 1):
            conv = conv + conv_ref[i] * cw_ref[i:i + 1, :]
        for i in range(CONV_W - 2):
            convo_ref[i] = conv_ref[i + 1]
        convo_ref[CONV_W - 2] = u
        xbc = _silu(conv + cb_ref[...])
        lane = _iota2((1, SMALL_W), 1)
        dt = _softplus(small_ref[...] + dtb_row_ref[...])
        a_row = jnp.where(lane < n_heads, -jnp.exp(alog_row_ref[...]), 0.0)
        dec_scr[0] = jnp.exp(dt * a_row).T
        dec_scr[1] = dt.T
        xs = xbc[:, :width]
        xt_scr[0] = xs.T.reshape(n_heads, head_dim, bsz)
        xt_scr[1] = (xs * d_ref[...]).T.reshape(n_heads, head_dim, bsz)
        bc_scr[...] = xbc[:, width:].T.reshape(2 * n_groups, d_state, bsz)

    st_scr[...] = s_ref[...].T.reshape(head_dim, d_state, bsz)
    g = h // hpg
    b_t = bc_scr[g]
    c_t = bc_scr[n_groups + g]
    decay = dec_scr[0, pl.ds(h, 1), :]
    dt_h = dec_scr[1, pl.ds(h, 1), :]

    def step(i, carry):
        x_i = xt_scr[0, h, pl.ds(i, 1), :]
        s_new = st_scr[i] * decay + (dt_h * x_i) * b_t
        st_scr[i] = s_new
        y_scr[h, pl.ds(i, 1), :] = (jnp.sum(s_new * c_t, axis=0, keepdims=True)
                                    + xt_scr[1, h, pl.ds(i, 1), :])
        return carry

    lax.fori_loop(0, head_dim, step, 0)
    so_ref[...] = st_scr[...].reshape(head_dim * d_state, bsz).T

    @pl.when(h == n_heads - 1)
    def _():
        y = y_scr[...].reshape(width, bsz).T
        o_ref[...] = _ssm_gated_norm(y, z_ref[...], nw_ref[...], n_groups)


def _ssd_decode(xbc, small, z, conv_t, state2d, prm, *, n_heads, head_dim, n_groups, d_state):
    bsz, conv_ch = xbc.shape
    width = n_heads * head_dim
    hs = head_dim * d_state
    fixed = lambda h: (0, 0)
    fixed3 = lambda h: (0, 0, 0)
    names = ("conv_w", "conv_b", "dtb_row", "alog_row", "d", "norm_w")
    params = [prm[n] for n in names]
    in_specs = [pl.BlockSpec((bsz, conv_ch), fixed), pl.BlockSpec((bsz, SMALL_W), fixed),
                pl.BlockSpec((bsz, width), fixed), pl.BlockSpec(conv_t.shape, fixed3),
                pl.BlockSpec((bsz, hs), lambda h: (0, h))]
    in_specs += [pl.BlockSpec(a.shape, fixed) for a in params]
    out_specs = [pl.BlockSpec((bsz, width), fixed), pl.BlockSpec(conv_t.shape, fixed3),
                 pl.BlockSpec((bsz, hs), lambda h: (0, h))]
    out_shape = [jax.ShapeDtypeStruct((bsz, width), F32),
                 jax.ShapeDtypeStruct(conv_t.shape, F32),
                 jax.ShapeDtypeStruct(state2d.shape, F32)]
    return pl.pallas_call(
        functools.partial(_ssd_decode_body, n_heads=n_heads, head_dim=head_dim,
                          n_groups=n_groups, d_state=d_state),
        grid=(n_heads,), in_specs=in_specs, out_specs=out_specs, out_shape=out_shape,
        scratch_shapes=[pltpu.VMEM((2, n_heads, head_dim, bsz), F32),
                        pltpu.VMEM((2 * n_groups, d_state, bsz), F32),
                        pltpu.VMEM((2, SMALL_W, bsz), F32),
                        pltpu.VMEM((n_heads, head_dim, bsz), F32),
                        pltpu.VMEM((head_dim, d_state, bsz), F32)],
        compiler_params=pltpu.CompilerParams(dimension_semantics=("arbitrary",),
                                             vmem_limit_bytes=VMEM_LIMIT),
        name="ssd_decode",
    )(xbc, small, z, conv_t, state2d, *params)


def _gdn_decode_body(qkv_ref, small_ref, z_ref, conv_ref, s_ref, cw_ref, dtb_row_ref,
                     alog_row_ref, nw_ref, o_ref, convo_ref, so_ref,
                     t_scr, gate_scr, y_scr, st_scr, *, n_heads, head_dim, a_off, b_off):
    h = pl.program_id(0)
    width = n_heads * head_dim
    bsz = qkv_ref.shape[0]

    @pl.when(h == 0)
    def _():
        u = qkv_ref[...]
        conv = u * cw_ref[CONV_W - 1:CONV_W, :]
        for i in range(CONV_W - 1):
            conv = conv + conv_ref[i] * cw_ref[i:i + 1, :]
        for i in range(CONV_W - 2):
            convo_ref[i] = conv_ref[i + 1]
        convo_ref[CONV_W - 2] = u
        qkv = _silu(conv)
        scale = head_dim ** -0.5
        q = jnp.concatenate([_l2norm(qkv[:, i * head_dim:(i + 1) * head_dim]) * scale
                             for i in range(n_heads)], axis=-1)
        k = jnp.concatenate([_l2norm(qkv[:, width + i * head_dim:width + (i + 1) * head_dim])
                             for i in range(n_heads)], axis=-1)
        v = qkv[:, 2 * width:]
        for i, arr in enumerate((q, k, v)):
            t_scr[i] = arr.T.reshape(n_heads, head_dim, bsz)
        small = small_ref[...]
        lane = _iota2((1, SMALL_W), 1)
        g = jnp.where(jnp.logical_and(lane >= a_off, lane < a_off + n_heads),
                      -jnp.exp(alog_row_ref[...]) * _softplus(small + dtb_row_ref[...]), 0.0)
        gate_scr[0] = jnp.exp(g).T
        gate_scr[1] = _sigmoid(small).T

    st_scr[...] = s_ref[...].T.reshape(head_dim, head_dim, bsz)
    q_t = t_scr[0, h]
    k_t = t_scr[1, h]
    v_t = t_scr[2, h]
    eg = gate_scr[0, pl.ds(a_off + h, 1), :]
    beta = gate_scr[1, pl.ds(b_off + h, 1), :]

    def reduce_step(i, carry):
        acc_k, acc_q = carry
        s_k = st_scr[i]
        k_i = t_scr[1, h, pl.ds(i, 1), :]
        q_i = t_scr[0, h, pl.ds(i, 1), :]
        return acc_k + k_i * s_k, acc_q + q_i * s_k

    zero = jnp.zeros((head_dim, bsz), F32)
    acc_k, acc_q = lax.fori_loop(0, head_dim, reduce_step, (zero, zero))
    v_new = beta * (v_t - eg * acc_k)
    qk = jnp.sum(q_t * k_t, axis=0, keepdims=True)
    y_scr[h] = eg * acc_q + qk * v_new

    def update_step(i, carry):
        k_i = t_scr[1, h, pl.ds(i, 1), :]
        st_scr[i] = st_scr[i] * eg + k_i * v_new
        return carry

    lax.fori_loop(0, head_dim, update_step, 0)
    so_ref[...] = st_scr[...].reshape(head_dim * head_dim, bsz).T

    @pl.when(h == n_heads - 1)
    def _():
        y = y_scr[...].reshape(width, bsz).T
        z = z_ref[...]
        outs = []
        for i in range(n_heads):
            sl = slice(i * head_dim, (i + 1) * head_dim)
            outs.append(_gdn_post_head(y[:, sl], z[:, sl], nw_ref[...]))
        o_ref[...] = jnp.concatenate(outs, axis=-1)


def _gdn_decode(qkv, small, z, conv_t, state2d, prm, *, n_heads, head_dim, a_off, b_off):
    bsz, conv_ch = qkv.shape
    width = n_heads * head_dim
    hh = head_dim * head_dim
    fixed = lambda h: (0, 0)
    fixed3 = lambda h: (0, 0, 0)
    names = ("conv_w", "dtb_row", "alog_row", "norm_w")
    params = [prm[n] for n in names]
    in_specs = [pl.BlockSpec((bsz, conv_ch), fixed), pl.BlockSpec((bsz, SMALL_W), fixed),
                pl.BlockSpec((bsz, width), fixed), pl.BlockSpec(conv_t.shape, fixed3),
                pl.BlockSpec((bsz, hh), lambda h: (0, h))]
    in_specs += [pl.BlockSpec(a.shape, fixed) for a in params]
    out_specs = [pl.BlockSpec((bsz, width), fixed), pl.BlockSpec(conv_t.shape, fixed3),
                 pl.BlockSpec((bsz, hh), lambda h: (0, h))]
    out_shape = [jax.ShapeDtypeStruct((bsz, width), F32),
                 jax.ShapeDtypeStruct(conv_t.shape, F32),
                 jax.ShapeDtypeStruct(state2d.shape, F32)]
    return pl.pallas_call(
        functools.partial(_gdn_decode_body, n_heads=n_heads, head_dim=head_dim,
                          a_off=a_off, b_off=b_off),
        grid=(n_heads,), in_specs=in_specs, out_specs=out_specs, out_shape=out_shape,
        scratch_shapes=[pltpu.VMEM((3, n_heads, head_dim, bsz), F32),
                        pltpu.VMEM((2, SMALL_W, bsz), F32),
                        pltpu.VMEM((n_heads, head_dim, bsz), F32),
                        pltpu.VMEM((head_dim, head_dim, bsz), F32)],
        compiler_params=pltpu.CompilerParams(dimension_semantics=("arbitrary",),
                                             vmem_limit_bytes=VMEM_LIMIT),
        name="gdn_decode",
    )(qkv, small, z, conv_t, state2d, *params)


def _pad_lanes(row, offset):
    out = jnp.zeros((SMALL_W,), F32)
    return lax.dynamic_update_slice(out, row.astype(F32), (offset,)).reshape(1, SMALL_W)


def _row_tile(seq_len, cap):
    best = 8
    for tm in range(8, cap + 1, 8):
        if seq_len % tm == 0:
            best = tm
    return best


def kernel(x_prompt, x_sample, state_rwkv_wkv, state_rwkv_shift, state_ssm, state_ssm_conv,
           state_gdn, state_gdn_conv, meta_tokens, norm_w, w_in, rw_mu, rw_w0, rw_w2, rw_a0,
           rw_a2, rw_k_k, rw_k_a, rw_r_k, rw_gn_w, rw_gn_b, ssm_conv_w, ssm_conv_b,
           ssm_dt_bias, ssm_a_log, ssm_d, ssm_norm_w, gdn_conv_w, gdn_dt_bias, gdn_a_log,
           gdn_norm_w, w_rw_out, w_ssm_out, w_gdn_out, w_out, final_norm_w):
    depth = w_in.shape[0]
    bp, seq, d_model = x_prompt.shape
    bd = x_sample.shape[0]
    n_meta = meta_tokens.shape[0]
    rw_heads, rw_hd = rw_r_k.shape[1], rw_r_k.shape[2]
    rw_w = rw_heads * rw_hd
    rw_shift_w = rw_mu.shape[1]
    ssm_heads, ssm_hd, ssm_state = state_ssm.shape[2], state_ssm.shape[3], state_ssm.shape[4]
    ssm_w = ssm_heads * ssm_hd
    ssm_conv_ch = ssm_conv_w.shape[2]
    ssm_groups = (ssm_conv_ch - ssm_w) // (2 * ssm_state)
    gdn_heads, gdn_hd = state_gdn.shape[2], state_gdn.shape[3]
    gdn_w = gdn_heads * gdn_hd
    gdn_conv_ch = gdn_conv_w.shape[2]
    a_off, b_off = ssm_heads, ssm_heads + gdn_heads
    assert b_off + gdn_heads <= SMALL_W

    off = 0
    cols = {}
    for name, wdt in (("rw", rw_shift_w), ("rw_z", rw_w), ("ssm_z", ssm_w), ("xbc", ssm_conv_ch),
                      ("dt", ssm_heads), ("qkv", gdn_conv_ch), ("gdn_z", gdn_w),
                      ("ga", gdn_heads), ("gb", gdn_heads), ("gate", 3 * d_model)):
        cols[name] = (off, off + wdt)
        off += wdt
    assert off == w_in.shape[2]

    def seg(name):
        lo, hi = cols[name]
        return w_in[:, :, lo:hi].astype(BF16)

    n_small = ssm_heads + 2 * gdn_heads
    w_small = jnp.concatenate(
        [seg("dt"), seg("ga"), seg("gb"), jnp.zeros((depth, d_model, SMALL_W - n_small), BF16)],
        axis=2)
    seg_names = ("rw", "rw_z", "ssm_z", "xbc", "qkv", "gdn_z", "gate")
    w_segs = [seg(n) for n in seg_names] + [w_small]
    w_r, w_s, w_g, w_o = (w.astype(BF16) for w in (w_rw_out, w_ssm_out, w_gdn_out, w_out))

    def layer_params(l):
        rw = dict(mu=rw_mu[l][None], w0=rw_w0[l][None], w2=rw_w2[l], a0=rw_a0[l][None],
                  a2=rw_a2[l], k_k=rw_k_k[l][None], k_a=rw_k_a[l][None],
                  r_k=rw_r_k[l].reshape(1, rw_w), gn_w=rw_gn_w[l][None], gn_b=rw_gn_b[l][None])
        dtb = _pad_lanes(ssm_dt_bias[l], 0)
        alog = _pad_lanes(ssm_a_log[l], 0)
        sm = dict(conv_w=ssm_conv_w[l], conv_b=ssm_conv_b[l][None], dtb_row=dtb,
                  dtb_col=dtb.reshape(SMALL_W, 1), alog_row=alog,
                  alog_col=alog.reshape(SMALL_W, 1),
                  d=jnp.repeat(ssm_d[l], ssm_hd)[None], norm_w=ssm_norm_w[l][None])
        gdtb = _pad_lanes(gdn_dt_bias[l], a_off)
        galog = _pad_lanes(gdn_a_log[l], a_off)
        gd = dict(conv_w=gdn_conv_w[l], dtb_row=gdtb, dtb_col=gdtb.reshape(SMALL_W, 1),
                  alog_row=galog, alog_col=galog.reshape(SMALL_W, 1),
                  norm_w=gdn_norm_w[l][None])
        return rw, sm, gd

    fw = final_norm_w[None]

    real_len = n_meta + seq
    seq_p = -(-real_len // CHUNK) * CHUNK
    n_pad = seq_p - real_len
    n_chunks = seq_p // CHUNK
    meta = jnp.broadcast_to(meta_tokens.astype(F32)[None], (bp, n_meta, d_model))
    xp = jnp.concatenate([jnp.zeros((bp, n_pad, d_model), F32), meta, x_prompt], axis=1)
    xp = xp.reshape(bp * seq_p, d_model)
    tm_in = _row_tile(seq_p, 384)
    tm_out = _row_tile(seq_p, 384)

    p_states = [[] for _ in range(6)]
    y_p = None
    for l in range(depth):
        rw_p, sm_p, gd_p = layer_params(l)
        p_rw, rw_z, ssm_z, xbc, qkv, gdn_z, gate, small = _inproj(
            xp, norm_w[l][None], [w[l] for w in w_segs], tm=tm_in, n_pad=n_pad, seq_len=seq_p)
        rw_o, wkv, shift = _rwkv_prompt(p_rw, rw_z, rw_p, batch=bp, n_chunks=n_chunks,
                                        n_heads=rw_heads, head_dim=rw_hd)
        sm_o, ssm, sconv = _ssd_prompt(xbc, small, ssm_z, sm_p, batch=bp, n_chunks=n_chunks,
                                       n_heads=ssm_heads, head_dim=ssm_hd, n_groups=ssm_groups,
                                       d_state=ssm_state, n_pad=n_pad)
        gd_o, gdn, gconv = _gdn_prompt(qkv, small, gdn_z, gd_p, batch=bp, n_chunks=n_chunks,
                                       n_heads=gdn_heads, head_dim=gdn_hd, a_off=a_off,
                                       b_off=b_off)
        last = l == depth - 1
        res = _outproj(rw_o, sm_o, gd_o, gate, xp, w_r[l], w_s[l], w_g[l], w_o[l], fw,
                       tm=tm_out, with_final=last)
        xp = res[0]
        if last:
            y_p = res[1]
        for acc, s in zip(p_states, (wkv, shift.reshape(bp, rw_shift_w), ssm, sconv, gdn, gconv)):
            acc.append(s)
    y_prompt = y_p.reshape(bp, seq_p, d_model)[:, n_pad + n_meta:]

    xs = x_sample.reshape(bd, d_model)
    s_states = [[] for _ in range(6)]
    y_s = None
    for l in range(depth):
        rw_p, sm_p, gd_p = layer_params(l)
        p_rw, rw_z, ssm_z, xbc, qkv, gdn_z, gate, small = _inproj(
            xs, norm_w[l][None], [w[l] for w in w_segs], tm=bd, n_pad=0, seq_len=bd)
        rw_o, wkv = _rwkv_decode(p_rw, rw_z, state_rwkv_shift[l],
                                 state_rwkv_wkv[l].reshape(bd, rw_heads * rw_hd * rw_hd), rw_p,
                                 n_heads=rw_heads, head_dim=rw_hd)
        sm_o, sconv, ssm = _ssd_decode(xbc, small, ssm_z,
                                       jnp.swapaxes(state_ssm_conv[l], 0, 1),
                                       state_ssm[l].reshape(bd, ssm_heads * ssm_hd * ssm_state),
                                       sm_p, n_heads=ssm_heads, head_dim=ssm_hd,
                                       n_groups=ssm_groups, d_state=ssm_state)
        gd_o, gconv, gdn = _gdn_decode(qkv, small, gdn_z,
                                       jnp.swapaxes(state_gdn_conv[l], 0, 1),
                                       state_gdn[l].reshape(bd, gdn_heads * gdn_hd * gdn_hd),
                                       gd_p, n_heads=gdn_heads, head_dim=gdn_hd, a_off=a_off,
                                       b_off=b_off)
        last = l == depth - 1
        res = _outproj(rw_o, sm_o, gd_o, gate, xs, w_r[l], w_s[l], w_g[l], w_o[l], fw,
                       tm=bd, with_final=last)
        xs = res[0]
        if last:
            y_s = res[1]
        new = (wkv.reshape(bd, rw_heads, rw_hd, rw_hd), p_rw,
               ssm.reshape(bd, ssm_heads, ssm_hd, ssm_state), jnp.swapaxes(sconv, 0, 1),
               gdn.reshape(bd, gdn_heads, gdn_hd, gdn_hd), jnp.swapaxes(gconv, 0, 1))
        for acc, s in zip(s_states, new):
            acc.append(s)
    y_sample = y_s.reshape(bd, 1, d_model)

    return (y_prompt, y_sample) + tuple(jnp.stack(a) for a in p_states) \
        + tuple(jnp.stack(a) for a in s_states)
```

```python
import functools
import math

import jax
import jax.numpy as jnp
from jax import lax
from jax.experimental import pallas as pl
from jax.experimental.pallas import tpu as pltpu

F32 = jnp.float32
BF16 = jnp.bfloat16
HIGHEST = lax.Precision.HIGHEST

NORM_EPS = 1e-6
RW_GN_EPS = 64e-5
SSM_NORM_EPS = 1e-5
GDN_NORM_EPS = 1e-6
L2_EPS = 1e-6
CONV_W = 4
CHUNK = 64
MAX_CHUNKS_PER_STEP = 3
INV_BASE = 8
SMALL_W = 128
CONV_HALO = 8
VMEM_LIMIT = 56 * 1024 * 1024


def _mm(a, b, exact=False):
    if exact:
        return jnp.dot(a, b, preferred_element_type=F32, precision=HIGHEST)
    return jnp.dot(a.astype(BF16), b.astype(BF16), preferred_element_type=F32)


def _mm_nt(a, b, exact=False):
    dn = (((1,), (1,)), ((), ()))
    if exact:
        return lax.dot_general(a, b, dn, preferred_element_type=F32, precision=HIGHEST)
    return lax.dot_general(a.astype(BF16), b.astype(BF16), dn, preferred_element_type=F32)


def _mm_tn(a, b, exact=False):
    dn = (((0,), (0,)), ((), ()))
    if exact:
        return lax.dot_general(a, b, dn, preferred_element_type=F32, precision=HIGHEST)
    return lax.dot_general(a.astype(BF16), b.astype(BF16), dn, preferred_element_type=F32)


def _sigmoid(x):
    return 1.0 / (1.0 + jnp.exp(-x))


def _silu(x):
    return x * _sigmoid(x)


def _softplus(x):
    return jnp.maximum(x, 0.0) + jnp.log1p(jnp.exp(-jnp.abs(x)))


def _iota2(shape, dim):
    return lax.broadcasted_iota(jnp.int32, shape, dim)


def _block_diag_pair(p):
    c = p.shape[0]
    lane = _iota2(p.shape, 1)
    pb = p.astype(BF16)
    zero = jnp.zeros_like(pb)
    return jnp.concatenate([jnp.where(lane < c, pb, zero), jnp.where(lane >= c, pb, zero)], axis=0)


def _mm_pair(x, p):
    return jnp.dot(x.astype(BF16), _block_diag_pair(p), preferred_element_type=F32)


def _unit_lower_inverse(l_mats):
    c = l_mats[0].shape[0]
    pairs = [jnp.concatenate(l_mats[i:i + 2], axis=1) for i in range(0, len(l_mats), 2)]
    ri = _iota2((c, 2 * c), 0)
    ci = _iota2((c, 2 * c), 1) & (c - 1)
    base_shift = int(math.log2(INV_BASE))
    same_base = (ri >> base_shift) == (ci >> base_shift)
    es = [jnp.where(same_base, l, 0.0) for l in pairs]
    ps = es
    n_terms = 1
    while 2 * n_terms < INV_BASE:
        ps = [_mm_pair(p, p) for p in ps]
        eps = [_mm_pair(e, p) for e, p in zip(es, ps)]
        es = [e + p + ep for e, p, ep in zip(es, ps, eps)]
        n_terms *= 2
    shift = base_shift
    while (1 << shift) < c:
        same_outer = (ri >> (shift + 1)) == (ci >> (shift + 1))
        other_inner = (ri >> shift) != (ci >> shift)
        l_off = [jnp.where(same_outer, jnp.where(other_inner, l, 0.0), 0.0) for l in pairs]
        ts = [lo + _mm_pair(lo, e) for lo, e in zip(l_off, es)]
        ets = [_mm_pair(e, t) for e, t in zip(es, ts)]
        es = [e + t + et for e, t, et in zip(es, ts, ets)]
        shift += 1
    out = []
    for e in es:
        out += [e[:, :c], e[:, c:]]
    return out


def _chunk_masks(c):
    ri = _iota2((c, c), 0)
    ci = _iota2((c, c), 1)
    return ri >= ci, ri > ci


def _block_tril(rows, c):
    shift = int(math.log2(c))
    ri = _iota2((rows, rows), 0)
    ci = _iota2((rows, rows), 1)
    return jnp.where((ri >> shift) == (ci >> shift),
                     jnp.where(ri >= ci, 1.0, 0.0), 0.0).astype(F32)


def _conv_chunk(ext_ref, u, w_ref, first_chunk):
    c = u.shape[0]

    @pl.when(first_chunk)
    def _():
        ext_ref[0:CONV_HALO, :] = jnp.zeros((CONV_HALO, u.shape[1]), F32)

    ext_ref[CONV_HALO:CONV_HALO + c, :] = u
    out = u * w_ref[CONV_W - 1:CONV_W, :]
    for i in range(CONV_W - 1):
        off = CONV_HALO - (CONV_W - 1) + i
        out = out + ext_ref[off:off + c, :] * w_ref[i:i + 1, :]
    tail = ext_ref[c:c + CONV_HALO, :]
    ext_ref[0:CONV_HALO, :] = tail
    return out, tail[CONV_HALO - (CONV_W - 1):, :]


def _inproj_body(x_ref, nw_ref, *refs, n_seg, n_pad, tiles_per_seq):
    w_refs = refs[:n_seg]
    o_refs = refs[n_seg:]
    x = x_ref[...]
    h = x * lax.rsqrt(jnp.mean(x * x, axis=-1, keepdims=True) + NORM_EPS) * nw_ref[...]
    if n_pad:
        first = lax.rem(pl.program_id(0), tiles_per_seq) == 0
        row = _iota2((x.shape[0], 1), 0)
        h = jnp.where(jnp.logical_and(first, row < n_pad), 0.0, h)
    hb = h.astype(BF16)
    for w_ref, o_ref in zip(w_refs, o_refs):
        o_ref[...] = jnp.dot(hb, w_ref[...], preferred_element_type=F32)


def _inproj(x, norm_w, weights, *, tm, n_pad, seq_len):
    t, d = x.shape
    n_seg = len(weights)
    grid = (t // tm,)
    in_specs = [pl.BlockSpec((tm, d), lambda i: (i, 0)),
                pl.BlockSpec((1, d), lambda i: (0, 0))]
    for w in weights:
        in_specs.append(pl.BlockSpec(w.shape, lambda i: (0, 0), pipeline_mode=pl.Buffered(1)))
    out_specs = [pl.BlockSpec((tm, w.shape[1]), lambda i: (i, 0)) for w in weights]
    out_shape = [jax.ShapeDtypeStruct((t, w.shape[1]), F32) for w in weights]
    body = functools.partial(_inproj_body, n_seg=n_seg, n_pad=n_pad,
                             tiles_per_seq=(seq_len // tm) if n_pad else 1)
    return pl.pallas_call(
        body, grid=grid, in_specs=in_specs, out_specs=out_specs, out_shape=out_shape,
        compiler_params=pltpu.CompilerParams(dimension_semantics=("arbitrary",),
                                             vmem_limit_bytes=VMEM_LIMIT),
        name="inproj",
    )(x, norm_w, *weights)


def _outproj_body(rw_ref, sm_ref, gd_ref, gate_ref, x_ref, wr_ref, ws_ref, wg_ref, wo_ref,
                  fw_ref, o_ref, *maybe_y_ref, d_model):
    merged = None
    for b, (y_ref, w_ref) in enumerate(((rw_ref, wr_ref), (sm_ref, ws_ref), (gd_ref, wg_ref))):
        g = _sigmoid(gate_ref[:, b * d_model:(b + 1) * d_model])
        term = g * jnp.dot(y_ref[...].astype(BF16), w_ref[...], preferred_element_type=F32)
        merged = term if merged is None else merged + term
    x_new = x_ref[...] + jnp.dot(merged.astype(BF16), wo_ref[...], preferred_element_type=F32)
    o_ref[...] = x_new
    if maybe_y_ref:
        ms = jnp.mean(x_new * x_new, axis=-1, keepdims=True)
        maybe_y_ref[0][...] = x_new * lax.rsqrt(ms + NORM_EPS) * fw_ref[...]


def _outproj(rw, sm, gd, gate, x, w_r, w_s, w_g, w_o, final_w, *, tm, with_final):
    t, d = x.shape
    grid = (t // tm,)
    row = lambda i: (i, 0)
    fixed = lambda i: (0, 0)
    in_specs = [pl.BlockSpec((tm, rw.shape[1]), row), pl.BlockSpec((tm, sm.shape[1]), row),
                pl.BlockSpec((tm, gd.shape[1]), row), pl.BlockSpec((tm, gate.shape[1]), row),
                pl.BlockSpec((tm, d), row),
                pl.BlockSpec(w_r.shape, fixed), pl.BlockSpec(w_s.shape, fixed),
                pl.BlockSpec(w_g.shape, fixed), pl.BlockSpec(w_o.shape, fixed),
                pl.BlockSpec((1, d), fixed)]
    n_out = 2 if with_final else 1
    out_specs = [pl.BlockSpec((tm, d), row)] * n_out
    out_shape = [jax.ShapeDtypeStruct((t, d), F32)] * n_out
    return pl.pallas_call(
        functools.partial(_outproj_body, d_model=d),
        grid=grid, in_specs=in_specs, out_specs=out_specs, out_shape=out_shape,
        compiler_params=pltpu.CompilerParams(dimension_semantics=("arbitrary",),
                                             vmem_limit_bytes=VMEM_LIMIT),
        name="outproj",
    )(rw, sm, gd, gate, x, w_r, w_s, w_g, w_o, final_w)


def _rwkv_pre(p, shifted, mu, w0, w2, a0, a2, k_k, k_a, width, lora):
    u = p + (shifted - p) * mu
    r = u[:, :width]
    k = u[:, width:2 * width]
    v = u[:, 2 * width:3 * width]
    wd = u[:, 3 * width:3 * width + lora]
    ad = u[:, 3 * width + lora:]
    w = -_softplus(-(w0 + _mm(jnp.tanh(wd), w2))) - 0.5
    log_decay = -jnp.exp(w)
    a = _sigmoid(a0 + _mm(ad, a2))
    kk_raw = k * k_k
    k_mod = k * (1.0 + (a - 1.0) * k_a)
    return r, k_mod, v, log_decay, a, kk_raw


def _l2norm(x):
    return x * lax.rsqrt(jnp.sum(x * x, axis=-1, keepdims=True) + L2_EPS)


def _rwkv_post_head(y, r, k_mod, v, z, r_k, gn_w, gn_b):
    mean = jnp.mean(y, axis=-1, keepdims=True)
    var = jnp.mean(jnp.square(y - mean), axis=-1, keepdims=True)
    yn = (y - mean) * lax.rsqrt(var + RW_GN_EPS) * gn_w + gn_b
    bonus = jnp.sum(r * k_mod * r_k, axis=-1, keepdims=True) * v
    return (yn + bonus) * _silu(z)


def _rwkv_prompt_body(p_ref, z_ref, mu_ref, w0_ref, w2_ref, a0_ref, a2_ref, kk_ref, ka_ref,
                      rk_ref, gnw_ref, gnb_ref, o_ref, wkv_ref, shift_ref, prev_scr,
                      *, n_heads, head_dim, n_sub):
    width = n_heads * head_dim
    lora = w2_ref.shape[0]
    p = p_ref[...]
    rows = p.shape[0]
    c = rows // n_sub

    @pl.when(pl.program_id(1) == 0)
    def _():
        wkv_ref[...] = jnp.zeros(wkv_ref.shape, F32)
        prev_scr[...] = jnp.zeros(prev_scr.shape, F32)

    row = _iota2((rows, 1), 0)
    shifted = jnp.where(row == 0, prev_scr[...], pltpu.roll(p, 1, 0))
    last_row = p[rows - 1:rows, :]
    prev_scr[...] = last_row
    shift_ref[0] = last_row

    r, k_mod, v, log_decay, a, kk_raw = _rwkv_pre(
        p, shifted, mu_ref[...], w0_ref[...], w2_ref[...], a0_ref[...], a2_ref[...],
        kk_ref[...], ka_ref[...], width, lora)
    cum = _mm(_block_tril(rows, c), log_decay, exact=True)
    e_pos = jnp.exp(cum)
    e_neg = jnp.exp(-cum)
    e_prev = jnp.exp(cum - log_decay)
    z = z_ref[...]
    incl, strict = _chunk_masks(c)

    hs = range(n_heads)
    sls = [slice(h * head_dim, (h + 1) * head_dim) for h in hs]
    units = [(j, h) for j in range(n_sub) for h in hs]
    rsl = [slice(j * c, (j + 1) * c) for j in range(n_sub)]
    kk = [_l2norm(kk_raw[:, sl]) for sl in sls]
    a_t = [-kk[h] * e_prev[:, sls[h]] for h in hs]
    b_t = [kk[h] * a[:, sls[h]] * e_neg[:, sls[h]] for h in hs]
    k_t = [k_mod[:, sl] * e_neg[:, sl] for sl in sls]
    r_t = [r[:, sl] * e_pos[:, sl] for sl in sls]
    v_h = [v[:, sl] for sl in sls]

    sc = {u: _mm_nt(jnp.concatenate([a_t[u[1]][rsl[u[0]]], r_t[u[1]][rsl[u[0]]]], axis=0),
                    jnp.concatenate([b_t[u[1]][rsl[u[0]]], k_t[u[1]][rsl[u[0]]]], axis=0))
          for u in units}
    l_ab = {u: jnp.where(strict, sc[u][:c, :c], 0.0) for u in units}
    l_ak = {u: jnp.where(strict, sc[u][:c, c:], 0.0) for u in units}
    m_rb = {u: jnp.where(incl, sc[u][c:, :c], 0.0) for u in units}
    m_rk = {u: jnp.where(incl, sc[u][c:, c:], 0.0) for u in units}
    lv = {u: _mm(jnp.concatenate([l_ak[u], m_rk[u]], axis=0), v_h[u[1]][rsl[u[0]]]) for u in units}
    inv = dict(zip(units, _unit_lower_inverse([l_ab[u] for u in units])))
    rhs = {u: jnp.concatenate([a_t[u[1]][rsl[u[0]]], lv[u][:c]], axis=1) for u in units}
    wu = {u: rhs[u] + _mm(inv[u], rhs[u]) for u in units}
    d_end = {(j, h): e_pos[(j + 1) * c - 1:(j + 1) * c, sls[h]] for j, h in units}
    qy = {u: _mm(m_rb[u], wu[u]) for u in units}
    mn = {u: _mm_tn(wu[u], b_t[u[1]][rsl[u[0]]] * d_end[u]) for u in units}
    n2 = {u: _mm_tn(v_h[u[1]][rsl[u[0]]], k_t[u[1]][rsl[u[0]]] * d_end[u]) for u in units}

    states = [wkv_ref[0, h] for h in hs]
    ys = [[] for _ in hs]
    for j in range(n_sub):
        y_dot = [_mm_nt(r_t[h][rsl[j]] + qy[(j, h)][:, :c], states[h]) for h in hs]
        s_dot = [_mm(states[h], mn[(j, h)][:c]) for h in hs]
        for h in hs:
            ys[h].append(y_dot[h] + qy[(j, h)][:, c:] + lv[(j, h)][c:])
            states[h] = states[h] * d_end[(j, h)] + s_dot[h] + (mn[(j, h)][c:] + n2[(j, h)])
    for h in hs:
        wkv_ref[0, h] = states[h]
    outs = [_rwkv_post_head(jnp.concatenate(ys[h], axis=0), r[:, sls[h]], k_mod[:, sls[h]],
                            v_h[h], z[:, sls[h]], rk_ref[:, sls[h]], gnw_ref[:, sls[h]],
                            gnb_ref[:, sls[h]]) for h in hs]
    o_ref[...] = jnp.concatenate(outs, axis=-1)


def _rwkv_prompt(p_rw, z, prm, *, batch, n_steps, n_sub, n_heads, head_dim):
    t, shift_w = p_rw.shape
    width = n_heads * head_dim
    rows = n_sub * CHUNK
    row = lambda b, c: (b * n_steps + c, 0)
    fixed = lambda b, c: (0, 0)
    names = ("mu", "w0", "w2", "a0", "a2", "k_k", "k_a", "r_k", "gn_w", "gn_b")
    params = [prm[n] for n in names]
    in_specs = [pl.BlockSpec((rows, shift_w), row), pl.BlockSpec((rows, width), row)]
    in_specs += [pl.BlockSpec(a.shape, fixed) for a in params]
    out_specs = [pl.BlockSpec((rows, width), row),
                 pl.BlockSpec((1, n_heads, head_dim, head_dim), lambda b, c: (b, 0, 0, 0)),
                 pl.BlockSpec((1, 1, shift_w), lambda b, c: (b, 0, 0))]
    out_shape = [jax.ShapeDtypeStruct((t, width), F32),
                 jax.ShapeDtypeStruct((batch, n_heads, head_dim, head_dim), F32),
                 jax.ShapeDtypeStruct((batch, 1, shift_w), F32)]
    return pl.pallas_call(
        functools.partial(_rwkv_prompt_body, n_heads=n_heads, head_dim=head_dim, n_sub=n_sub),
        grid=(batch, n_steps), in_specs=in_specs, out_specs=out_specs, out_shape=out_shape,
        scratch_shapes=[pltpu.VMEM((1, shift_w), F32)],
        compiler_params=pltpu.CompilerParams(dimension_semantics=("arbitrary", "arbitrary"),
                                             vmem_limit_bytes=VMEM_LIMIT),
        name="rwkv_prompt",
    )(p_rw, z, *params)


def _ssm_gated_norm(y, z, norm_w, n_groups):
    g = y * _silu(z)
    gw = g.shape[1] // n_groups
    parts = []
    for i in range(n_groups):
        gi = g[:, i * gw:(i + 1) * gw]
        parts.append(gi * lax.rsqrt(jnp.mean(gi * gi, axis=-1, keepdims=True) + SSM_NORM_EPS))
    return jnp.concatenate(parts, axis=-1) * norm_w


def _ssd_prompt_body(xbc_ref, small_ref, z_ref, cw_ref, cb_ref, dtb_row_ref, dtb_col_ref,
                     alog_row_ref, alog_col_ref, d_ref, nw_ref, o_ref, h_ref, conv_ref, ext_scr,
                     *, n_heads, head_dim, n_groups, d_state, n_pad, n_sub):
    first = pl.program_id(1) == 0
    width = n_heads * head_dim
    hpg = n_heads // n_groups
    gw = hpg * head_dim
    x_pre = xbc_ref[...]
    rows = x_pre.shape[0]
    c = rows // n_sub
    rsl = [slice(j * c, (j + 1) * c) for j in range(n_sub)]

    @pl.when(first)
    def _():
        h_ref[...] = jnp.zeros(h_ref.shape, F32)

    conv, tail = _conv_chunk(ext_scr, x_pre, cw_ref, first)
    conv_ref[0] = tail
    xbc = _silu(conv + cb_ref[...])
    xs = xbc[:, :width]
    b_all = xbc[:, width:width + n_groups * d_state]
    c_all = xbc[:, width + n_groups * d_state:]

    small = small_ref[...]
    lane = _iota2((1, SMALL_W), 1)
    sub = _iota2((SMALL_W, 1), 0)
    dt = _softplus(small + dtb_row_ref[...])
    dt_t = [_softplus(small[sl].T + dtb_col_ref[...]) for sl in rsl]
    if n_pad:
        assert n_pad <= c
        not_first = jnp.logical_not(first)
        dt = jnp.where(jnp.logical_or(not_first, _iota2((rows, 1), 0) >= n_pad), dt, 0.0)
        dt_t[0] = jnp.where(jnp.logical_or(not_first, _iota2((1, c), 1) >= n_pad), dt_t[0], 0.0)
    a_row = jnp.where(lane < n_heads, -jnp.exp(alog_row_ref[...]), 0.0)
    a_col = jnp.where(sub < n_heads, -jnp.exp(alog_col_ref[...]), 0.0)
    incl, _ = _chunk_masks(c)
    triu = jnp.where(_iota2((c, c), 0) <= _iota2((c, c), 1), 1.0, 0.0).astype(F32)
    acs = _mm(_block_tril(rows, c), dt * a_row, exact=True)
    acs_t = [_mm(t * a_col, triu, exact=True) for t in dt_t]
    expand = jnp.where((_iota2((SMALL_W, width), 1) // head_dim) == _iota2((SMALL_W, width), 0),
                       1.0, 0.0).astype(F32)
    wide = _mm(jnp.concatenate([acs, dt], axis=0), expand, exact=True)
    acs_w = wide[:rows]
    xdt = xs * wide[rows:]
    e_acs = jnp.exp(acs_w)

    hs = range(n_heads)
    gs = range(n_groups)
    js = range(n_sub)
    b_g = {(j, g): b_all[rsl[j], g * d_state:(g + 1) * d_state] for j in js for g in gs}
    c_g = {(j, g): c_all[rsl[j], g * d_state:(g + 1) * d_state] for j in js for g in gs}
    a_last = {j: acs_w[(j + 1) * c - 1:(j + 1) * c] for j in js}
    xdt_end = {j: xdt[rsl[j]] * jnp.exp(a_last[j] - acs_w[rsl[j]]) for j in js}

    cb = {jg: _mm_nt(c_g[jg], b_g[jg]) for jg in b_g}
    seg = {(j, h): jnp.exp(jnp.where(incl, acs[rsl[j], h:h + 1] - acs_t[j][h:h + 1, :], -jnp.inf))
           for j in js for h in hs}
    y_diag = {(j, h): _mm(cb[(j, h // hpg)] * seg[(j, h)],
                          xdt[rsl[j], h * head_dim:(h + 1) * head_dim])
              for j in js for h in hs}
    st = {(j, g): _mm_tn(xdt_end[j][:, g * gw:(g + 1) * gw], b_g[(j, g)]) for j in js for g in gs}

    states = [h_ref[0, g * hpg:(g + 1) * hpg].reshape(gw, d_state) for g in gs]
    ys = []
    for j in js:
        y_off = [_mm_nt(c_g[(j, g)], states[g]) for g in gs]
        ys.append(jnp.concatenate(y_off, axis=-1) * e_acs[rsl[j]]
                  + jnp.concatenate([y_diag[(j, h)] for h in hs], axis=-1))
        for g in gs:
            scaled = [states[g][i * head_dim:(i + 1) * head_dim]
                      * jnp.exp(acs[(j + 1) * c - 1:(j + 1) * c, g * hpg + i:g * hpg + i + 1])
                      for i in range(hpg)]
            states[g] = jnp.concatenate(scaled, axis=0) + st[(j, g)]
    for g in gs:
        h_ref[0, g * hpg:(g + 1) * hpg] = states[g].reshape(hpg, head_dim, d_state)
    y_all = jnp.concatenate(ys, axis=0) + xs * d_ref[...]
    o_ref[...] = _ssm_gated_norm(y_all, z_ref[...], nw_ref[...], n_groups)


def _ssd_prompt(xbc, small, z, prm, *, batch, n_steps, n_sub, n_heads, head_dim, n_groups,
                d_state, n_pad):
    t, conv_ch = xbc.shape
    width = n_heads * head_dim
    rows = n_sub * CHUNK
    row = lambda b, c: (b * n_steps + c, 0)
    fixed = lambda b, c: (0, 0)
    names = ("conv_w", "conv_b", "dtb_row", "dtb_col", "alog_row", "alog_col", "d", "norm_w")
    params = [prm[n] for n in names]
    in_specs = [pl.BlockSpec((rows, conv_ch), row), pl.BlockSpec((rows, SMALL_W), row),
                pl.BlockSpec((rows, width), row)]
    in_specs += [pl.BlockSpec(a.shape, fixed) for a in params]
    out_specs = [pl.BlockSpec((rows, width), row),
                 pl.BlockSpec((1, n_heads, head_dim, d_state), lambda b, c: (b, 0, 0, 0)),
                 pl.BlockSpec((1, CONV_W - 1, conv_ch), lambda b, c: (b, 0, 0))]
    out_shape = [jax.ShapeDtypeStruct((t, width), F32),
                 jax.ShapeDtypeStruct((batch, n_heads, head_dim, d_state), F32),
                 jax.ShapeDtypeStruct((batch, CONV_W - 1, conv_ch), F32)]
    return pl.pallas_call(
        functools.partial(_ssd_prompt_body, n_heads=n_heads, head_dim=head_dim,
                          n_groups=n_groups, d_state=d_state, n_pad=n_pad, n_sub=n_sub),
        grid=(batch, n_steps), in_specs=in_specs, out_specs=out_specs, out_shape=out_shape,
        scratch_shapes=[pltpu.VMEM((rows + CONV_HALO, conv_ch), F32)],
        compiler_params=pltpu.CompilerParams(dimension_semantics=("arbitrary", "arbitrary"),
                                             vmem_limit_bytes=VMEM_LIMIT),
        name="ssd_prompt",
    )(xbc, small, z, *params)


def _gdn_post_head(o, z, norm_w):
    return o * lax.rsqrt(jnp.mean(o * o, axis=-1, keepdims=True) + GDN_NORM_EPS) * norm_w * _silu(z)


def _gdn_prompt_body(qkv_ref, small_ref, z_ref, cw_ref, dtb_row_ref, dtb_col_ref, alog_row_ref,
                     alog_col_ref, nw_ref, o_ref, s_ref, conv_ref, ext_scr,
                     *, n_heads, head_dim, a_off, b_off, n_sub):
    first = pl.program_id(1) == 0
    width = n_heads * head_dim
    x_pre = qkv_ref[...]
    rows = x_pre.shape[0]
    c = rows // n_sub

    @pl.when(first)
    def _():
        s_ref[...] = jnp.zeros(s_ref.shape, F32)

    conv, tail = _conv_chunk(ext_scr, x_pre, cw_ref, first)
    conv_ref[0] = tail
    qkv = _silu(conv)
    small = small_ref[...]
    rsl = [slice(j * c, (j + 1) * c) for j in range(n_sub)]
    small_t = [small[sl].T for sl in rsl]
    lane = _iota2((1, SMALL_W), 1)
    sub = _iota2((SMALL_W, 1), 0)
    g = jnp.where(jnp.logical_and(lane >= a_off, lane < a_off + n_heads),
                  -jnp.exp(alog_row_ref[...]) * _softplus(small + dtb_row_ref[...]), 0.0)
    g_t = [jnp.where(jnp.logical_and(sub >= a_off, sub < a_off + n_heads),
                     -jnp.exp(alog_col_ref[...]) * _softplus(st + dtb_col_ref[...]), 0.0)
           for st in small_t]
    beta = _sigmoid(small)
    incl, strict = _chunk_masks(c)
    triu = jnp.where(_iota2((c, c), 0) <= _iota2((c, c), 1), 1.0, 0.0).astype(F32)
    gcs = _mm(_block_tril(rows, c), g, exact=True)
    gcs_t = [_mm(gt, triu, exact=True) for gt in g_t]
    z = z_ref[...]
    scale = head_dim ** -0.5

    hs = range(n_heads)
    sls = [slice(h * head_dim, (h + 1) * head_dim) for h in hs]
    units = [(j, h) for j in range(n_sub) for h in hs]
    q_h = [_l2norm(qkv[:, sl]) * scale for sl in sls]
    k_h = [_l2norm(qkv[:, width + h * head_dim:width + (h + 1) * head_dim]) for h in hs]
    v_h = [qkv[:, 2 * width + h * head_dim:2 * width + (h + 1) * head_dim] for h in hs]
    g_c = {(j, h): gcs[rsl[j], a_off + h:a_off + h + 1] for j, h in units}
    g_r = {(j, h): gcs_t[j][a_off + h:a_off + h + 1, :] for j, h in units}
    g_last = {(j, h): gcs[(j + 1) * c - 1:(j + 1) * c, a_off + h:a_off + h + 1] for j, h in units}
    b_c = {(j, h): beta[rsl[j], b_off + h:b_off + h + 1] for j, h in units}
    q = {(j, h): q_h[h][rsl[j]] for j, h in units}
    k = {(j, h): k_h[h][rsl[j]] for j, h in units}
    v = {(j, h): v_h[h][rsl[j]] for j, h in units}

    decay = {u: jnp.exp(jnp.where(incl, g_c[u] - g_r[u], -jnp.inf)) for u in units}
    kb = {u: k[u] * b_c[u] for u in units}
    e_c = {u: jnp.exp(g_c[u]) for u in units}
    sc = {u: _mm_nt(jnp.concatenate([kb[u], q[u]], axis=0), k[u]) for u in units}
    a_mat = {u: jnp.where(strict, sc[u][:c] * decay[u], 0.0) for u in units}
    qk = {u: jnp.where(incl, sc[u][c:] * decay[u], 0.0) for u in units}
    inv = dict(zip(units, _unit_lower_inverse([-a_mat[u] for u in units])))
    rhs = {u: jnp.concatenate([v[u] * b_c[u], kb[u] * e_c[u]], axis=1) for u in units}
    uw = {u: rhs[u] + _mm(inv[u], rhs[u]) for u in units}
    qo = {u: _mm(qk[u], uw[u]) for u in units}
    mn = {u: _mm_tn(k[u] * jnp.exp(g_last[u] - g_c[u]), uw[u]) for u in units}

    states = [s_ref[0, h] for h in hs]
    os_ = [[] for _ in hs]
    for j in range(n_sub):
        o_dot = [_mm(q[(j, h)] * e_c[(j, h)] - qo[(j, h)][:, head_dim:], states[h]) for h in hs]
        s_dot = [_mm(mn[(j, h)][:, head_dim:], states[h]) for h in hs]
        for h in hs:
            os_[h].append(o_dot[h] + qo[(j, h)][:, :head_dim])
            states[h] = (states[h] * jnp.exp(g_last[(j, h)]) - s_dot[h]
                         + mn[(j, h)][:, :head_dim])
    for h in hs:
        s_ref[0, h] = states[h]
    outs = [_gdn_post_head(jnp.concatenate(os_[h], axis=0), z[:, sls[h]], nw_ref[...])
            for h in hs]
    o_ref[...] = jnp.concatenate(outs, axis=-1)


def _gdn_prompt(qkv, small, z, prm, *, batch, n_steps, n_sub, n_heads, head_dim, a_off, b_off):
    t, conv_ch = qkv.shape
    width = n_heads * head_dim
    rows = n_sub * CHUNK
    row = lambda b, c: (b * n_steps + c, 0)
    fixed = lambda b, c: (0, 0)
    names = ("conv_w", "dtb_row", "dtb_col", "alog_row", "alog_col", "norm_w")
    params = [prm[n] for n in names]
    in_specs = [pl.BlockSpec((rows, conv_ch), row), pl.BlockSpec((rows, SMALL_W), row),
                pl.BlockSpec((rows, width), row)]
    in_specs += [pl.BlockSpec(a.shape, fixed) for a in params]
    out_specs = [pl.BlockSpec((rows, width), row),
                 pl.BlockSpec((1, n_heads, head_dim, head_dim), lambda b, c: (b, 0, 0, 0)),
                 pl.BlockSpec((1, CONV_W - 1, conv_ch), lambda b, c: (b, 0, 0))]
    out_shape = [jax.ShapeDtypeStruct((t, width), F32),
                 jax.ShapeDtypeStruct((batch, n_heads, head_dim, head_dim), F32),
                 jax.ShapeDtypeStruct((batch, CONV_W - 1, conv_ch), F32)]
    return pl.pallas_call(
        functools.partial(_gdn_prompt_body, n_heads=n_heads, head_dim=head_dim,
                          a_off=a_off, b_off=b_off, n_sub=n_sub),
        grid=(batch, n_steps), in_specs=in_specs, out_specs=out_specs, out_shape=out_shape,
        scratch_shapes=[pltpu.VMEM((rows + CONV_HALO, conv_ch), F32)],
        compiler_params=pltpu.CompilerParams(dimension_semantics=("arbitrary", "arbitrary"),
                                             vmem_limit_bytes=VMEM_LIMIT),
        name="gdn_prompt",
    )(qkv, small, z, *params)


def _rwkv_decode_body(p_ref, z_ref, prev_ref, s_ref, mu_ref, w0_ref, w2_ref, a0_ref, a2_ref,
                      kk_ref, ka_ref, rk_ref, gnw_ref, gnb_ref, o_ref, so_ref,
                      t_scr, y_scr, row_scr, st_scr, *, n_heads, head_dim):
    h = pl.program_id(0)
    width = n_heads * head_dim
    lora = w2_ref.shape[0]
    bsz = p_ref.shape[0]

    @pl.when(h == 0)
    def _():
        r, k_mod, v, log_decay, a, kk_raw = _rwkv_pre(
            p_ref[...], prev_ref[...], mu_ref[...], w0_ref[...], w2_ref[...], a0_ref[...],
            a2_ref[...], kk_ref[...], ka_ref[...], width, lora)
        kk = jnp.concatenate(
            [_l2norm(kk_raw[:, i * head_dim:(i + 1) * head_dim]) for i in range(n_heads)], axis=-1)
        row_scr[0] = r
        row_scr[1] = k_mod
        row_scr[2] = v
        for i, arr in enumerate((r, jnp.exp(log_decay), k_mod, v, -kk, kk * a)):
            t_scr[i] = arr.T.reshape(n_heads, head_dim, bsz)

    st_scr[...] = s_ref[...].T.reshape(head_dim, head_dim, bsz)
    r_t = t_scr[0, h]
    w_t = t_scr[1, h]
    k_t = t_scr[2, h]
    a_t = t_scr[4, h]
    b_t = t_scr[5, h]

    def step(i, carry):
        s_v = st_scr[i]
        sa = jnp.sum(s_v * a_t, axis=0, keepdims=True)
        v_i = t_scr[3, h, pl.ds(i, 1), :]
        s_new = s_v * w_t + sa * b_t + v_i * k_t
        st_scr[i] = s_new
        y_scr[h, pl.ds(i, 1), :] = jnp.sum(s_new * r_t, axis=0, keepdims=True)
        return carry

    lax.fori_loop(0, head_dim, step, 0)
    so_ref[...] = st_scr[...].reshape(head_dim * head_dim, bsz).T

    @pl.when(h == n_heads - 1)
    def _():
        y = y_scr[...].reshape(width, bsz).T
        r, k_mod, v, z = row_scr[0], row_scr[1], row_scr[2], z_ref[...]
        outs = []
        for i in range(n_heads):
            sl = slice(i * head_dim, (i + 1) * head_dim)
            outs.append(_rwkv_post_head(y[:, sl], r[:, sl], k_mod[:, sl], v[:, sl], z[:, sl],
                                        rk_ref[:, sl], gnw_ref[:, sl], gnb_ref[:, sl]))
        o_ref[...] = jnp.concatenate(outs, axis=-1)


def _rwkv_decode(p_rw, z, prev, state2d, prm, *, n_heads, head_dim):
    bsz, shift_w = p_rw.shape
    width = n_heads * head_dim
    hh = head_dim * head_dim
    fixed = lambda h: (0, 0)
    names = ("mu", "w0", "w2", "a0", "a2", "k_k", "k_a", "r_k", "gn_w", "gn_b")
    params = [prm[n] for n in names]
    in_specs = [pl.BlockSpec((bsz, shift_w), fixed), pl.BlockSpec((bsz, width), fixed),
                pl.BlockSpec((bsz, shift_w), fixed), pl.BlockSpec((bsz, hh), lambda h: (0, h))]
    in_specs += [pl.BlockSpec(a.shape, fixed) for a in params]
    out_specs = [pl.BlockSpec((bsz, width), fixed), pl.BlockSpec((bsz, hh), lambda h: (0, h))]
    out_shape = [jax.ShapeDtypeStruct((bsz, width), F32),
                 jax.ShapeDtypeStruct(state2d.shape, F32)]
    return pl.pallas_call(
        functools.partial(_rwkv_decode_body, n_heads=n_heads, head_dim=head_dim),
        grid=(n_heads,), in_specs=in_specs, out_specs=out_specs, out_shape=out_shape,
        scratch_shapes=[pltpu.VMEM((6, n_heads, head_dim, bsz), F32),
                        pltpu.VMEM((n_heads, head_dim, bsz), F32),
                        pltpu.VMEM((3, bsz, width), F32),
                        pltpu.VMEM((head_dim, head_dim, bsz), F32)],
        compiler_params=pltpu.CompilerParams(dimension_semantics=("arbitrary",),
                                             vmem_limit_bytes=VMEM_LIMIT),
        name="rwkv_decode",
    )(p_rw, z, prev, state2d, *params)


def _ssd_decode_body(xbc_ref, small_ref, z_ref, conv_ref, s_ref, cw_ref, cb_ref, dtb_row_ref,
                     alog_row_ref, d_ref, nw_ref, o_ref, convo_ref, so_ref,
                     xt_scr, bc_scr, dec_scr, y_scr, st_scr,
                     *, n_heads, head_dim, n_groups, d_state):
    h = pl.program_id(0)
    width = n_heads * head_dim
    bsz = xbc_ref.shape[0]
    hpg = n_heads // n_groups

    @pl.when(h == 0)
    def _():
        u = xbc_ref[...]
        conv = u * cw_ref[CONV_W - 1:CONV_W, :]
        for i in range(CONV_W - 1):
            conv = conv + conv_ref[i] * cw_ref[i:i + 1, :]
        for i in range(CONV_W - 2):
            convo_ref[i] = conv_ref[i + 1]
        convo_ref[CONV_W - 2] = u
        xbc = _silu(conv + cb_ref[...])
        lane = _iota2((1, SMALL_W), 1)
        dt = _softplus(small_ref[...] + dtb_row_ref[...])
        a_row = jnp.where(lane < n_heads, -jnp.exp(alog_row_ref[...]), 0.0)
        dec_scr[0] = jnp.exp(dt * a_row).T
        dec_scr[1] = dt.T
        xs = xbc[:, :width]
        xt_scr[0] = xs.T.reshape(n_heads, head_dim, bsz)
        xt_scr[1] = (xs * d_ref[...]).T.reshape(n_heads, head_dim, bsz)
        bc_scr[...] = xbc[:, width:].T.reshape(2 * n_groups, d_state, bsz)

    st_scr[...] = s_ref[...].T.reshape(head_dim, d_state, bsz)
    g = h // hpg
    b_t = bc_scr[g]
    c_t = bc_scr[n_groups + g]
    decay = dec_scr[0, pl.ds(h, 1), :]
    dt_h = dec_scr[1, pl.ds(h, 1), :]

    def step(i, carry):
        x_i = xt_scr[0, h, pl.ds(i, 1), :]
        s_new = st_scr[i] * decay + (dt_h * x_i) * b_t
        st_scr[i] = s_new
        y_scr[h, pl.ds(i, 1), :] = (jnp.sum(s_new * c_t, axis=0, keepdims=True)
                                    + xt_scr[1, h, pl.ds(i, 1), :])
        return carry

    lax.fori_loop(0, head_dim, step, 0)
    so_ref[...] = st_scr[...].reshape(head_dim * d_state, bsz).T

    @pl.when(h == n_heads - 1)
    def _():
        y = y_scr[...].reshape(width, bsz).T
        o_ref[...] = _ssm_gated_norm(y, z_ref[...], nw_ref[...], n_groups)


def _ssd_decode(xbc, small, z, conv_t, state2d, prm, *, n_heads, head_dim, n_groups, d_state):
    bsz, conv_ch = xbc.shape
    width = n_heads * head_dim
    hs = head_dim * d_state
    fixed = lambda h: (0, 0)
    fixed3 = lambda h: (0, 0, 0)
    names = ("conv_w", "conv_b", "dtb_row", "alog_row", "d", "norm_w")
    params = [prm[n] for n in names]
    in_specs = [pl.BlockSpec((bsz, conv_ch), fixed), pl.BlockSpec((bsz, SMALL_W), fixed),
                pl.BlockSpec((bsz, width), fixed), pl.BlockSpec(conv_t.shape, fixed3),
                pl.BlockSpec((bsz, hs), lambda h: (0, h))]
    in_specs += [pl.BlockSpec(a.shape, fixed) for a in params]
    out_specs = [pl.BlockSpec((bsz, width), fixed), pl.BlockSpec(conv_t.shape, fixed3),
                 pl.BlockSpec((bsz, hs), lambda h: (0, h))]
    out_shape = [jax.ShapeDtypeStruct((bsz, width), F32),
                 jax.ShapeDtypeStruct(conv_t.shape, F32),
                 jax.ShapeDtypeStruct(state2d.shape, F32)]
    return pl.pallas_call(
        functools.partial(_ssd_decode_body, n_heads=n_heads, head_dim=head_dim,
                          n_groups=n_groups, d_state=d_state),
        grid=(n_heads,), in_specs=in_specs, out_specs=out_specs, out_shape=out_shape,
        scratch_shapes=[pltpu.VMEM((2, n_heads, head_dim, bsz), F32),
                        pltpu.VMEM((2 * n_groups, d_state, bsz), F32),
                        pltpu.VMEM((2, SMALL_W, bsz), F32),
                        pltpu.VMEM((n_heads, head_dim, bsz), F32),
                        pltpu.VMEM((head_dim, d_state, bsz), F32)],
        compiler_params=pltpu.CompilerParams(dimension_semantics=("arbitrary",),
                                             vmem_limit_bytes=VMEM_LIMIT),
        name="ssd_decode",
    )(xbc, small, z, conv_t, state2d, *params)


def _gdn_decode_body(qkv_ref, small_ref, z_ref, conv_ref, s_ref, cw_ref, dtb_row_ref,
                     alog_row_ref, nw_ref, o_ref, convo_ref, so_ref,
                     t_scr, gate_scr, y_scr, st_scr, *, n_heads, head_dim, a_off, b_off):
    h = pl.program_id(0)
    width = n_heads * head_dim
    bsz = qkv_ref.shape[0]

    @pl.when(h == 0)
    def _():
        u = qkv_ref[...]
        conv = u * cw_ref[CONV_W - 1:CONV_W, :]
        for i in range(CONV_W - 1):
            conv = conv + conv_ref[i] * cw_ref[i:i + 1, :]
        for i in range(CONV_W - 2):
            convo_ref[i] = conv_ref[i + 1]
        convo_ref[CONV_W - 2] = u
        qkv = _silu(conv)
        scale = head_dim ** -0.5
        q = jnp.concatenate([_l2norm(qkv[:, i * head_dim:(i + 1) * head_dim]) * scale
                             for i in range(n_heads)], axis=-1)
        k = jnp.concatenate([_l2norm(qkv[:, width + i * head_dim:width + (i + 1) * head_dim])
                             for i in range(n_heads)], axis=-1)
        v = qkv[:, 2 * width:]
        for i, arr in enumerate((q, k, v)):
            t_scr[i] = arr.T.reshape(n_heads, head_dim, bsz)
        small = small_ref[...]
        lane = _iota2((1, SMALL_W), 1)
        g = jnp.where(jnp.logical_and(lane >= a_off, lane < a_off + n_heads),
                      -jnp.exp(alog_row_ref[...]) * _softplus(small + dtb_row_ref[...]), 0.0)
        gate_scr[0] = jnp.exp(g).T
        gate_scr[1] = _sigmoid(small).T

    st_scr[...] = s_ref[...].T.reshape(head_dim, head_dim, bsz)
    q_t = t_scr[0, h]
    k_t = t_scr[1, h]
    v_t = t_scr[2, h]
    eg = gate_scr[0, pl.ds(a_off + h, 1), :]
    beta = gate_scr[1, pl.ds(b_off + h, 1), :]

    def reduce_step(i, carry):
        acc_k, acc_q = carry
        s_k = st_scr[i]
        k_i = t_scr[1, h, pl.ds(i, 1), :]
        q_i = t_scr[0, h, pl.ds(i, 1), :]
        return acc_k + k_i * s_k, acc_q + q_i * s_k

    zero = jnp.zeros((head_dim, bsz), F32)
    acc_k, acc_q = lax.fori_loop(0, head_dim, reduce_step, (zero, zero))
    v_new = beta * (v_t - eg * acc_k)
    qk = jnp.sum(q_t * k_t, axis=0, keepdims=True)
    y_scr[h] = eg * acc_q + qk * v_new

    def update_step(i, carry):
        k_i = t_scr[1, h, pl.ds(i, 1), :]
        st_scr[i] = st_scr[i] * eg + k_i * v_new
        return carry

    lax.fori_loop(0, head_dim, update_step, 0)
    so_ref[...] = st_scr[...].reshape(head_dim * head_dim, bsz).T

    @pl.when(h == n_heads - 1)
    def _():
        y = y_scr[...].reshape(width, bsz).T
        z = z_ref[...]
        outs = []
        for i in range(n_heads):
            sl = slice(i * head_dim, (i + 1) * head_dim)
            outs.append(_gdn_post_head(y[:, sl], z[:, sl], nw_ref[...]))
        o_ref[...] = jnp.concatenate(outs, axis=-1)


def _gdn_decode(qkv, small, z, conv_t, state2d, prm, *, n_heads, head_dim, a_off, b_off):
    bsz, conv_ch = qkv.shape
    width = n_heads * head_dim
    hh = head_dim * head_dim
    fixed = lambda h: (0, 0)
    fixed3 = lambda h: (0, 0, 0)
    names = ("conv_w", "dtb_row", "alog_row", "norm_w")
    params = [prm[n] for n in names]
    in_specs = [pl.BlockSpec((bsz, conv_ch), fixed), pl.BlockSpec((bsz, SMALL_W), fixed),
                pl.BlockSpec((bsz, width), fixed), pl.BlockSpec(conv_t.shape, fixed3),
                pl.BlockSpec((bsz, hh), lambda h: (0, h))]
    in_specs += [pl.BlockSpec(a.shape, fixed) for a in params]
    out_specs = [pl.BlockSpec((bsz, width), fixed), pl.BlockSpec(conv_t.shape, fixed3),
                 pl.BlockSpec((bsz, hh), lambda h: (0, h))]
    out_shape = [jax.ShapeDtypeStruct((bsz, width), F32),
                 jax.ShapeDtypeStruct(conv_t.shape, F32),
                 jax.ShapeDtypeStruct(state2d.shape, F32)]
    return pl.pallas_call(
        functools.partial(_gdn_decode_body, n_heads=n_heads, head_dim=head_dim,
                          a_off=a_off, b_off=b_off),
        grid=(n_heads,), in_specs=in_specs, out_specs=out_specs, out_shape=out_shape,
        scratch_shapes=[pltpu.VMEM((3, n_heads, head_dim, bsz), F32),
                        pltpu.VMEM((2, SMALL_W, bsz), F32),
                        pltpu.VMEM((n_heads, head_dim, bsz), F32),
                        pltpu.VMEM((head_dim, head_dim, bsz), F32)],
        compiler_params=pltpu.CompilerParams(dimension_semantics=("arbitrary",),
                                             vmem_limit_bytes=VMEM_LIMIT),
        name="gdn_decode",
    )(qkv, small, z, conv_t, state2d, *params)


def _pad_lanes(row, offset):
    out = jnp.zeros((SMALL_W,), F32)
    return lax.dynamic_update_slice(out, row.astype(F32), (offset,)).reshape(1, SMALL_W)


def _row_tile(seq_len, cap):
    best = 8
    for tm in range(8, cap + 1, 8):
        if seq_len % tm == 0:
            best = tm
    return best


def kernel(x_prompt, x_sample, state_rwkv_wkv, state_rwkv_shift, state_ssm, state_ssm_conv,
           state_gdn, state_gdn_conv, meta_tokens, norm_w, w_in, rw_mu, rw_w0, rw_w2, rw_a0,
           rw_a2, rw_k_k, rw_k_a, rw_r_k, rw_gn_w, rw_gn_b, ssm_conv_w, ssm_conv_b,
           ssm_dt_bias, ssm_a_log, ssm_d, ssm_norm_w, gdn_conv_w, gdn_dt_bias, gdn_a_log,
           gdn_norm_w, w_rw_out, w_ssm_out, w_gdn_out, w_out, final_norm_w):
    depth = w_in.shape[0]
    bp, seq, d_model = x_prompt.shape
    bd = x_sample.shape[0]
    n_meta = meta_tokens.shape[0]
    rw_heads, rw_hd = rw_r_k.shape[1], rw_r_k.shape[2]
    rw_w = rw_heads * rw_hd
    rw_shift_w = rw_mu.shape[1]
    ssm_heads, ssm_hd, ssm_state = state_ssm.shape[2], state_ssm.shape[3], state_ssm.shape[4]
    ssm_w = ssm_heads * ssm_hd
    ssm_conv_ch = ssm_conv_w.shape[2]
    ssm_groups = (ssm_conv_ch - ssm_w) // (2 * ssm_state)
    gdn_heads, gdn_hd = state_gdn.shape[2], state_gdn.shape[3]
    gdn_w = gdn_heads * gdn_hd
    gdn_conv_ch = gdn_conv_w.shape[2]
    a_off, b_off = ssm_heads, ssm_heads + gdn_heads
    assert b_off + gdn_heads <= SMALL_W

    off = 0
    cols = {}
    for name, wdt in (("rw", rw_shift_w), ("rw_z", rw_w), ("ssm_z", ssm_w), ("xbc", ssm_conv_ch),
                      ("dt", ssm_heads), ("qkv", gdn_conv_ch), ("gdn_z", gdn_w),
                      ("ga", gdn_heads), ("gb", gdn_heads), ("gate", 3 * d_model)):
        cols[name] = (off, off + wdt)
        off += wdt
    assert off == w_in.shape[2]

    def seg(name):
        lo, hi = cols[name]
        return w_in[:, :, lo:hi].astype(BF16)

    n_small = ssm_heads + 2 * gdn_heads
    w_small = jnp.concatenate(
        [seg("dt"), seg("ga"), seg("gb"), jnp.zeros((depth, d_model, SMALL_W - n_small), BF16)],
        axis=2)
    seg_names = ("rw", "rw_z", "ssm_z", "xbc", "qkv", "gdn_z", "gate")
    w_segs = [seg(n) for n in seg_names] + [w_small]
    w_r, w_s, w_g, w_o = (w.astype(BF16) for w in (w_rw_out, w_ssm_out, w_gdn_out, w_out))

    def layer_params(l):
        rw = dict(mu=rw_mu[l][None], w0=rw_w0[l][None], w2=rw_w2[l], a0=rw_a0[l][None],
                  a2=rw_a2[l], k_k=rw_k_k[l][None], k_a=rw_k_a[l][None],
                  r_k=rw_r_k[l].reshape(1, rw_w), gn_w=rw_gn_w[l][None], gn_b=rw_gn_b[l][None])
        dtb = _pad_lanes(ssm_dt_bias[l], 0)
        alog = _pad_lanes(ssm_a_log[l], 0)
        sm = dict(conv_w=ssm_conv_w[l], conv_b=ssm_conv_b[l][None], dtb_row=dtb,
                  dtb_col=dtb.reshape(SMALL_W, 1), alog_row=alog,
                  alog_col=alog.reshape(SMALL_W, 1),
                  d=jnp.repeat(ssm_d[l], ssm_hd)[None], norm_w=ssm_norm_w[l][None])
        gdtb = _pad_lanes(gdn_dt_bias[l], a_off)
        galog = _pad_lanes(gdn_a_log[l], a_off)
        gd = dict(conv_w=gdn_conv_w[l], dtb_row=gdtb, dtb_col=gdtb.reshape(SMALL_W, 1),
                  alog_row=galog, alog_col=galog.reshape(SMALL_W, 1),
                  norm_w=gdn_norm_w[l][None])
        return rw, sm, gd

    fw = final_norm_w[None]

    real_len = n_meta + seq
    seq_p = -(-real_len // CHUNK) * CHUNK
    n_pad = seq_p - real_len
    n_chunks = seq_p // CHUNK
    n_sub = max(d for d in range(1, MAX_CHUNKS_PER_STEP + 1) if n_chunks % d == 0)
    n_steps = n_chunks // n_sub
    meta = jnp.broadcast_to(meta_tokens.astype(F32)[None], (bp, n_meta, d_model))
    xp = jnp.concatenate([jnp.zeros((bp, n_pad, d_model), F32), meta, x_prompt], axis=1)
    xp = xp.reshape(bp * seq_p, d_model)
    tm_in = _row_tile(seq_p, 384)
    tm_out = _row_tile(seq_p, 384)

    p_states = [[] for _ in range(6)]
    y_p = None
    for l in range(depth):
        rw_p, sm_p, gd_p = layer_params(l)
        p_rw, rw_z, ssm_z, xbc, qkv, gdn_z, gate, small = _inproj(
            xp, norm_w[l][None], [w[l] for w in w_segs], tm=tm_in, n_pad=n_pad, seq_len=seq_p)
        rw_o, wkv, shift = _rwkv_prompt(p_rw, rw_z, rw_p, batch=bp, n_steps=n_steps,
                                        n_sub=n_sub, n_heads=rw_heads, head_dim=rw_hd)
        sm_o, ssm, sconv = _ssd_prompt(xbc, small, ssm_z, sm_p, batch=bp, n_steps=n_steps,
                                       n_sub=n_sub, n_heads=ssm_heads, head_dim=ssm_hd,
                                       n_groups=ssm_groups, d_state=ssm_state, n_pad=n_pad)
        gd_o, gdn, gconv = _gdn_prompt(qkv, small, gdn_z, gd_p, batch=bp, n_steps=n_steps,
                                       n_sub=n_sub, n_heads=gdn_heads, head_dim=gdn_hd,
                                       a_off=a_off, b_off=b_off)
        last = l == depth - 1
        res = _outproj(rw_o, sm_o, gd_o, gate, xp, w_r[l], w_s[l], w_g[l], w_o[l], fw,
                       tm=tm_out, with_final=last)
        xp = res[0]
        if last:
            y_p = res[1]
        for acc, s in zip(p_states, (wkv, shift.reshape(bp, rw_shift_w), ssm, sconv, gdn, gconv)):
            acc.append(s)
    y_prompt = y_p.reshape(bp, seq_p, d_model)[:, n_pad + n_meta:]

    xs = x_sample.reshape(bd, d_model)
    s_states = [[] for _ in range(6)]
    y_s = None
    for l in range(depth):
        rw_p, sm_p, gd_p = layer_params(l)
        p_rw, rw_z, ssm_z, xbc, qkv, gdn_z, gate, small = _inproj(
            xs, norm_w[l][None], [w[l] for w in w_segs], tm=bd, n_pad=0, seq_len=bd)
        rw_o, wkv = _rwkv_decode(p_rw, rw_z, state_rwkv_shift[l],
                                 state_rwkv_wkv[l].reshape(bd, rw_heads * rw_hd * rw_hd), rw_p,
                                 n_heads=rw_heads, head_dim=rw_hd)
        sm_o, sconv, ssm = _ssd_decode(xbc, small, ssm_z,
                                       jnp.swapaxes(state_ssm_conv[l], 0, 1),
                                       state_ssm[l].reshape(bd, ssm_heads * ssm_hd * ssm_state),
                                       sm_p, n_heads=ssm_heads, head_dim=ssm_hd,
                                       n_groups=ssm_groups, d_state=ssm_state)
        gd_o, gconv, gdn = _gdn_decode(qkv, small, gdn_z,
                                       jnp.swapaxes(state_gdn_conv[l], 0, 1),
                                       state_gdn[l].reshape(bd, gdn_heads * gdn_hd * gdn_hd),
                                       gd_p, n_heads=gdn_heads, head_dim=gdn_hd, a_off=a_off,
                                       b_off=b_off)
        last = l == depth - 1
        res = _outproj(rw_o, sm_o, gd_o, gate, xs, w_r[l], w_s[l], w_g[l], w_o[l], fw,
                       tm=bd, with_final=last)
        xs = res[0]
        if last:
            y_s = res[1]
        new = (wkv.reshape(bd, rw_heads, rw_hd, rw_hd), p_rw,
               ssm.reshape(bd, ssm_heads, ssm_hd, ssm_state), jnp.swapaxes(sconv, 0, 1),
               gdn.reshape(bd, gdn_heads, gdn_hd, gdn_hd), jnp.swapaxes(gconv, 0, 1))
        for acc, s in zip(s_states, new):
            acc.append(s)
    y_sample = y_s.reshape(bd, 1, d_model)

    return (y_prompt, y_sample) + tuple(jnp.stack(a) for a in p_states) \
        + tuple(jnp.stack(a) for a in s_states)
```

```python
import functools
import math

import jax
import jax.numpy as jnp
from jax import lax
from jax.experimental import pallas as pl
from jax.experimental.pallas import tpu as pltpu

F32 = jnp.float32
BF16 = jnp.bfloat16
HIGHEST = lax.Precision.HIGHEST

NORM_EPS = 1e-6
RW_GN_EPS = 64e-5
SSM_NORM_EPS = 1e-5
GDN_NORM_EPS = 1e-6
L2_EPS = 1e-6
CONV_W = 4
CHUNK = 64
MAX_CHUNKS_PER_STEP = 3
INV_BASE = 8
SEG_TILE = 256
SMALL_W = 128
CONV_HALO = 8
VMEM_LIMIT = 56 * 1024 * 1024


def _mm(a, b, exact=False):
    if exact:
        return jnp.dot(a, b, preferred_element_type=F32, precision=HIGHEST)
    return jnp.dot(a.astype(BF16), b.astype(BF16), preferred_element_type=F32)


def _mm_nt(a, b, exact=False):
    dn = (((1,), (1,)), ((), ()))
    if exact:
        return lax.dot_general(a, b, dn, preferred_element_type=F32, precision=HIGHEST)
    return lax.dot_general(a.astype(BF16), b.astype(BF16), dn, preferred_element_type=F32)


def _mm_tn(a, b, exact=False):
    dn = (((0,), (0,)), ((), ()))
    if exact:
        return lax.dot_general(a, b, dn, preferred_element_type=F32, precision=HIGHEST)
    return lax.dot_general(a.astype(BF16), b.astype(BF16), dn, preferred_element_type=F32)


def _sigmoid(x):
    return 1.0 / (1.0 + jnp.exp(-x))


def _silu(x):
    return x * _sigmoid(x)


def _softplus(x):
    return jnp.maximum(x, 0.0) + jnp.log(1.0 + jnp.exp(-jnp.abs(x)))


def _iota2(shape, dim):
    return lax.broadcasted_iota(jnp.int32, shape, dim)


def _seg_sum(x, seg_ones, split=False):
    w = seg_ones.shape[1]
    parts = []
    for i in range(x.shape[1] // w):
        xi = x[:, i * w:(i + 1) * w]
        x_hi = xi.astype(BF16)
        if split:
            x_lo = (xi - x_hi.astype(F32)).astype(BF16)
            parts.append(jnp.dot(jnp.concatenate([x_hi, x_lo], axis=1), seg_ones,
                                 preferred_element_type=F32))
        else:
            parts.append(jnp.dot(x_hi, seg_ones[:w], preferred_element_type=F32))
    return jnp.concatenate(parts, axis=1)


def _block_diag_pair(p):
    c = p.shape[0]
    lane = _iota2(p.shape, 1)
    pb = p.astype(BF16)
    zero = jnp.zeros_like(pb)
    return jnp.concatenate([jnp.where(lane < c, pb, zero), jnp.where(lane >= c, pb, zero)], axis=0)


def _mm_pair(x, p):
    return jnp.dot(x.astype(BF16), _block_diag_pair(p), preferred_element_type=F32)


def _unit_lower_inverse(l_mats):
    c = l_mats[0].shape[0]
    pairs = [jnp.concatenate(l_mats[i:i + 2], axis=1) for i in range(0, len(l_mats), 2)]
    ri = _iota2((c, 2 * c), 0)
    ci = _iota2((c, 2 * c), 1) & (c - 1)
    base_shift = int(math.log2(INV_BASE))
    same_base = (ri >> base_shift) == (ci >> base_shift)
    es = [jnp.where(same_base, l, 0.0) for l in pairs]
    ps = es
    n_terms = 1
    while 2 * n_terms < INV_BASE:
        ps = [_mm_pair(p, p) for p in ps]
        eps = [_mm_pair(e, p) for e, p in zip(es, ps)]
        es = [e + p + ep for e, p, ep in zip(es, ps, eps)]
        n_terms *= 2
    shift = base_shift
    while (1 << shift) < c:
        same_outer = (ri >> (shift + 1)) == (ci >> (shift + 1))
        other_inner = (ri >> shift) != (ci >> shift)
        l_off = [jnp.where(same_outer, jnp.where(other_inner, l, 0.0), 0.0) for l in pairs]
        ts = [lo + _mm_pair(lo, e) for lo, e in zip(l_off, es)]
        ets = [_mm_pair(e, t) for e, t in zip(es, ts)]
        es = [e + t + et for e, t, et in zip(es, ts, ets)]
        shift += 1
    out = []
    for e in es:
        out += [e[:, :c], e[:, c:]]
    return out


def _chunk_masks(c):
    ri = _iota2((c, c), 0)
    ci = _iota2((c, c), 1)
    return ri >= ci, ri > ci


def _block_tril(rows, c):
    shift = int(math.log2(c))
    ri = _iota2((rows, rows), 0)
    ci = _iota2((rows, rows), 1)
    return jnp.where((ri >> shift) == (ci >> shift),
                     jnp.where(ri >= ci, 1.0, 0.0), 0.0).astype(F32)


def _conv_chunk(ext_ref, u, w_ref, first_chunk):
    c = u.shape[0]

    @pl.when(first_chunk)
    def _():
        ext_ref[0:CONV_HALO, :] = jnp.zeros((CONV_HALO, u.shape[1]), F32)

    ext_ref[CONV_HALO:CONV_HALO + c, :] = u
    out = u * w_ref[CONV_W - 1:CONV_W, :]
    for i in range(CONV_W - 1):
        off = CONV_HALO - (CONV_W - 1) + i
        out = out + ext_ref[off:off + c, :] * w_ref[i:i + 1, :]
    tail = ext_ref[c:c + CONV_HALO, :]
    ext_ref[0:CONV_HALO, :] = tail
    return out, tail[CONV_HALO - (CONV_W - 1):, :]


def _inproj_body(x_ref, nw_ref, w_ref, *o_refs, n_pad, tiles_per_seq):
    x = x_ref[...]
    h = x * lax.rsqrt(jnp.mean(x * x, axis=-1, keepdims=True) + NORM_EPS) * nw_ref[...]
    if n_pad:
        first = lax.rem(pl.program_id(0), tiles_per_seq) == 0
        row = _iota2((x.shape[0], 1), 0)
        h = jnp.where(jnp.logical_and(first, row < n_pad), 0.0, h)
    hb = h.astype(BF16)
    off = 0
    for o_ref in o_refs:
        width = o_ref.shape[1]
        o_ref[...] = jnp.dot(hb, w_ref[:, off:off + width], preferred_element_type=F32)
        off += width


def _inproj(x, norm_w, w_all, widths, *, layer, tm, n_pad, seq_len):
    t, d = x.shape
    assert w_all.shape[2] == sum(widths)
    in_specs = [pl.BlockSpec((tm, d), lambda i: (i, 0)),
                pl.BlockSpec((1, d), lambda i: (0, 0)),
                pl.BlockSpec((None,) + w_all.shape[1:], lambda i: (layer, 0, 0),
                             pipeline_mode=pl.Buffered(1))]
    out_specs = [pl.BlockSpec((tm, w), lambda i: (i, 0)) for w in widths]
    out_shape = [jax.ShapeDtypeStruct((t, w), F32) for w in widths]
    body = functools.partial(_inproj_body, n_pad=n_pad,
                             tiles_per_seq=(seq_len // tm) if n_pad else 1)
    return pl.pallas_call(
        body, grid=(t // tm,), in_specs=in_specs, out_specs=out_specs, out_shape=out_shape,
        compiler_params=pltpu.CompilerParams(dimension_semantics=("arbitrary",),
                                             vmem_limit_bytes=VMEM_LIMIT),
        name="inproj",
    )(x, norm_w, w_all)


def _outproj_body(rw_ref, sm_ref, gd_ref, gate_ref, x_ref, wr_ref, ws_ref, wg_ref, wo_ref,
                  fw_ref, o_ref, *maybe_y_ref, d_model):
    merged = None
    for b, (y_ref, w_ref) in enumerate(((rw_ref, wr_ref), (sm_ref, ws_ref), (gd_ref, wg_ref))):
        g = _sigmoid(gate_ref[:, b * d_model:(b + 1) * d_model])
        term = g * jnp.dot(y_ref[...].astype(BF16), w_ref[...], preferred_element_type=F32)
        merged = term if merged is None else merged + term
    x_new = x_ref[...] + jnp.dot(merged.astype(BF16), wo_ref[...], preferred_element_type=F32)
    o_ref[...] = x_new
    if maybe_y_ref:
        ms = jnp.mean(x_new * x_new, axis=-1, keepdims=True)
        maybe_y_ref[0][...] = x_new * lax.rsqrt(ms + NORM_EPS) * fw_ref[...]


def _outproj(rw, sm, gd, gate, x, w_r, w_s, w_g, w_o, final_w, *, tm, with_final):
    t, d = x.shape
    grid = (t // tm,)
    row = lambda i: (i, 0)
    fixed = lambda i: (0, 0)
    in_specs = [pl.BlockSpec((tm, rw.shape[1]), row), pl.BlockSpec((tm, sm.shape[1]), row),
                pl.BlockSpec((tm, gd.shape[1]), row), pl.BlockSpec((tm, gate.shape[1]), row),
                pl.BlockSpec((tm, d), row),
                pl.BlockSpec(w_r.shape, fixed), pl.BlockSpec(w_s.shape, fixed),
                pl.BlockSpec(w_g.shape, fixed), pl.BlockSpec(w_o.shape, fixed),
                pl.BlockSpec((1, d), fixed)]
    n_out = 2 if with_final else 1
    out_specs = [pl.BlockSpec((tm, d), row)] * n_out
    out_shape = [jax.ShapeDtypeStruct((t, d), F32)] * n_out
    return pl.pallas_call(
        functools.partial(_outproj_body, d_model=d),
        grid=grid, in_specs=in_specs, out_specs=out_specs, out_shape=out_shape,
        compiler_params=pltpu.CompilerParams(dimension_semantics=("arbitrary",),
                                             vmem_limit_bytes=VMEM_LIMIT),
        name="outproj",
    )(rw, sm, gd, gate, x, w_r, w_s, w_g, w_o, final_w)


def _rwkv_pre(p, shifted, mu, w0, w2, a0, a2, k_k, k_a, width, lora):
    u = p + (shifted - p) * mu
    r = u[:, :width]
    k = u[:, width:2 * width]
    v = u[:, 2 * width:3 * width]
    wd = u[:, 3 * width:3 * width + lora]
    ad = u[:, 3 * width + lora:]
    w = -_softplus(-(w0 + _mm(jnp.tanh(wd), w2))) - 0.5
    log_decay = -jnp.exp(w)
    a = _sigmoid(a0 + _mm(ad, a2))
    kk_raw = k * k_k
    k_mod = k * (1.0 + (a - 1.0) * k_a)
    return r, k_mod, v, log_decay, a, kk_raw


def _l2norm(x):
    return x * lax.rsqrt(jnp.sum(x * x, axis=-1, keepdims=True) + L2_EPS)


def _rwkv_post_head(y, r, k_mod, v, z, r_k, gn_w, gn_b):
    mean = jnp.mean(y, axis=-1, keepdims=True)
    var = jnp.mean(jnp.square(y - mean), axis=-1, keepdims=True)
    yn = (y - mean) * lax.rsqrt(var + RW_GN_EPS) * gn_w + gn_b
    bonus = jnp.sum(r * k_mod * r_k, axis=-1, keepdims=True) * v
    return (yn + bonus) * _silu(z)


def _rwkv_prompt_body(p_ref, z_ref, ones_ref, mu_ref, w0_ref, w2_ref, a0_ref, a2_ref, kk_ref,
                      ka_ref, rk_ref, gnw_ref, gnb_ref, o_ref, wkv_ref, shift_ref, prev_scr,
                      *, n_heads, head_dim, n_sub):
    width = n_heads * head_dim
    lora = w2_ref.shape[0]
    p = p_ref[...]
    rows = p.shape[0]
    c = rows // n_sub

    @pl.when(pl.program_id(1) == 0)
    def _():
        wkv_ref[...] = jnp.zeros(wkv_ref.shape, F32)
        prev_scr[...] = jnp.zeros(prev_scr.shape, F32)

    row = _iota2((rows, 1), 0)
    shifted = jnp.where(row == 0, prev_scr[...], pltpu.roll(p, 1, 0))
    last_row = p[rows - 1:rows, :]
    prev_scr[...] = last_row
    shift_ref[0] = last_row

    r, k_mod, v, log_decay, a, kk_raw = _rwkv_pre(
        p, shifted, mu_ref[...], w0_ref[...], w2_ref[...], a0_ref[...], a2_ref[...],
        kk_ref[...], ka_ref[...], width, lora)
    cum = _mm(_block_tril(rows, c), log_decay, exact=True)
    e_pos = jnp.exp(cum)
    e_neg = jnp.exp(-cum)
    e_prev = jnp.exp(cum - log_decay)
    z = z_ref[...]
    incl, strict = _chunk_masks(c)

    hs = range(n_heads)
    sls = [slice(h * head_dim, (h + 1) * head_dim) for h in hs]
    units = [(j, h) for j in range(n_sub) for h in hs]
    rsl = [slice(j * c, (j + 1) * c) for j in range(n_sub)]
    seg_ones = ones_ref[...]
    kk = kk_raw * lax.rsqrt(_seg_sum(kk_raw * kk_raw, seg_ones) + L2_EPS)
    a_all = -kk * e_prev
    b_all = kk * a * e_neg
    k_all = k_mod * e_neg
    r_all = r * e_pos
    a_t = [a_all[:, sl] for sl in sls]
    b_t = [b_all[:, sl] for sl in sls]
    k_t = [k_all[:, sl] for sl in sls]
    r_t = [r_all[:, sl] for sl in sls]
    v_h = [v[:, sl] for sl in sls]

    sc = {u: _mm_nt(jnp.concatenate([a_t[u[1]][rsl[u[0]]], r_t[u[1]][rsl[u[0]]]], axis=0),
                    jnp.concatenate([b_t[u[1]][rsl[u[0]]], k_t[u[1]][rsl[u[0]]]], axis=0))
          for u in units}
    l_ab = {u: jnp.where(strict, sc[u][:c, :c], 0.0) for u in units}
    l_ak = {u: jnp.where(strict, sc[u][:c, c:], 0.0) for u in units}
    m_rb = {u: jnp.where(incl, sc[u][c:, :c], 0.0) for u in units}
    m_rk = {u: jnp.where(incl, sc[u][c:, c:], 0.0) for u in units}
    lv = {u: _mm(jnp.concatenate([l_ak[u], m_rk[u]], axis=0), v_h[u[1]][rsl[u[0]]]) for u in units}
    inv = dict(zip(units, _unit_lower_inverse([l_ab[u] for u in units])))
    rhs = {u: jnp.concatenate([a_t[u[1]][rsl[u[0]]], lv[u][:c]], axis=1) for u in units}
    wu = {u: rhs[u] + _mm(inv[u], rhs[u]) for u in units}
    d_end = {(j, h): e_pos[(j + 1) * c - 1:(j + 1) * c, sls[h]] for j, h in units}
    qy = {u: _mm(m_rb[u], wu[u]) for u in units}
    mn = {u: _mm_tn(wu[u], b_t[u[1]][rsl[u[0]]] * d_end[u]) for u in units}
    n2 = {u: _mm_tn(v_h[u[1]][rsl[u[0]]], k_t[u[1]][rsl[u[0]]] * d_end[u]) for u in units}

    states = [wkv_ref[0, h] for h in hs]
    ys = [[] for _ in hs]
    for j in range(n_sub):
        y_dot = [_mm_nt(r_t[h][rsl[j]] + qy[(j, h)][:, :c], states[h]) for h in hs]
        s_dot = [_mm(states[h], mn[(j, h)][:c]) for h in hs]
        for h in hs:
            ys[h].append(y_dot[h] + qy[(j, h)][:, c:] + lv[(j, h)][c:])
            states[h] = states[h] * d_end[(j, h)] + s_dot[h] + (mn[(j, h)][c:] + n2[(j, h)])
    for h in hs:
        wkv_ref[0, h] = states[h]
    y = jnp.concatenate([jnp.concatenate(ys[h], axis=0) for h in hs], axis=-1)
    inv_n = 1.0 / head_dim
    dev = y - _seg_sum(y, seg_ones) * inv_n
    var = _seg_sum(dev * dev, seg_ones) * inv_n
    yn = dev * lax.rsqrt(var + RW_GN_EPS) * gnw_ref[...] + gnb_ref[...]
    bonus = _seg_sum(r * k_mod * rk_ref[...], seg_ones, split=True) * v
    o_ref[...] = (yn + bonus) * _silu(z)


def _rwkv_prompt(p_rw, z, prm, *, batch, n_steps, n_sub, n_heads, head_dim):
    t, shift_w = p_rw.shape
    width = n_heads * head_dim
    rows = n_sub * CHUNK
    row = lambda b, c: (b * n_steps + c, 0)
    fixed = lambda b, c: (0, 0)
    names = ("mu", "w0", "w2", "a0", "a2", "k_k", "k_a", "r_k", "gn_w", "gn_b")
    params = [_segment_ones(SEG_TILE, head_dim)] + [prm[n] for n in names]
    in_specs = [pl.BlockSpec((rows, shift_w), row), pl.BlockSpec((rows, width), row)]
    in_specs += [pl.BlockSpec(a.shape, fixed) for a in params]
    out_specs = [pl.BlockSpec((rows, width), row),
                 pl.BlockSpec((1, n_heads, head_dim, head_dim), lambda b, c: (b, 0, 0, 0)),
                 pl.BlockSpec((1, 1, shift_w), lambda b, c: (b, 0, 0))]
    out_shape = [jax.ShapeDtypeStruct((t, width), F32),
                 jax.ShapeDtypeStruct((batch, n_heads, head_dim, head_dim), F32),
                 jax.ShapeDtypeStruct((batch, 1, shift_w), F32)]
    return pl.pallas_call(
        functools.partial(_rwkv_prompt_body, n_heads=n_heads, head_dim=head_dim, n_sub=n_sub),
        grid=(batch, n_steps), in_specs=in_specs, out_specs=out_specs, out_shape=out_shape,
        scratch_shapes=[pltpu.VMEM((1, shift_w), F32)],
        compiler_params=pltpu.CompilerParams(dimension_semantics=("arbitrary", "arbitrary"),
                                             vmem_limit_bytes=VMEM_LIMIT),
        name="rwkv_prompt",
    )(p_rw, z, *params)


def _ssm_gated_norm(y, z, norm_w, n_groups):
    g = y * _silu(z)
    gw = g.shape[1] // n_groups
    parts = []
    for i in range(n_groups):
        gi = g[:, i * gw:(i + 1) * gw]
        parts.append(gi * lax.rsqrt(jnp.mean(gi * gi, axis=-1, keepdims=True) + SSM_NORM_EPS))
    return jnp.concatenate(parts, axis=-1) * norm_w


def _ssd_prompt_body(xbc_ref, small_ref, z_ref, cw_ref, cb_ref, dtb_row_ref, dtb_col_ref,
                     alog_row_ref, alog_col_ref, d_ref, nw_ref, o_ref, h_ref, conv_ref, ext_scr,
                     *, n_heads, head_dim, n_groups, d_state, n_pad, n_sub):
    first = pl.program_id(1) == 0
    width = n_heads * head_dim
    hpg = n_heads // n_groups
    gw = hpg * head_dim
    x_pre = xbc_ref[...]
    rows = x_pre.shape[0]
    c = rows // n_sub
    rsl = [slice(j * c, (j + 1) * c) for j in range(n_sub)]

    @pl.when(first)
    def _():
        h_ref[...] = jnp.zeros(h_ref.shape, F32)

    conv, tail = _conv_chunk(ext_scr, x_pre, cw_ref, first)
    conv_ref[0] = tail
    xbc = _silu(conv + cb_ref[...])
    xs = xbc[:, :width]
    b_all = xbc[:, width:width + n_groups * d_state]
    c_all = xbc[:, width + n_groups * d_state:]

    small = small_ref[...]
    lane = _iota2((1, SMALL_W), 1)
    sub = _iota2((SMALL_W, 1), 0)
    dt = _softplus(small + dtb_row_ref[...])
    dt_t = [_softplus(small[sl].T + dtb_col_ref[...]) for sl in rsl]
    if n_pad:
        assert n_pad <= c
        not_first = jnp.logical_not(first)
        dt = jnp.where(jnp.logical_or(not_first, _iota2((rows, 1), 0) >= n_pad), dt, 0.0)
        dt_t[0] = jnp.where(jnp.logical_or(not_first, _iota2((1, c), 1) >= n_pad), dt_t[0], 0.0)
    a_row = jnp.where(lane < n_heads, -jnp.exp(alog_row_ref[...]), 0.0)
    a_col = jnp.where(sub < n_heads, -jnp.exp(alog_col_ref[...]), 0.0)
    incl, _ = _chunk_masks(c)
    triu = jnp.where(_iota2((c, c), 0) <= _iota2((c, c), 1), 1.0, 0.0).astype(F32)
    acs = _mm(_block_tril(rows, c), dt * a_row, exact=True)
    acs_t = [_mm(t * a_col, triu, exact=True) for t in dt_t]
    expand = jnp.where((_iota2((SMALL_W, width), 1) // head_dim) == _iota2((SMALL_W, width), 0),
                       1.0, 0.0).astype(F32)
    wide = _mm(jnp.concatenate([acs, dt], axis=0), expand, exact=True)
    acs_w = wide[:rows]
    xdt = xs * wide[rows:]
    e_acs = jnp.exp(acs_w)

    hs = range(n_heads)
    gs = range(n_groups)
    js = range(n_sub)
    b_g = {(j, g): b_all[rsl[j], g * d_state:(g + 1) * d_state] for j in js for g in gs}
    c_g = {(j, g): c_all[rsl[j], g * d_state:(g + 1) * d_state] for j in js for g in gs}
    a_last = {j: acs_w[(j + 1) * c - 1:(j + 1) * c] for j in js}
    xdt_end = {j: xdt[rsl[j]] * jnp.exp(a_last[j] - acs_w[rsl[j]]) for j in js}

    cb = {jg: _mm_nt(c_g[jg], b_g[jg]) for jg in b_g}
    seg = {(j, h): jnp.exp(jnp.where(incl, acs[rsl[j], h:h + 1] - acs_t[j][h:h + 1, :], -jnp.inf))
           for j in js for h in hs}
    y_diag = {(j, h): _mm(cb[(j, h // hpg)] * seg[(j, h)],
                          xdt[rsl[j], h * head_dim:(h + 1) * head_dim])
              for j in js for h in hs}
    st = {(j, g): _mm_tn(xdt_end[j][:, g * gw:(g + 1) * gw], b_g[(j, g)]) for j in js for g in gs}

    states = [h_ref[0, g * hpg:(g + 1) * hpg].reshape(gw, d_state) for g in gs]
    ys = []
    for j in js:
        y_off = [_mm_nt(c_g[(j, g)], states[g]) for g in gs]
        ys.append(jnp.concatenate(y_off, axis=-1) * e_acs[rsl[j]]
                  + jnp.concatenate([y_diag[(j, h)] for h in hs], axis=-1))
        for g in gs:
            scaled = [states[g][i * head_dim:(i + 1) * head_dim]
                      * jnp.exp(acs[(j + 1) * c - 1:(j + 1) * c, g * hpg + i:g * hpg + i + 1])
                      for i in range(hpg)]
            states[g] = jnp.concatenate(scaled, axis=0) + st[(j, g)]
    for g in gs:
        h_ref[0, g * hpg:(g + 1) * hpg] = states[g].reshape(hpg, head_dim, d_state)
    y_all = jnp.concatenate(ys, axis=0) + xs * d_ref[...]
    o_ref[...] = _ssm_gated_norm(y_all, z_ref[...], nw_ref[...], n_groups)


def _ssd_prompt(xbc, small, z, prm, *, batch, n_steps, n_sub, n_heads, head_dim, n_groups,
                d_state, n_pad):
    t, conv_ch = xbc.shape
    width = n_heads * head_dim
    rows = n_sub * CHUNK
    row = lambda b, c: (b * n_steps + c, 0)
    fixed = lambda b, c: (0, 0)
    names = ("conv_w", "conv_b", "dtb_row", "dtb_col", "alog_row", "alog_col", "d", "norm_w")
    params = [prm[n] for n in names]
    in_specs = [pl.BlockSpec((rows, conv_ch), row), pl.BlockSpec((rows, SMALL_W), row),
                pl.BlockSpec((rows, width), row)]
    in_specs += [pl.BlockSpec(a.shape, fixed) for a in params]
    out_specs = [pl.BlockSpec((rows, width), row),
                 pl.BlockSpec((1, n_heads, head_dim, d_state), lambda b, c: (b, 0, 0, 0)),
                 pl.BlockSpec((1, CONV_W - 1, conv_ch), lambda b, c: (b, 0, 0))]
    out_shape = [jax.ShapeDtypeStruct((t, width), F32),
                 jax.ShapeDtypeStruct((batch, n_heads, head_dim, d_state), F32),
                 jax.ShapeDtypeStruct((batch, CONV_W - 1, conv_ch), F32)]
    return pl.pallas_call(
        functools.partial(_ssd_prompt_body, n_heads=n_heads, head_dim=head_dim,
                          n_groups=n_groups, d_state=d_state, n_pad=n_pad, n_sub=n_sub),
        grid=(batch, n_steps), in_specs=in_specs, out_specs=out_specs, out_shape=out_shape,
        scratch_shapes=[pltpu.VMEM((rows + CONV_HALO, conv_ch), F32)],
        compiler_params=pltpu.CompilerParams(dimension_semantics=("arbitrary", "arbitrary"),
                                             vmem_limit_bytes=VMEM_LIMIT),
        name="ssd_prompt",
    )(xbc, small, z, *params)


def _gdn_post_head(o, z, norm_w):
    return o * lax.rsqrt(jnp.mean(o * o, axis=-1, keepdims=True) + GDN_NORM_EPS) * norm_w * _silu(z)


def _gdn_prompt_body(qkv_ref, small_ref, z_ref, cw_ref, dtb_row_ref, dtb_col_ref, alog_row_ref,
                     alog_col_ref, nw_ref, o_ref, s_ref, conv_ref, ext_scr,
                     *, n_heads, head_dim, a_off, b_off, n_sub):
    first = pl.program_id(1) == 0
    width = n_heads * head_dim
    x_pre = qkv_ref[...]
    rows = x_pre.shape[0]
    c = rows // n_sub

    @pl.when(first)
    def _():
        s_ref[...] = jnp.zeros(s_ref.shape, F32)

    conv, tail = _conv_chunk(ext_scr, x_pre, cw_ref, first)
    conv_ref[0] = tail
    qkv = _silu(conv)
    small = small_ref[...]
    rsl = [slice(j * c, (j + 1) * c) for j in range(n_sub)]
    small_t = [small[sl].T for sl in rsl]
    lane = _iota2((1, SMALL_W), 1)
    sub = _iota2((SMALL_W, 1), 0)
    g = jnp.where(jnp.logical_and(lane >= a_off, lane < a_off + n_heads),
                  -jnp.exp(alog_row_ref[...]) * _softplus(small + dtb_row_ref[...]), 0.0)
    g_t = [jnp.where(jnp.logical_and(sub >= a_off, sub < a_off + n_heads),
                     -jnp.exp(alog_col_ref[...]) * _softplus(st + dtb_col_ref[...]), 0.0)
           for st in small_t]
    beta = _sigmoid(small)
    incl, strict = _chunk_masks(c)
    triu = jnp.where(_iota2((c, c), 0) <= _iota2((c, c), 1), 1.0, 0.0).astype(F32)
    gcs = _mm(_block_tril(rows, c), g, exact=True)
    gcs_t = [_mm(gt, triu, exact=True) for gt in g_t]
    z = z_ref[...]
    scale = head_dim ** -0.5

    hs = range(n_heads)
    sls = [slice(h * head_dim, (h + 1) * head_dim) for h in hs]
    units = [(j, h) for j in range(n_sub) for h in hs]
    q_h = [_l2norm(qkv[:, sl]) * scale for sl in sls]
    k_h = [_l2norm(qkv[:, width + h * head_dim:width + (h + 1) * head_dim]) for h in hs]
    v_h = [qkv[:, 2 * width + h * head_dim:2 * width + (h + 1) * head_dim] for h in hs]
    g_c = {(j, h): gcs[rsl[j], a_off + h:a_off + h + 1] for j, h in units}
    g_r = {(j, h): gcs_t[j][a_off + h:a_off + h + 1, :] for j, h in units}
    g_last = {(j, h): gcs[(j + 1) * c - 1:(j + 1) * c, a_off + h:a_off + h + 1] for j, h in units}
    b_c = {(j, h): beta[rsl[j], b_off + h:b_off + h + 1] for j, h in units}
    q = {(j, h): q_h[h][rsl[j]] for j, h in units}
    k = {(j, h): k_h[h][rsl[j]] for j, h in units}
    v = {(j, h): v_h[h][rsl[j]] for j, h in units}

    decay = {u: jnp.exp(jnp.where(incl, g_c[u] - g_r[u], -jnp.inf)) for u in units}
    kb = {u: k[u] * b_c[u] for u in units}
    e_c = {u: jnp.exp(g_c[u]) for u in units}
    sc = {u: _mm_nt(jnp.concatenate([kb[u], q[u]], axis=0), k[u]) for u in units}
    a_mat = {u: jnp.where(strict, sc[u][:c] * decay[u], 0.0) for u in units}
    qk = {u: jnp.where(incl, sc[u][c:] * decay[u], 0.0) for u in units}
    inv = dict(zip(units, _unit_lower_inverse([-a_mat[u] for u in units])))
    rhs = {u: jnp.concatenate([v[u] * b_c[u], kb[u] * e_c[u]], axis=1) for u in units}
    uw = {u: rhs[u] + _mm(inv[u], rhs[u]) for u in units}
    qo = {u: _mm(qk[u], uw[u]) for u in units}
    mn = {u: _mm_tn(k[u] * jnp.exp(g_last[u] - g_c[u]), uw[u]) for u in units}

    states = [s_ref[0, h] for h in hs]
    os_ = [[] for _ in hs]
    for j in range(n_sub):
        o_dot = [_mm(q[(j, h)] * e_c[(j, h)] - qo[(j, h)][:, head_dim:], states[h]) for h in hs]
        s_dot = [_mm(mn[(j, h)][:, head_dim:], states[h]) for h in hs]
        for h in hs:
            os_[h].append(o_dot[h] + qo[(j, h)][:, :head_dim])
            states[h] = (states[h] * jnp.exp(g_last[(j, h)]) - s_dot[h]
                         + mn[(j, h)][:, :head_dim])
    for h in hs:
        s_ref[0, h] = states[h]
    outs = [_gdn_post_head(jnp.concatenate(os_[h], axis=0), z[:, sls[h]], nw_ref[...])
            for h in hs]
    o_ref[...] = jnp.concatenate(outs, axis=-1)


def _gdn_prompt(qkv, small, z, prm, *, batch, n_steps, n_sub, n_heads, head_dim, a_off, b_off):
    t, conv_ch = qkv.shape
    width = n_heads * head_dim
    rows = n_sub * CHUNK
    row = lambda b, c: (b * n_steps + c, 0)
    fixed = lambda b, c: (0, 0)
    names = ("conv_w", "dtb_row", "dtb_col", "alog_row", "alog_col", "norm_w")
    params = [prm[n] for n in names]
    in_specs = [pl.BlockSpec((rows, conv_ch), row), pl.BlockSpec((rows, SMALL_W), row),
                pl.BlockSpec((rows, width), row)]
    in_specs += [pl.BlockSpec(a.shape, fixed) for a in params]
    out_specs = [pl.BlockSpec((rows, width), row),
                 pl.BlockSpec((1, n_heads, head_dim, head_dim), lambda b, c: (b, 0, 0, 0)),
                 pl.BlockSpec((1, CONV_W - 1, conv_ch), lambda b, c: (b, 0, 0))]
    out_shape = [jax.ShapeDtypeStruct((t, width), F32),
                 jax.ShapeDtypeStruct((batch, n_heads, head_dim, head_dim), F32),
                 jax.ShapeDtypeStruct((batch, CONV_W - 1, conv_ch), F32)]
    return pl.pallas_call(
        functools.partial(_gdn_prompt_body, n_heads=n_heads, head_dim=head_dim,
                          a_off=a_off, b_off=b_off, n_sub=n_sub),
        grid=(batch, n_steps), in_specs=in_specs, out_specs=out_specs, out_shape=out_shape,
        scratch_shapes=[pltpu.VMEM((rows + CONV_HALO, conv_ch), F32)],
        compiler_params=pltpu.CompilerParams(dimension_semantics=("arbitrary", "arbitrary"),
                                             vmem_limit_bytes=VMEM_LIMIT),
        name="gdn_prompt",
    )(qkv, small, z, *params)


def _conv_step(u, conv_ref, convo_ref, w_ref):
    ch = u.shape[1]
    out = u * w_ref[CONV_W - 1:CONV_W, :]
    for i in range(CONV_W - 1):
        out = out + conv_ref[:, i * ch:(i + 1) * ch] * w_ref[i:i + 1, :]
    for i in range(CONV_W - 2):
        convo_ref[:, i * ch:(i + 1) * ch] = conv_ref[:, (i + 1) * ch:(i + 2) * ch]
    convo_ref[:, (CONV_W - 2) * ch:] = u
    return out


def _rwkv_decode_body(p_ref, z_ref, prev_ref, s_ref, mu_ref, w0_ref, w2_ref, a0_ref, a2_ref,
                      kk_ref, ka_ref, rk_ref, gnw_ref, gnb_ref, o_ref, so_ref,
                      t_scr, y_scr, row_scr, st_scr, *, n_heads, head_dim):
    h = pl.program_id(0)
    width = n_heads * head_dim
    lora = w2_ref.shape[0]
    bsz = p_ref.shape[0]

    @pl.when(h == 0)
    def _():
        r, k_mod, v, log_decay, a, kk_raw = _rwkv_pre(
            p_ref[...], prev_ref[...], mu_ref[...], w0_ref[...], w2_ref[...], a0_ref[...],
            a2_ref[...], kk_ref[...], ka_ref[...], width, lora)
        kk = jnp.concatenate(
            [_l2norm(kk_raw[:, i * head_dim:(i + 1) * head_dim]) for i in range(n_heads)], axis=-1)
        row_scr[0] = r
        row_scr[1] = k_mod
        row_scr[2] = v
        for i, arr in enumerate((r, jnp.exp(log_decay), k_mod, v, -kk, kk * a)):
            t_scr[i] = arr.T.reshape(n_heads, head_dim, bsz)

    st_scr[...] = s_ref[...].T.reshape(head_dim, head_dim, bsz)
    r_t = t_scr[0, h]
    w_t = t_scr[1, h]
    k_t = t_scr[2, h]
    a_t = t_scr[4, h]
    b_t = t_scr[5, h]

    def step(i, carry):
        s_v = st_scr[i]
        sa = jnp.sum(s_v * a_t, axis=0, keepdims=True)
        v_i = t_scr[3, h, pl.ds(i, 1), :]
        s_new = s_v * w_t + sa * b_t + v_i * k_t
        st_scr[i] = s_new
        y_scr[h, pl.ds(i, 1), :] = jnp.sum(s_new * r_t, axis=0, keepdims=True)
        return carry

    lax.fori_loop(0, head_dim, step, 0)
    so_ref[...] = st_scr[...].reshape(head_dim * head_dim, bsz).T

    @pl.when(h == n_heads - 1)
    def _():
        y = y_scr[...].reshape(width, bsz).T
        r, k_mod, v, z = row_scr[0], row_scr[1], row_scr[2], z_ref[...]
        outs = []
        for i in range(n_heads):
            sl = slice(i * head_dim, (i + 1) * head_dim)
            outs.append(_rwkv_post_head(y[:, sl], r[:, sl], k_mod[:, sl], v[:, sl], z[:, sl],
                                        rk_ref[:, sl], gnw_ref[:, sl], gnb_ref[:, sl]))
        o_ref[...] = jnp.concatenate(outs, axis=-1)


def _layer_state_call(body, name, row_inputs, layer_inputs, state_all, new_all, params, *,
                      layer, n_heads, state_cols, n_row_out, scratch_shapes):
    bsz = row_inputs[0].shape[0]
    fixed = lambda h: (0, 0)
    at_layer = lambda h: (layer, 0)
    in_specs = [pl.BlockSpec(a.shape, fixed) for a in row_inputs]
    in_specs += [pl.BlockSpec((bsz, a.shape[1]), at_layer) for a in layer_inputs]
    in_specs += [pl.BlockSpec((bsz, state_cols), lambda h: (layer, h))]
    in_specs += [pl.BlockSpec(a.shape, fixed) for a in params]
    in_specs.append(pl.BlockSpec(memory_space=pl.ANY))
    operands = list(row_inputs) + list(layer_inputs) + [state_all] + list(params) + [new_all]
    n_in = len(operands)
    aliases = {n_in - 1: len(n_row_out)}
    inner = body

    def body(*refs):
        return inner(*refs[:n_in - 1], *refs[n_in:])

    out_specs = [pl.BlockSpec((bsz, w), fixed) for w in n_row_out]
    out_specs += [pl.BlockSpec((bsz, state_cols), lambda h: (layer, h))]
    out_shape = [jax.ShapeDtypeStruct((bsz, w), F32) for w in n_row_out]
    out_shape += [jax.ShapeDtypeStruct(state_all.shape, F32)]
    return pl.pallas_call(
        body, grid=(n_heads,), in_specs=in_specs, out_specs=out_specs, out_shape=out_shape,
        scratch_shapes=scratch_shapes, input_output_aliases=aliases,
        compiler_params=pltpu.CompilerParams(dimension_semantics=("arbitrary",),
                                             vmem_limit_bytes=VMEM_LIMIT),
        name=name,
    )(*operands)


def _rwkv_decode(p_rw, z, prev_all, state_all, new_all, prm, *, layer, n_heads, head_dim):
    bsz = p_rw.shape[0]
    width = n_heads * head_dim
    names = ("mu", "w0", "w2", "a0", "a2", "k_k", "k_a", "r_k", "gn_w", "gn_b")
    return _layer_state_call(
        functools.partial(_rwkv_decode_body, n_heads=n_heads, head_dim=head_dim), "rwkv_decode",
        [p_rw, z], [prev_all], state_all, new_all, [prm[n] for n in names],
        layer=layer, n_heads=n_heads, state_cols=head_dim * head_dim, n_row_out=[width],
        scratch_shapes=[pltpu.VMEM((6, n_heads, head_dim, bsz), F32),
                        pltpu.VMEM((n_heads, head_dim, bsz), F32),
                        pltpu.VMEM((3, bsz, width), F32),
                        pltpu.VMEM((head_dim, head_dim, bsz), F32)])


def _ssd_decode_body(xbc_ref, small_ref, z_ref, conv_ref, s_ref, cw_ref, cb_ref, dtb_row_ref,
                     alog_row_ref, d_ref, nw_ref, o_ref, convo_ref, so_ref,
                     xt_scr, bc_scr, dec_scr, y_scr, st_scr,
                     *, n_heads, head_dim, n_groups, d_state):
    h = pl.program_id(0)
    width = n_heads * head_dim
    bsz = xbc_ref.shape[0]
    hpg = n_heads // n_groups

    @pl.when(h == 0)
    def _():
        conv = _conv_step(xbc_ref[...], conv_ref, convo_ref, cw_ref)
        xbc = _silu(conv + cb_ref[...])
        lane = _iota2((1, SMALL_W), 1)
        dt = _softplus(small_ref[...] + dtb_row_ref[...])
        a_row = jnp.where(lane < n_heads, -jnp.exp(alog_row_ref[...]), 0.0)
        dec_scr[0] = jnp.exp(dt * a_row).T
        dec_scr[1] = dt.T
        xs = xbc[:, :width]
        xt_scr[0] = xs.T.reshape(n_heads, head_dim, bsz)
        xt_scr[1] = (xs * d_ref[...]).T.reshape(n_heads, head_dim, bsz)
        bc_scr[...] = xbc[:, width:].T.reshape(2 * n_groups, d_state, bsz)

    st_scr[...] = s_ref[...].T.reshape(head_dim, d_state, bsz)
    g = h // hpg
    b_t = bc_scr[g]
    c_t = bc_scr[n_groups + g]
    decay = dec_scr[0, pl.ds(h, 1), :]
    dt_h = dec_scr[1, pl.ds(h, 1), :]

    def step(i, carry):
        x_i = xt_scr[0, h, pl.ds(i, 1), :]
        s_new = st_scr[i] * decay + (dt_h * x_i) * b_t
        st_scr[i] = s_new
        y_scr[h, pl.ds(i, 1), :] = (jnp.sum(s_new * c_t, axis=0, keepdims=True)
                                    + xt_scr[1, h, pl.ds(i, 1), :])
        return carry

    lax.fori_loop(0, head_dim, step, 0)
    so_ref[...] = st_scr[...].reshape(head_dim * d_state, bsz).T

    @pl.when(h == n_heads - 1)
    def _():
        y = y_scr[...].reshape(width, bsz).T
        o_ref[...] = _ssm_gated_norm(y, z_ref[...], nw_ref[...], n_groups)


def _ssd_decode(xbc, small, z, conv_all, state_all, new_all, prm, *, layer, n_heads, head_dim,
                n_groups, d_state):
    bsz = xbc.shape[0]
    width = n_heads * head_dim
    names = ("conv_w", "conv_b", "dtb_row", "alog_row", "d", "norm_w")
    return _layer_state_call(
        functools.partial(_ssd_decode_body, n_heads=n_heads, head_dim=head_dim,
                          n_groups=n_groups, d_state=d_state), "ssd_decode",
        [xbc, small, z], [conv_all], state_all, new_all, [prm[n] for n in names],
        layer=layer, n_heads=n_heads, state_cols=head_dim * d_state,
        n_row_out=[width, conv_all.shape[1]],
        scratch_shapes=[pltpu.VMEM((2, n_heads, head_dim, bsz), F32),
                        pltpu.VMEM((2 * n_groups, d_state, bsz), F32),
                        pltpu.VMEM((2, SMALL_W, bsz), F32),
                        pltpu.VMEM((n_heads, head_dim, bsz), F32),
                        pltpu.VMEM((head_dim, d_state, bsz), F32)])


def _gdn_decode_body(qkv_ref, small_ref, z_ref, conv_ref, s_ref, cw_ref, dtb_row_ref,
                     alog_row_ref, nw_ref, o_ref, convo_ref, so_ref,
                     t_scr, gate_scr, y_scr, st_scr, *, n_heads, head_dim, a_off, b_off):
    h = pl.program_id(0)
    width = n_heads * head_dim
    bsz = qkv_ref.shape[0]

    @pl.when(h == 0)
    def _():
        qkv = _silu(_conv_step(qkv_ref[...], conv_ref, convo_ref, cw_ref))
        scale = head_dim ** -0.5
        q = jnp.concatenate([_l2norm(qkv[:, i * head_dim:(i + 1) * head_dim]) * scale
                             for i in range(n_heads)], axis=-1)
        k = jnp.concatenate([_l2norm(qkv[:, width + i * head_dim:width + (i + 1) * head_dim])
                             for i in range(n_heads)], axis=-1)
        v = qkv[:, 2 * width:]
        for i, arr in enumerate((q, k, v)):
            t_scr[i] = arr.T.reshape(n_heads, head_dim, bsz)
        small = small_ref[...]
        lane = _iota2((1, SMALL_W), 1)
        g = jnp.where(jnp.logical_and(lane >= a_off, lane < a_off + n_heads),
                      -jnp.exp(alog_row_ref[...]) * _softplus(small + dtb_row_ref[...]), 0.0)
        gate_scr[0] = jnp.exp(g).T
        gate_scr[1] = _sigmoid(small).T

    st_scr[...] = s_ref[...].T.reshape(head_dim, head_dim, bsz)
    q_t = t_scr[0, h]
    k_t = t_scr[1, h]
    v_t = t_scr[2, h]
    eg = gate_scr[0, pl.ds(a_off + h, 1), :]
    beta = gate_scr[1, pl.ds(b_off + h, 1), :]

    def reduce_step(i, carry):
        acc_k, acc_q = carry
        s_k = st_scr[i]
        k_i = t_scr[1, h, pl.ds(i, 1), :]
        q_i = t_scr[0, h, pl.ds(i, 1), :]
        return acc_k + k_i * s_k, acc_q + q_i * s_k

    zero = jnp.zeros((head_dim, bsz), F32)
    acc_k, acc_q = lax.fori_loop(0, head_dim, reduce_step, (zero, zero))
    v_new = beta * (v_t - eg * acc_k)
    qk = jnp.sum(q_t * k_t, axis=0, keepdims=True)
    y_scr[h] = eg * acc_q + qk * v_new

    def update_step(i, carry):
        k_i = t_scr[1, h, pl.ds(i, 1), :]
        st_scr[i] = st_scr[i] * eg + k_i * v_new
        return carry

    lax.fori_loop(0, head_dim, update_step, 0)
    so_ref[...] = st_scr[...].reshape(head_dim * head_dim, bsz).T

    @pl.when(h == n_heads - 1)
    def _():
        y = y_scr[...].reshape(width, bsz).T
        z = z_ref[...]
        outs = []
        for i in range(n_heads):
            sl = slice(i * head_dim, (i + 1) * head_dim)
            outs.append(_gdn_post_head(y[:, sl], z[:, sl], nw_ref[...]))
        o_ref[...] = jnp.concatenate(outs, axis=-1)


def _gdn_decode(qkv, small, z, conv_all, state_all, new_all, prm, *, layer, n_heads, head_dim,
                a_off, b_off):
    bsz = qkv.shape[0]
    width = n_heads * head_dim
    names = ("conv_w", "dtb_row", "alog_row", "norm_w")
    return _layer_state_call(
        functools.partial(_gdn_decode_body, n_heads=n_heads, head_dim=head_dim,
                          a_off=a_off, b_off=b_off), "gdn_decode",
        [qkv, small, z], [conv_all], state_all, new_all, [prm[n] for n in names],
        layer=layer, n_heads=n_heads, state_cols=head_dim * head_dim,
        n_row_out=[width, conv_all.shape[1]],
        scratch_shapes=[pltpu.VMEM((3, n_heads, head_dim, bsz), F32),
                        pltpu.VMEM((2, SMALL_W, bsz), F32),
                        pltpu.VMEM((n_heads, head_dim, bsz), F32),
                        pltpu.VMEM((head_dim, head_dim, bsz), F32)])


def _segment_ones(width, seg):
    i = jnp.arange(2 * width)[:, None] % width
    j = jnp.arange(width)[None, :]
    return ((i // seg) == (j // seg)).astype(BF16)


def _pad_lanes(row, offset):
    out = jnp.zeros((SMALL_W,), F32)
    return lax.dynamic_update_slice(out, row.astype(F32), (offset,)).reshape(1, SMALL_W)


def _row_tile(seq_len, cap):
    best = 8
    for tm in range(8, cap + 1, 8):
        if seq_len % tm == 0:
            best = tm
    return best


def kernel(x_prompt, x_sample, state_rwkv_wkv, state_rwkv_shift, state_ssm, state_ssm_conv,
           state_gdn, state_gdn_conv, meta_tokens, norm_w, w_in, rw_mu, rw_w0, rw_w2, rw_a0,
           rw_a2, rw_k_k, rw_k_a, rw_r_k, rw_gn_w, rw_gn_b, ssm_conv_w, ssm_conv_b,
           ssm_dt_bias, ssm_a_log, ssm_d, ssm_norm_w, gdn_conv_w, gdn_dt_bias, gdn_a_log,
           gdn_norm_w, w_rw_out, w_ssm_out, w_gdn_out, w_out, final_norm_w):
    depth = w_in.shape[0]
    bp, seq, d_model = x_prompt.shape
    bd = x_sample.shape[0]
    n_meta = meta_tokens.shape[0]
    rw_heads, rw_hd = rw_r_k.shape[1], rw_r_k.shape[2]
    rw_w = rw_heads * rw_hd
    rw_shift_w = rw_mu.shape[1]
    ssm_heads, ssm_hd, ssm_state = state_ssm.shape[2], state_ssm.shape[3], state_ssm.shape[4]
    ssm_w = ssm_heads * ssm_hd
    ssm_conv_ch = ssm_conv_w.shape[2]
    ssm_groups = (ssm_conv_ch - ssm_w) // (2 * ssm_state)
    gdn_heads, gdn_hd = state_gdn.shape[2], state_gdn.shape[3]
    gdn_w = gdn_heads * gdn_hd
    gdn_conv_ch = gdn_conv_w.shape[2]
    a_off, b_off = ssm_heads, ssm_heads + gdn_heads
    assert b_off + gdn_heads <= SMALL_W

    off = 0
    cols = {}
    for name, wdt in (("rw", rw_shift_w), ("rw_z", rw_w), ("ssm_z", ssm_w), ("xbc", ssm_conv_ch),
                      ("dt", ssm_heads), ("qkv", gdn_conv_ch), ("gdn_z", gdn_w),
                      ("ga", gdn_heads), ("gb", gdn_heads), ("gate", 3 * d_model)):
        cols[name] = (off, off + wdt)
        off += wdt
    assert off == w_in.shape[2]

    seg_names = ("rw", "rw_z", "ssm_z", "xbc", "qkv", "gdn_z", "gate", "dt", "ga", "gb")
    n_small = ssm_heads + 2 * gdn_heads
    w_all = jnp.concatenate(
        [w_in[:, :, cols[n][0]:cols[n][1]] for n in seg_names]
        + [jnp.zeros((depth, d_model, SMALL_W - n_small), w_in.dtype)], axis=2).astype(BF16)
    widths = [cols[n][1] - cols[n][0] for n in seg_names[:7]] + [SMALL_W]
    assert all(w % 128 == 0 for w in widths)
    w_r, w_s, w_g, w_o = (w.astype(BF16) for w in (w_rw_out, w_ssm_out, w_gdn_out, w_out))

    def layer_params(l):
        rw = dict(mu=rw_mu[l][None], w0=rw_w0[l][None], w2=rw_w2[l], a0=rw_a0[l][None],
                  a2=rw_a2[l], k_k=rw_k_k[l][None], k_a=rw_k_a[l][None],
                  r_k=rw_r_k[l].reshape(1, rw_w), gn_w=rw_gn_w[l][None], gn_b=rw_gn_b[l][None])
        dtb = _pad_lanes(ssm_dt_bias[l], 0)
        alog = _pad_lanes(ssm_a_log[l], 0)
        sm = dict(conv_w=ssm_conv_w[l], conv_b=ssm_conv_b[l][None], dtb_row=dtb,
                  dtb_col=dtb.reshape(SMALL_W, 1), alog_row=alog,
                  alog_col=alog.reshape(SMALL_W, 1),
                  d=jnp.repeat(ssm_d[l], ssm_hd)[None], norm_w=ssm_norm_w[l][None])
        gdtb = _pad_lanes(gdn_dt_bias[l], a_off)
        galog = _pad_lanes(gdn_a_log[l], a_off)
        gd = dict(conv_w=gdn_conv_w[l], dtb_row=gdtb, dtb_col=gdtb.reshape(SMALL_W, 1),
                  alog_row=galog, alog_col=galog.reshape(SMALL_W, 1),
                  norm_w=gdn_norm_w[l][None])
        return rw, sm, gd

    fw = final_norm_w[None]

    real_len = n_meta + seq
    seq_p = -(-real_len // CHUNK) * CHUNK
    n_pad = seq_p - real_len
    n_chunks = seq_p // CHUNK
    n_sub = max(d for d in range(1, MAX_CHUNKS_PER_STEP + 1) if n_chunks % d == 0)
    n_steps = n_chunks // n_sub
    meta = jnp.broadcast_to(meta_tokens.astype(F32)[None], (bp, n_meta, d_model))
    xp = jnp.concatenate([jnp.zeros((bp, n_pad, d_model), F32), meta, x_prompt], axis=1)
    xp = xp.reshape(bp * seq_p, d_model)
    tm_in = _row_tile(seq_p, 384)
    tm_out = _row_tile(seq_p, 384)

    p_states = [[] for _ in range(6)]
    y_p = None
    for l in range(depth):
        rw_p, sm_p, gd_p = layer_params(l)
        p_rw, rw_z, ssm_z, xbc, qkv, gdn_z, gate, small = _inproj(
            xp, norm_w[l][None], w_all, widths, layer=l, tm=tm_in, n_pad=n_pad, seq_len=seq_p)
        rw_o, wkv, shift = _rwkv_prompt(p_rw, rw_z, rw_p, batch=bp, n_steps=n_steps,
                                        n_sub=n_sub, n_heads=rw_heads, head_dim=rw_hd)
        sm_o, ssm, sconv = _ssd_prompt(xbc, small, ssm_z, sm_p, batch=bp, n_steps=n_steps,
                                       n_sub=n_sub, n_heads=ssm_heads, head_dim=ssm_hd,
                                       n_groups=ssm_groups, d_state=ssm_state, n_pad=n_pad)
        gd_o, gdn, gconv = _gdn_prompt(qkv, small, gdn_z, gd_p, batch=bp, n_steps=n_steps,
                                       n_sub=n_sub, n_heads=gdn_heads, head_dim=gdn_hd,
                                       a_off=a_off, b_off=b_off)
        last = l == depth - 1
        res = _outproj(rw_o, sm_o, gd_o, gate, xp, w_r[l], w_s[l], w_g[l], w_o[l], fw,
                       tm=tm_out, with_final=last)
        xp = res[0]
        if last:
            y_p = res[1]
        for acc, s in zip(p_states, (wkv, shift.reshape(bp, rw_shift_w), ssm, sconv, gdn, gconv)):
            acc.append(s)
    y_prompt = y_p.reshape(bp, seq_p, d_model)[:, n_pad + n_meta:]

    xs = x_sample.reshape(bd, d_model)
    wkv_all = state_rwkv_wkv.reshape(depth * bd, -1)
    shift_all = state_rwkv_shift.reshape(depth * bd, -1)
    ssm_all = state_ssm.reshape(depth * bd, -1)
    sconv_all = state_ssm_conv.reshape(depth * bd, -1)
    gdn_all = state_gdn.reshape(depth * bd, -1)
    gconv_all = state_gdn_conv.reshape(depth * bd, -1)
    wkv_new, ssm_new, gdn_new = (jnp.zeros(a.shape, F32) for a in (wkv_all, ssm_all, gdn_all))
    small_states = [[] for _ in range(3)]
    y_s = None
    for l in range(depth):
        rw_p, sm_p, gd_p = layer_params(l)
        p_rw, rw_z, ssm_z, xbc, qkv, gdn_z, gate, small = _inproj(
            xs, norm_w[l][None], w_all, widths, layer=l, tm=bd, n_pad=0, seq_len=bd)
        rw_o, wkv_new = _rwkv_decode(p_rw, rw_z, shift_all, wkv_all, wkv_new, rw_p, layer=l,
                                     n_heads=rw_heads, head_dim=rw_hd)
        sm_o, sconv, ssm_new = _ssd_decode(xbc, small, ssm_z, sconv_all, ssm_all, ssm_new, sm_p,
                                           layer=l, n_heads=ssm_heads, head_dim=ssm_hd,
                                           n_groups=ssm_groups, d_state=ssm_state)
        gd_o, gconv, gdn_new = _gdn_decode(qkv, small, gdn_z, gconv_all, gdn_all, gdn_new, gd_p,
                                           layer=l, n_heads=gdn_heads, head_dim=gdn_hd,
                                           a_off=a_off, b_off=b_off)
        last = l == depth - 1
        res = _outproj(rw_o, sm_o, gd_o, gate, xs, w_r[l], w_s[l], w_g[l], w_o[l], fw,
                       tm=bd, with_final=last)
        xs = res[0]
        if last:
            y_s = res[1]
        for acc, s in zip(small_states, (p_rw, sconv, gconv)):
            acc.append(s)
    y_sample = y_s.reshape(bd, 1, d_model)
    s_shift, s_sconv, s_gconv = (jnp.stack(a) for a in small_states)
    s_states = (wkv_new.reshape(state_rwkv_wkv.shape), s_shift,
                ssm_new.reshape(state_ssm.shape), s_sconv.reshape(state_ssm_conv.shape),
                gdn_new.reshape(state_gdn.shape), s_gconv.reshape(state_gdn_conv.shape))

    return (y_prompt, y_sample) + tuple(jnp.stack(a) for a in p_states) + s_states
```

```python
import functools
import math

import jax
import jax.numpy as jnp
from jax import lax
from jax.experimental import pallas as pl
from jax.experimental.pallas import tpu as pltpu

F32 = jnp.float32
BF16 = jnp.bfloat16
HIGHEST = lax.Precision.HIGHEST

NORM_EPS = 1e-6
RW_GN_EPS = 64e-5
SSM_NORM_EPS = 1e-5
GDN_NORM_EPS = 1e-6
L2_EPS = 1e-6
CONV_W = 4
CHUNK = 64
MAX_CHUNKS_PER_STEP = 3
INV_BASE = 8
SEG_TILE = 256
DECODE_UNROLL = 8
SMALL_W = 128
CONV_HALO = 8
VMEM_LIMIT = 56 * 1024 * 1024


def _mm(a, b, exact=False):
    if exact:
        return jnp.dot(a, b, preferred_element_type=F32, precision=HIGHEST)
    return jnp.dot(a.astype(BF16), b.astype(BF16), preferred_element_type=F32)


def _mm_nt(a, b, exact=False):
    dn = (((1,), (1,)), ((), ()))
    if exact:
        return lax.dot_general(a, b, dn, preferred_element_type=F32, precision=HIGHEST)
    return lax.dot_general(a.astype(BF16), b.astype(BF16), dn, preferred_element_type=F32)


def _mm_tn(a, b, exact=False):
    dn = (((0,), (0,)), ((), ()))
    if exact:
        return lax.dot_general(a, b, dn, preferred_element_type=F32, precision=HIGHEST)
    return lax.dot_general(a.astype(BF16), b.astype(BF16), dn, preferred_element_type=F32)


def _sigmoid(x):
    return 1.0 / (1.0 + jnp.exp(-x))


def _silu(x):
    return x * _sigmoid(x)


def _softplus(x):
    return jnp.maximum(x, 0.0) + jnp.log(1.0 + jnp.exp(-jnp.abs(x)))


def _iota2(shape, dim):
    return lax.broadcasted_iota(jnp.int32, shape, dim)


def _seg_sum(x, seg_ones, split=False):
    w = seg_ones.shape[1]
    parts = []
    for i in range(x.shape[1] // w):
        xi = x[:, i * w:(i + 1) * w]
        x_hi = xi.astype(BF16)
        if split:
            x_lo = (xi - x_hi.astype(F32)).astype(BF16)
            parts.append(jnp.dot(jnp.concatenate([x_hi, x_lo], axis=1), seg_ones,
                                 preferred_element_type=F32))
        else:
            parts.append(jnp.dot(x_hi, seg_ones[:w], preferred_element_type=F32))
    return jnp.concatenate(parts, axis=1)


def _block_diag_pair(p):
    c = p.shape[0]
    lane = _iota2(p.shape, 1)
    pb = p.astype(BF16)
    zero = jnp.zeros_like(pb)
    return jnp.concatenate([jnp.where(lane < c, pb, zero), jnp.where(lane >= c, pb, zero)], axis=0)


def _mm_pair(x, p):
    return jnp.dot(x.astype(BF16), _block_diag_pair(p), preferred_element_type=F32)


def _unit_lower_inverse(l_mats):
    c = l_mats[0].shape[0]
    pairs = [jnp.concatenate(l_mats[i:i + 2], axis=1) for i in range(0, len(l_mats), 2)]
    ri = _iota2((c, 2 * c), 0)
    ci = _iota2((c, 2 * c), 1) & (c - 1)
    base_shift = int(math.log2(INV_BASE))
    same_base = (ri >> base_shift) == (ci >> base_shift)
    es = [jnp.where(same_base, l, 0.0) for l in pairs]
    ps = es
    n_terms = 1
    while 2 * n_terms < INV_BASE:
        ps = [_mm_pair(p, p) for p in ps]
        eps = [_mm_pair(e, p) for e, p in zip(es, ps)]
        es = [e + p + ep for e, p, ep in zip(es, ps, eps)]
        n_terms *= 2
    shift = base_shift
    while (1 << shift) < c:
        same_outer = (ri >> (shift + 1)) == (ci >> (shift + 1))
        other_inner = (ri >> shift) != (ci >> shift)
        l_off = [jnp.where(same_outer, jnp.where(other_inner, l, 0.0), 0.0) for l in pairs]
        ts = [lo + _mm_pair(lo, e) for lo, e in zip(l_off, es)]
        ets = [_mm_pair(e, t) for e, t in zip(es, ts)]
        es = [e + t + et for e, t, et in zip(es, ts, ets)]
        shift += 1
    out = []
    for e in es:
        out += [e[:, :c], e[:, c:]]
    return out


def _chunk_masks(c):
    ri = _iota2((c, c), 0)
    ci = _iota2((c, c), 1)
    return ri >= ci, ri > ci


def _block_tril(rows, c):
    shift = int(math.log2(c))
    ri = _iota2((rows, rows), 0)
    ci = _iota2((rows, rows), 1)
    return jnp.where((ri >> shift) == (ci >> shift),
                     jnp.where(ri >= ci, 1.0, 0.0), 0.0).astype(F32)


def _conv_chunk(ext_ref, u, w_ref, first_chunk):
    c = u.shape[0]

    @pl.when(first_chunk)
    def _():
        ext_ref[0:CONV_HALO, :] = jnp.zeros((CONV_HALO, u.shape[1]), F32)

    ext_ref[CONV_HALO:CONV_HALO + c, :] = u
    out = u * w_ref[CONV_W - 1:CONV_W, :]
    for i in range(CONV_W - 1):
        off = CONV_HALO - (CONV_W - 1) + i
        out = out + ext_ref[off:off + c, :] * w_ref[i:i + 1, :]
    tail = ext_ref[c:c + CONV_HALO, :]
    ext_ref[0:CONV_HALO, :] = tail
    return out, tail[CONV_HALO - (CONV_W - 1):, :]


def _inproj_body(x_ref, nw_ref, w_ref, *o_refs, n_pad, tiles_per_seq):
    x = x_ref[...]
    h = x * lax.rsqrt(jnp.mean(x * x, axis=-1, keepdims=True) + NORM_EPS) * nw_ref[...]
    if n_pad:
        first = lax.rem(pl.program_id(0), tiles_per_seq) == 0
        row = _iota2((x.shape[0], 1), 0)
        h = jnp.where(jnp.logical_and(first, row < n_pad), 0.0, h)
    hb = h.astype(BF16)
    off = 0
    for o_ref in o_refs:
        width = o_ref.shape[1]
        o_ref[...] = jnp.dot(hb, w_ref[:, off:off + width], preferred_element_type=F32)
        off += width


def _inproj(x, norm_w, w_all, widths, *, layer, tm, n_pad, seq_len):
    t, d = x.shape
    assert w_all.shape[2] == sum(widths)
    in_specs = [pl.BlockSpec((tm, d), lambda i: (i, 0)),
                pl.BlockSpec((1, d), lambda i: (0, 0)),
                pl.BlockSpec((None,) + w_all.shape[1:], lambda i: (layer, 0, 0),
                             pipeline_mode=pl.Buffered(1))]
    out_specs = [pl.BlockSpec((tm, w), lambda i: (i, 0)) for w in widths]
    out_shape = [jax.ShapeDtypeStruct((t, w), F32) for w in widths]
    body = functools.partial(_inproj_body, n_pad=n_pad,
                             tiles_per_seq=(seq_len // tm) if n_pad else 1)
    return pl.pallas_call(
        body, grid=(t // tm,), in_specs=in_specs, out_specs=out_specs, out_shape=out_shape,
        compiler_params=pltpu.CompilerParams(dimension_semantics=("arbitrary",),
                                             vmem_limit_bytes=VMEM_LIMIT),
        name="inproj",
    )(x, norm_w, w_all)


def _outproj_body(rw_ref, sm_ref, gd_ref, gate_ref, x_ref, wr_ref, ws_ref, wg_ref, wo_ref,
                  fw_ref, o_ref, *maybe_y_ref, d_model):
    merged = None
    for b, (y_ref, w_ref) in enumerate(((rw_ref, wr_ref), (sm_ref, ws_ref), (gd_ref, wg_ref))):
        g = _sigmoid(gate_ref[:, b * d_model:(b + 1) * d_model])
        term = g * jnp.dot(y_ref[...].astype(BF16), w_ref[...], preferred_element_type=F32)
        merged = term if merged is None else merged + term
    x_new = x_ref[...] + jnp.dot(merged.astype(BF16), wo_ref[...], preferred_element_type=F32)
    o_ref[...] = x_new
    if maybe_y_ref:
        ms = jnp.mean(x_new * x_new, axis=-1, keepdims=True)
        maybe_y_ref[0][...] = x_new * lax.rsqrt(ms + NORM_EPS) * fw_ref[...]


def _outproj(rw, sm, gd, gate, x, w_r, w_s, w_g, w_o, final_w, *, tm, with_final):
    t, d = x.shape
    grid = (t // tm,)
    row = lambda i: (i, 0)
    fixed = lambda i: (0, 0)
    in_specs = [pl.BlockSpec((tm, rw.shape[1]), row), pl.BlockSpec((tm, sm.shape[1]), row),
                pl.BlockSpec((tm, gd.shape[1]), row), pl.BlockSpec((tm, gate.shape[1]), row),
                pl.BlockSpec((tm, d), row),
                pl.BlockSpec(w_r.shape, fixed), pl.BlockSpec(w_s.shape, fixed),
                pl.BlockSpec(w_g.shape, fixed), pl.BlockSpec(w_o.shape, fixed),
                pl.BlockSpec((1, d), fixed)]
    n_out = 2 if with_final else 1
    out_specs = [pl.BlockSpec((tm, d), row)] * n_out
    out_shape = [jax.ShapeDtypeStruct((t, d), F32)] * n_out
    return pl.pallas_call(
        functools.partial(_outproj_body, d_model=d),
        grid=grid, in_specs=in_specs, out_specs=out_specs, out_shape=out_shape,
        compiler_params=pltpu.CompilerParams(dimension_semantics=("arbitrary",),
                                             vmem_limit_bytes=VMEM_LIMIT),
        name="outproj",
    )(rw, sm, gd, gate, x, w_r, w_s, w_g, w_o, final_w)


def _rwkv_pre(p, shifted, mu, w0, w2, a0, a2, k_k, k_a, width, lora):
    u = p + (shifted - p) * mu
    r = u[:, :width]
    k = u[:, width:2 * width]
    v = u[:, 2 * width:3 * width]
    wd = u[:, 3 * width:3 * width + lora]
    ad = u[:, 3 * width + lora:]
    w = -_softplus(-(w0 + _mm(jnp.tanh(wd), w2))) - 0.5
    log_decay = -jnp.exp(w)
    a = _sigmoid(a0 + _mm(ad, a2))
    kk_raw = k * k_k
    k_mod = k * (1.0 + (a - 1.0) * k_a)
    return r, k_mod, v, log_decay, a, kk_raw


def _l2norm(x):
    return x * lax.rsqrt(jnp.sum(x * x, axis=-1, keepdims=True) + L2_EPS)


def _rwkv_post_head(y, r, k_mod, v, z, r_k, gn_w, gn_b):
    mean = jnp.mean(y, axis=-1, keepdims=True)
    var = jnp.mean(jnp.square(y - mean), axis=-1, keepdims=True)
    yn = (y - mean) * lax.rsqrt(var + RW_GN_EPS) * gn_w + gn_b
    bonus = jnp.sum(r * k_mod * r_k, axis=-1, keepdims=True) * v
    return (yn + bonus) * _silu(z)


def _rwkv_prompt_body(p_ref, z_ref, ones_ref, mu_ref, w0_ref, w2_ref, a0_ref, a2_ref, kk_ref,
                      ka_ref, rk_ref, gnw_ref, gnb_ref, o_ref, wkv_ref, shift_ref, prev_scr,
                      *, n_heads, head_dim, n_sub):
    width = n_heads * head_dim
    lora = w2_ref.shape[0]
    p = p_ref[...]
    rows = p.shape[0]
    c = rows // n_sub

    @pl.when(pl.program_id(1) == 0)
    def _():
        wkv_ref[...] = jnp.zeros(wkv_ref.shape, F32)
        prev_scr[...] = jnp.zeros(prev_scr.shape, F32)

    row = _iota2((rows, 1), 0)
    shifted = jnp.where(row == 0, prev_scr[...], pltpu.roll(p, 1, 0))
    last_row = p[rows - 1:rows, :]
    prev_scr[...] = last_row
    shift_ref[0] = last_row

    r, k_mod, v, log_decay, a, kk_raw = _rwkv_pre(
        p, shifted, mu_ref[...], w0_ref[...], w2_ref[...], a0_ref[...], a2_ref[...],
        kk_ref[...], ka_ref[...], width, lora)
    cum = _mm(_block_tril(rows, c), log_decay, exact=True)
    e_pos = jnp.exp(cum)
    e_neg = jnp.exp(-cum)
    e_prev = jnp.exp(cum - log_decay)
    z = z_ref[...]
    incl, strict = _chunk_masks(c)

    hs = range(n_heads)
    sls = [slice(h * head_dim, (h + 1) * head_dim) for h in hs]
    units = [(j, h) for j in range(n_sub) for h in hs]
    rsl = [slice(j * c, (j + 1) * c) for j in range(n_sub)]
    seg_ones = ones_ref[...]
    kk = kk_raw * lax.rsqrt(_seg_sum(kk_raw * kk_raw, seg_ones) + L2_EPS)
    a_all = -kk * e_prev
    b_all = kk * a * e_neg
    k_all = k_mod * e_neg
    r_all = r * e_pos
    a_t = [a_all[:, sl] for sl in sls]
    b_t = [b_all[:, sl] for sl in sls]
    k_t = [k_all[:, sl] for sl in sls]
    r_t = [r_all[:, sl] for sl in sls]
    v_h = [v[:, sl] for sl in sls]

    sc = {u: _mm_nt(jnp.concatenate([a_t[u[1]][rsl[u[0]]], r_t[u[1]][rsl[u[0]]]], axis=0),
                    jnp.concatenate([b_t[u[1]][rsl[u[0]]], k_t[u[1]][rsl[u[0]]]], axis=0))
          for u in units}
    l_ab = {u: jnp.where(strict, sc[u][:c, :c], 0.0) for u in units}
    l_ak = {u: jnp.where(strict, sc[u][:c, c:], 0.0) for u in units}
    m_rb = {u: jnp.where(incl, sc[u][c:, :c], 0.0) for u in units}
    m_rk = {u: jnp.where(incl, sc[u][c:, c:], 0.0) for u in units}
    lv = {u: _mm(jnp.concatenate([l_ak[u], m_rk[u]], axis=0), v_h[u[1]][rsl[u[0]]]) for u in units}
    inv = dict(zip(units, _unit_lower_inverse([l_ab[u] for u in units])))
    rhs = {u: jnp.concatenate([a_t[u[1]][rsl[u[0]]], lv[u][:c]], axis=1) for u in units}
    wu = {u: rhs[u] + _mm(inv[u], rhs[u]) for u in units}
    d_end = {(j, h): e_pos[(j + 1) * c - 1:(j + 1) * c, sls[h]] for j, h in units}
    qy = {u: _mm(m_rb[u], wu[u]) for u in units}
    mn = {u: _mm_tn(wu[u], b_t[u[1]][rsl[u[0]]] * d_end[u]) for u in units}
    n2 = {u: _mm_tn(v_h[u[1]][rsl[u[0]]], k_t[u[1]][rsl[u[0]]] * d_end[u]) for u in units}

    states = [wkv_ref[0, h] for h in hs]
    ys = [[] for _ in hs]
    for j in range(n_sub):
        y_dot = [_mm_nt(r_t[h][rsl[j]] + qy[(j, h)][:, :c], states[h]) for h in hs]
        s_dot = [_mm(states[h], mn[(j, h)][:c]) for h in hs]
        for h in hs:
            ys[h].append(y_dot[h] + qy[(j, h)][:, c:] + lv[(j, h)][c:])
            states[h] = states[h] * d_end[(j, h)] + s_dot[h] + (mn[(j, h)][c:] + n2[(j, h)])
    for h in hs:
        wkv_ref[0, h] = states[h]
    y = jnp.concatenate([jnp.concatenate(ys[h], axis=0) for h in hs], axis=-1)
    inv_n = 1.0 / head_dim
    dev = y - _seg_sum(y, seg_ones) * inv_n
    var = _seg_sum(dev * dev, seg_ones) * inv_n
    yn = dev * lax.rsqrt(var + RW_GN_EPS) * gnw_ref[...] + gnb_ref[...]
    bonus = _seg_sum(r * k_mod * rk_ref[...], seg_ones, split=True) * v
    o_ref[...] = (yn + bonus) * _silu(z)


def _rwkv_prompt(p_rw, z, prm, *, batch, n_steps, n_sub, n_heads, head_dim):
    t, shift_w = p_rw.shape
    width = n_heads * head_dim
    rows = n_sub * CHUNK
    row = lambda b, c: (b * n_steps + c, 0)
    fixed = lambda b, c: (0, 0)
    names = ("mu", "w0", "w2", "a0", "a2", "k_k", "k_a", "r_k", "gn_w", "gn_b")
    params = [_segment_ones(SEG_TILE, head_dim)] + [prm[n] for n in names]
    in_specs = [pl.BlockSpec((rows, shift_w), row), pl.BlockSpec((rows, width), row)]
    in_specs += [pl.BlockSpec(a.shape, fixed) for a in params]
    out_specs = [pl.BlockSpec((rows, width), row),
                 pl.BlockSpec((1, n_heads, head_dim, head_dim), lambda b, c: (b, 0, 0, 0)),
                 pl.BlockSpec((1, 1, shift_w), lambda b, c: (b, 0, 0))]
    out_shape = [jax.ShapeDtypeStruct((t, width), F32),
                 jax.ShapeDtypeStruct((batch, n_heads, head_dim, head_dim), F32),
                 jax.ShapeDtypeStruct((batch, 1, shift_w), F32)]
    return pl.pallas_call(
        functools.partial(_rwkv_prompt_body, n_heads=n_heads, head_dim=head_dim, n_sub=n_sub),
        grid=(batch, n_steps), in_specs=in_specs, out_specs=out_specs, out_shape=out_shape,
        scratch_shapes=[pltpu.VMEM((1, shift_w), F32)],
        compiler_params=pltpu.CompilerParams(dimension_semantics=("arbitrary", "arbitrary"),
                                             vmem_limit_bytes=VMEM_LIMIT),
        name="rwkv_prompt",
    )(p_rw, z, *params)


def _ssm_gated_norm(y, z, norm_w, n_groups):
    g = y * _silu(z)
    gw = g.shape[1] // n_groups
    parts = []
    for i in range(n_groups):
        gi = g[:, i * gw:(i + 1) * gw]
        parts.append(gi * lax.rsqrt(jnp.mean(gi * gi, axis=-1, keepdims=True) + SSM_NORM_EPS))
    return jnp.concatenate(parts, axis=-1) * norm_w


def _ssd_prompt_body(xbc_ref, small_ref, z_ref, cw_ref, cb_ref, dtb_row_ref, dtb_col_ref,
                     alog_row_ref, alog_col_ref, d_ref, nw_ref, o_ref, h_ref, conv_ref, ext_scr,
                     *, n_heads, head_dim, n_groups, d_state, n_pad, n_sub):
    first = pl.program_id(1) == 0
    width = n_heads * head_dim
    hpg = n_heads // n_groups
    gw = hpg * head_dim
    x_pre = xbc_ref[...]
    rows = x_pre.shape[0]
    c = rows // n_sub
    rsl = [slice(j * c, (j + 1) * c) for j in range(n_sub)]

    @pl.when(first)
    def _():
        h_ref[...] = jnp.zeros(h_ref.shape, F32)

    conv, tail = _conv_chunk(ext_scr, x_pre, cw_ref, first)
    conv_ref[0] = tail
    xbc = _silu(conv + cb_ref[...])
    xs = xbc[:, :width]
    b_all = xbc[:, width:width + n_groups * d_state]
    c_all = xbc[:, width + n_groups * d_state:]

    small = small_ref[...]
    lane = _iota2((1, SMALL_W), 1)
    sub = _iota2((SMALL_W, 1), 0)
    dt = _softplus(small + dtb_row_ref[...])
    dt_t = [_softplus(small[sl].T + dtb_col_ref[...]) for sl in rsl]
    if n_pad:
        assert n_pad <= c
        not_first = jnp.logical_not(first)
        dt = jnp.where(jnp.logical_or(not_first, _iota2((rows, 1), 0) >= n_pad), dt, 0.0)
        dt_t[0] = jnp.where(jnp.logical_or(not_first, _iota2((1, c), 1) >= n_pad), dt_t[0], 0.0)
    a_row = jnp.where(lane < n_heads, -jnp.exp(alog_row_ref[...]), 0.0)
    a_col = jnp.where(sub < n_heads, -jnp.exp(alog_col_ref[...]), 0.0)
    incl, _ = _chunk_masks(c)
    triu = jnp.where(_iota2((c, c), 0) <= _iota2((c, c), 1), 1.0, 0.0).astype(F32)
    acs = _mm(_block_tril(rows, c), dt * a_row, exact=True)
    acs_t = [_mm(t * a_col, triu, exact=True) for t in dt_t]
    expand = jnp.where((_iota2((SMALL_W, width), 1) // head_dim) == _iota2((SMALL_W, width), 0),
                       1.0, 0.0).astype(F32)
    wide = _mm(jnp.concatenate([acs, dt], axis=0), expand, exact=True)
    acs_w = wide[:rows]
    xdt = xs * wide[rows:]
    e_acs = jnp.exp(acs_w)

    hs = range(n_heads)
    gs = range(n_groups)
    js = range(n_sub)
    b_g = {(j, g): b_all[rsl[j], g * d_state:(g + 1) * d_state] for j in js for g in gs}
    c_g = {(j, g): c_all[rsl[j], g * d_state:(g + 1) * d_state] for j in js for g in gs}
    a_last = {j: acs_w[(j + 1) * c - 1:(j + 1) * c] for j in js}
    xdt_end = {j: xdt[rsl[j]] * jnp.exp(a_last[j] - acs_w[rsl[j]]) for j in js}

    cb = {jg: _mm_nt(c_g[jg], b_g[jg]) for jg in b_g}
    seg = {(j, h): jnp.exp(jnp.where(incl, acs[rsl[j], h:h + 1] - acs_t[j][h:h + 1, :], -jnp.inf))
           for j in js for h in hs}
    y_diag = {(j, h): _mm(cb[(j, h // hpg)] * seg[(j, h)],
                          xdt[rsl[j], h * head_dim:(h + 1) * head_dim])
              for j in js for h in hs}
    st = {(j, g): _mm_tn(xdt_end[j][:, g * gw:(g + 1) * gw], b_g[(j, g)]) for j in js for g in gs}

    states = [h_ref[0, g * hpg:(g + 1) * hpg].reshape(gw, d_state) for g in gs]
    ys = []
    for j in js:
        y_off = [_mm_nt(c_g[(j, g)], states[g]) for g in gs]
        ys.append(jnp.concatenate(y_off, axis=-1) * e_acs[rsl[j]]
                  + jnp.concatenate([y_diag[(j, h)] for h in hs], axis=-1))
        for g in gs:
            scaled = [states[g][i * head_dim:(i + 1) * head_dim]
                      * jnp.exp(acs[(j + 1) * c - 1:(j + 1) * c, g * hpg + i:g * hpg + i + 1])
                      for i in range(hpg)]
            states[g] = jnp.concatenate(scaled, axis=0) + st[(j, g)]
    for g in gs:
        h_ref[0, g * hpg:(g + 1) * hpg] = states[g].reshape(hpg, head_dim, d_state)
    y_all = jnp.concatenate(ys, axis=0) + xs * d_ref[...]
    o_ref[...] = _ssm_gated_norm(y_all, z_ref[...], nw_ref[...], n_groups)


def _ssd_prompt(xbc, small, z, prm, *, batch, n_steps, n_sub, n_heads, head_dim, n_groups,
                d_state, n_pad):
    t, conv_ch = xbc.shape
    width = n_heads * head_dim
    rows = n_sub * CHUNK
    row = lambda b, c: (b * n_steps + c, 0)
    fixed = lambda b, c: (0, 0)
    names = ("conv_w", "conv_b", "dtb_row", "dtb_col", "alog_row", "alog_col", "d", "norm_w")
    params = [prm[n] for n in names]
    in_specs = [pl.BlockSpec((rows, conv_ch), row), pl.BlockSpec((rows, SMALL_W), row),
                pl.BlockSpec((rows, width), row)]
    in_specs += [pl.BlockSpec(a.shape, fixed) for a in params]
    out_specs = [pl.BlockSpec((rows, width), row),
                 pl.BlockSpec((1, n_heads, head_dim, d_state), lambda b, c: (b, 0, 0, 0)),
                 pl.BlockSpec((1, CONV_W - 1, conv_ch), lambda b, c: (b, 0, 0))]
    out_shape = [jax.ShapeDtypeStruct((t, width), F32),
                 jax.ShapeDtypeStruct((batch, n_heads, head_dim, d_state), F32),
                 jax.ShapeDtypeStruct((batch, CONV_W - 1, conv_ch), F32)]
    return pl.pallas_call(
        functools.partial(_ssd_prompt_body, n_heads=n_heads, head_dim=head_dim,
                          n_groups=n_groups, d_state=d_state, n_pad=n_pad, n_sub=n_sub),
        grid=(batch, n_steps), in_specs=in_specs, out_specs=out_specs, out_shape=out_shape,
        scratch_shapes=[pltpu.VMEM((rows + CONV_HALO, conv_ch), F32)],
        compiler_params=pltpu.CompilerParams(dimension_semantics=("arbitrary", "arbitrary"),
                                             vmem_limit_bytes=VMEM_LIMIT),
        name="ssd_prompt",
    )(xbc, small, z, *params)


def _gdn_post_head(o, z, norm_w):
    return o * lax.rsqrt(jnp.mean(o * o, axis=-1, keepdims=True) + GDN_NORM_EPS) * norm_w * _silu(z)


def _gdn_prompt_body(qkv_ref, small_ref, z_ref, cw_ref, dtb_row_ref, dtb_col_ref, alog_row_ref,
                     alog_col_ref, nw_ref, o_ref, s_ref, conv_ref, ext_scr,
                     *, n_heads, head_dim, a_off, b_off, n_sub):
    first = pl.program_id(1) == 0
    width = n_heads * head_dim
    x_pre = qkv_ref[...]
    rows = x_pre.shape[0]
    c = rows // n_sub

    @pl.when(first)
    def _():
        s_ref[...] = jnp.zeros(s_ref.shape, F32)

    conv, tail = _conv_chunk(ext_scr, x_pre, cw_ref, first)
    conv_ref[0] = tail
    qkv = _silu(conv)
    small = small_ref[...]
    rsl = [slice(j * c, (j + 1) * c) for j in range(n_sub)]
    small_t = [small[sl].T for sl in rsl]
    lane = _iota2((1, SMALL_W), 1)
    sub = _iota2((SMALL_W, 1), 0)
    g = jnp.where(jnp.logical_and(lane >= a_off, lane < a_off + n_heads),
                  -jnp.exp(alog_row_ref[...]) * _softplus(small + dtb_row_ref[...]), 0.0)
    g_t = [jnp.where(jnp.logical_and(sub >= a_off, sub < a_off + n_heads),
                     -jnp.exp(alog_col_ref[...]) * _softplus(st + dtb_col_ref[...]), 0.0)
           for st in small_t]
    beta = _sigmoid(small)
    incl, strict = _chunk_masks(c)
    triu = jnp.where(_iota2((c, c), 0) <= _iota2((c, c), 1), 1.0, 0.0).astype(F32)
    gcs = _mm(_block_tril(rows, c), g, exact=True)
    gcs_t = [_mm(gt, triu, exact=True) for gt in g_t]
    z = z_ref[...]
    scale = head_dim ** -0.5

    hs = range(n_heads)
    sls = [slice(h * head_dim, (h + 1) * head_dim) for h in hs]
    units = [(j, h) for j in range(n_sub) for h in hs]
    q_h = [_l2norm(qkv[:, sl]) * scale for sl in sls]
    k_h = [_l2norm(qkv[:, width + h * head_dim:width + (h + 1) * head_dim]) for h in hs]
    v_h = [qkv[:, 2 * width + h * head_dim:2 * width + (h + 1) * head_dim] for h in hs]
    g_c = {(j, h): gcs[rsl[j], a_off + h:a_off + h + 1] for j, h in units}
    g_r = {(j, h): gcs_t[j][a_off + h:a_off + h + 1, :] for j, h in units}
    g_last = {(j, h): gcs[(j + 1) * c - 1:(j + 1) * c, a_off + h:a_off + h + 1] for j, h in units}
    b_c = {(j, h): beta[rsl[j], b_off + h:b_off + h + 1] for j, h in units}
    q = {(j, h): q_h[h][rsl[j]] for j, h in units}
    k = {(j, h): k_h[h][rsl[j]] for j, h in units}
    v = {(j, h): v_h[h][rsl[j]] for j, h in units}

    decay = {u: jnp.exp(jnp.where(incl, g_c[u] - g_r[u], -jnp.inf)) for u in units}
    kb = {u: k[u] * b_c[u] for u in units}
    e_c = {u: jnp.exp(g_c[u]) for u in units}
    sc = {u: _mm_nt(jnp.concatenate([kb[u], q[u]], axis=0), k[u]) for u in units}
    a_mat = {u: jnp.where(strict, sc[u][:c] * decay[u], 0.0) for u in units}
    qk = {u: jnp.where(incl, sc[u][c:] * decay[u], 0.0) for u in units}
    inv = dict(zip(units, _unit_lower_inverse([-a_mat[u] for u in units])))
    rhs = {u: jnp.concatenate([v[u] * b_c[u], kb[u] * e_c[u]], axis=1) for u in units}
    uw = {u: rhs[u] + _mm(inv[u], rhs[u]) for u in units}
    qo = {u: _mm(qk[u], uw[u]) for u in units}
    mn = {u: _mm_tn(k[u] * jnp.exp(g_last[u] - g_c[u]), uw[u]) for u in units}

    states = [s_ref[0, h] for h in hs]
    os_ = [[] for _ in hs]
    for j in range(n_sub):
        o_dot = [_mm(q[(j, h)] * e_c[(j, h)] - qo[(j, h)][:, head_dim:], states[h]) for h in hs]
        s_dot = [_mm(mn[(j, h)][:, head_dim:], states[h]) for h in hs]
        for h in hs:
            os_[h].append(o_dot[h] + qo[(j, h)][:, :head_dim])
            states[h] = (states[h] * jnp.exp(g_last[(j, h)]) - s_dot[h]
                         + mn[(j, h)][:, :head_dim])
    for h in hs:
        s_ref[0, h] = states[h]
    outs = [_gdn_post_head(jnp.concatenate(os_[h], axis=0), z[:, sls[h]], nw_ref[...])
            for h in hs]
    o_ref[...] = jnp.concatenate(outs, axis=-1)


def _gdn_prompt(qkv, small, z, prm, *, batch, n_steps, n_sub, n_heads, head_dim, a_off, b_off):
    t, conv_ch = qkv.shape
    width = n_heads * head_dim
    rows = n_sub * CHUNK
    row = lambda b, c: (b * n_steps + c, 0)
    fixed = lambda b, c: (0, 0)
    names = ("conv_w", "dtb_row", "dtb_col", "alog_row", "alog_col", "norm_w")
    params = [prm[n] for n in names]
    in_specs = [pl.BlockSpec((rows, conv_ch), row), pl.BlockSpec((rows, SMALL_W), row),
                pl.BlockSpec((rows, width), row)]
    in_specs += [pl.BlockSpec(a.shape, fixed) for a in params]
    out_specs = [pl.BlockSpec((rows, width), row),
                 pl.BlockSpec((1, n_heads, head_dim, head_dim), lambda b, c: (b, 0, 0, 0)),
                 pl.BlockSpec((1, CONV_W - 1, conv_ch), lambda b, c: (b, 0, 0))]
    out_shape = [jax.ShapeDtypeStruct((t, width), F32),
                 jax.ShapeDtypeStruct((batch, n_heads, head_dim, head_dim), F32),
                 jax.ShapeDtypeStruct((batch, CONV_W - 1, conv_ch), F32)]
    return pl.pallas_call(
        functools.partial(_gdn_prompt_body, n_heads=n_heads, head_dim=head_dim,
                          a_off=a_off, b_off=b_off, n_sub=n_sub),
        grid=(batch, n_steps), in_specs=in_specs, out_specs=out_specs, out_shape=out_shape,
        scratch_shapes=[pltpu.VMEM((rows + CONV_HALO, conv_ch), F32)],
        compiler_params=pltpu.CompilerParams(dimension_semantics=("arbitrary", "arbitrary"),
                                             vmem_limit_bytes=VMEM_LIMIT),
        name="gdn_prompt",
    )(qkv, small, z, *params)


def _conv_step(u, conv_ref, convo_ref, w_ref):
    ch = u.shape[1]
    out = u * w_ref[CONV_W - 1:CONV_W, :]
    for i in range(CONV_W - 1):
        out = out + conv_ref[:, i * ch:(i + 1) * ch] * w_ref[i:i + 1, :]
    for i in range(CONV_W - 2):
        convo_ref[:, i * ch:(i + 1) * ch] = conv_ref[:, (i + 1) * ch:(i + 2) * ch]
    convo_ref[:, (CONV_W - 2) * ch:] = u
    return out


def _lane_pick(x, lane, idx):
    return jnp.sum(jnp.where(lane == idx, x, 0.0), axis=1, keepdims=True)


def _rwkv_decode_body(p_ref, z_ref, prev_ref, s_ref, mu_ref, w0_ref, w2_ref, a0_ref, a2_ref,
                      kk_ref, ka_ref, rk_ref, gnw_ref, gnb_ref, o_ref, so_ref,
                      rows_scr, vt_scr, y_scr, row_scr, *, n_heads, head_dim):
    h = pl.program_id(0)
    width = n_heads * head_dim
    lora = w2_ref.shape[0]
    bsz = p_ref.shape[0]

    @pl.when(h == 0)
    def _():
        r, k_mod, v, log_decay, a, kk_raw = _rwkv_pre(
            p_ref[...], prev_ref[...], mu_ref[...], w0_ref[...], w2_ref[...], a0_ref[...],
            a2_ref[...], kk_ref[...], ka_ref[...], width, lora)
        kk = jnp.concatenate(
            [_l2norm(kk_raw[:, i * head_dim:(i + 1) * head_dim]) for i in range(n_heads)], axis=-1)
        row_scr[0] = r
        row_scr[1] = k_mod
        row_scr[2] = v
        for j, arr in enumerate((r, jnp.exp(log_decay), k_mod, -kk, kk * a)):
            for i in range(n_heads):
                rows_scr[j, i] = arr[:, i * head_dim:(i + 1) * head_dim]
        vt_scr[...] = v.T.reshape(n_heads, head_dim, bsz)

    v_t = vt_scr[h]
    lane = _iota2((head_dim, bsz), 1)

    def step(b, y_t):
        row = lambda j: rows_scr[j, h, pl.ds(b, 1), :]
        s_b = s_ref[b]
        sa = jnp.sum(s_b * row(3), axis=1, keepdims=True)
        s_new = s_b * row(1) + sa * row(4) + _lane_pick(v_t, lane, b) * row(2)
        so_ref[b] = s_new
        y_col = jnp.sum(s_new * row(0), axis=1, keepdims=True)
        return jnp.where(lane == b, y_col, y_t)

    y_scr[h] = lax.fori_loop(0, bsz, step, jnp.zeros((head_dim, bsz), F32), unroll=DECODE_UNROLL)

    @pl.when(h == n_heads - 1)
    def _():
        y = y_scr[...].reshape(width, bsz).T
        r, k_mod, v, z = row_scr[0], row_scr[1], row_scr[2], z_ref[...]
        outs = []
        for i in range(n_heads):
            sl = slice(i * head_dim, (i + 1) * head_dim)
            outs.append(_rwkv_post_head(y[:, sl], r[:, sl], k_mod[:, sl], v[:, sl], z[:, sl],
                                        rk_ref[:, sl], gnw_ref[:, sl], gnb_ref[:, sl]))
        o_ref[...] = jnp.concatenate(outs, axis=-1)


def _layer_state_call(body, name, row_inputs, layer_inputs, state_all, new_all, params, *,
                      layer, n_row_out, scratch_shapes):
    bsz = row_inputs[0].shape[0]
    n_heads = state_all.shape[2]
    fixed = lambda h: (0, 0)
    at_layer = lambda h: (layer, 0)
    state_spec = pl.BlockSpec((None, bsz, None) + state_all.shape[3:],
                              lambda h: (layer, 0, h, 0, 0))
    in_specs = [pl.BlockSpec(a.shape, fixed) for a in row_inputs]
    in_specs += [pl.BlockSpec((bsz, a.shape[1]), at_layer) for a in layer_inputs]
    in_specs += [state_spec]
    in_specs += [pl.BlockSpec(a.shape, fixed) for a in params]
    in_specs.append(pl.BlockSpec(memory_space=pl.ANY))
    operands = list(row_inputs) + list(layer_inputs) + [state_all] + list(params) + [new_all]
    n_in = len(operands)
    aliases = {n_in - 1: len(n_row_out)}
    inner = body

    def body(*refs):
        return inner(*refs[:n_in - 1], *refs[n_in:])

    out_specs = [pl.BlockSpec((bsz, w), fixed) for w in n_row_out] + [state_spec]
    out_shape = [jax.ShapeDtypeStruct((bsz, w), F32) for w in n_row_out]
    out_shape += [jax.ShapeDtypeStruct(state_all.shape, F32)]
    return pl.pallas_call(
        body, grid=(n_heads,), in_specs=in_specs, out_specs=out_specs, out_shape=out_shape,
        scratch_shapes=scratch_shapes, input_output_aliases=aliases,
        compiler_params=pltpu.CompilerParams(dimension_semantics=("arbitrary",),
                                             vmem_limit_bytes=VMEM_LIMIT),
        name=name,
    )(*operands)


def _rwkv_decode(p_rw, z, prev_all, state_all, new_all, prm, *, layer):
    bsz = p_rw.shape[0]
    n_heads, head_dim = state_all.shape[2], state_all.shape[3]
    width = n_heads * head_dim
    names = ("mu", "w0", "w2", "a0", "a2", "k_k", "k_a", "r_k", "gn_w", "gn_b")
    return _layer_state_call(
        functools.partial(_rwkv_decode_body, n_heads=n_heads, head_dim=head_dim), "rwkv_decode",
        [p_rw, z], [prev_all], state_all, new_all, [prm[n] for n in names],
        layer=layer, n_row_out=[width],
        scratch_shapes=[pltpu.VMEM((5, n_heads, bsz, head_dim), F32),
                        pltpu.VMEM((n_heads, head_dim, bsz), F32),
                        pltpu.VMEM((n_heads, head_dim, bsz), F32),
                        pltpu.VMEM((3, bsz, width), F32)])


def _ssd_decode_body(xbc_ref, small_ref, z_ref, conv_ref, s_ref, cw_ref, cb_ref, dtb_row_ref,
                     alog_row_ref, d_ref, nw_ref, o_ref, convo_ref, so_ref,
                     xt_scr, bc_scr, dec_scr, y_scr, skip_scr,
                     *, n_heads, head_dim, n_groups, d_state):
    h = pl.program_id(0)
    width = n_heads * head_dim
    bsz = xbc_ref.shape[0]
    hpg = n_heads // n_groups

    @pl.when(h == 0)
    def _():
        conv = _conv_step(xbc_ref[...], conv_ref, convo_ref, cw_ref)
        xbc = _silu(conv + cb_ref[...])
        lane_s = _iota2((1, SMALL_W), 1)
        dt = _softplus(small_ref[...] + dtb_row_ref[...])
        a_row = jnp.where(lane_s < n_heads, -jnp.exp(alog_row_ref[...]), 0.0)
        dec_scr[...] = jnp.exp(dt * a_row).T
        expand = jnp.where((_iota2((SMALL_W, width), 1) // head_dim) == _iota2((SMALL_W, width), 0),
                           1.0, 0.0).astype(F32)
        xs = xbc[:, :width]
        xt_scr[...] = (xs * _mm(dt, expand, exact=True)).T.reshape(n_heads, head_dim, bsz)
        skip_scr[...] = xs * d_ref[...]
        for i in range(2 * n_groups):
            bc_scr[i] = xbc[:, width + i * d_state:width + (i + 1) * d_state]

    g = h // hpg
    xdt_t = xt_scr[h]
    dec_row = dec_scr[pl.ds(h, 1), :]
    lane = _iota2((head_dim, bsz), 1)
    lane1 = _iota2((1, bsz), 1)

    def step(b, y_t):
        b_row = bc_scr[g, pl.ds(b, 1), :]
        c_row = bc_scr[n_groups + g, pl.ds(b, 1), :]
        s_new = (s_ref[b] * _lane_pick(dec_row, lane1, b)
                 + _lane_pick(xdt_t, lane, b) * b_row)
        so_ref[b] = s_new
        y_col = jnp.sum(s_new * c_row, axis=1, keepdims=True)
        return jnp.where(lane == b, y_col, y_t)

    y_scr[h] = lax.fori_loop(0, bsz, step, jnp.zeros((head_dim, bsz), F32), unroll=DECODE_UNROLL)

    @pl.when(h == n_heads - 1)
    def _():
        y = y_scr[...].reshape(width, bsz).T + skip_scr[...]
        o_ref[...] = _ssm_gated_norm(y, z_ref[...], nw_ref[...], n_groups)


def _ssd_decode(xbc, small, z, conv_all, state_all, new_all, prm, *, layer, n_groups):
    bsz = xbc.shape[0]
    n_heads, head_dim, d_state = state_all.shape[2:]
    width = n_heads * head_dim
    names = ("conv_w", "conv_b", "dtb_row", "alog_row", "d", "norm_w")
    return _layer_state_call(
        functools.partial(_ssd_decode_body, n_heads=n_heads, head_dim=head_dim,
                          n_groups=n_groups, d_state=d_state), "ssd_decode",
        [xbc, small, z], [conv_all], state_all, new_all, [prm[n] for n in names],
        layer=layer, n_row_out=[width, conv_all.shape[1]],
        scratch_shapes=[pltpu.VMEM((n_heads, head_dim, bsz), F32),
                        pltpu.VMEM((2 * n_groups, bsz, d_state), F32),
                        pltpu.VMEM((SMALL_W, bsz), F32),
                        pltpu.VMEM((n_heads, head_dim, bsz), F32),
                        pltpu.VMEM((bsz, width), F32)])


def _gdn_decode_body(qkv_ref, small_ref, z_ref, conv_ref, s_ref, cw_ref, dtb_row_ref,
                     alog_row_ref, nw_ref, o_ref, convo_ref, so_ref,
                     t_scr, v_scr, gate_scr, y_scr, *, n_heads, head_dim, a_off, b_off):
    h = pl.program_id(0)
    width = n_heads * head_dim
    bsz = qkv_ref.shape[0]

    @pl.when(h == 0)
    def _():
        qkv = _silu(_conv_step(qkv_ref[...], conv_ref, convo_ref, cw_ref))
        scale = head_dim ** -0.5
        q = jnp.concatenate([_l2norm(qkv[:, i * head_dim:(i + 1) * head_dim]) * scale
                             for i in range(n_heads)], axis=-1)
        k = jnp.concatenate([_l2norm(qkv[:, width + i * head_dim:width + (i + 1) * head_dim])
                             for i in range(n_heads)], axis=-1)
        for i, arr in enumerate((q, k)):
            t_scr[i] = arr.T.reshape(n_heads, head_dim, bsz)
        for i in range(n_heads):
            v_scr[i] = qkv[:, 2 * width + i * head_dim:2 * width + (i + 1) * head_dim]
        small = small_ref[...]
        lane_s = _iota2((1, SMALL_W), 1)
        g = jnp.where(jnp.logical_and(lane_s >= a_off, lane_s < a_off + n_heads),
                      -jnp.exp(alog_row_ref[...]) * _softplus(small + dtb_row_ref[...]), 0.0)
        gate_scr[0] = jnp.exp(g).T
        gate_scr[1] = _sigmoid(small).T
        gate_scr[2, 0:n_heads, :] = jnp.concatenate(
            [jnp.sum(q[:, i * head_dim:(i + 1) * head_dim] * k[:, i * head_dim:(i + 1) * head_dim],
                     axis=1, keepdims=True) for i in range(n_heads)], axis=1).T

    q_t = t_scr[0, h]
    k_t = t_scr[1, h]
    eg_row = gate_scr[0, pl.ds(a_off + h, 1), :]
    beta_row = gate_scr[1, pl.ds(b_off + h, 1), :]
    qk_row = gate_scr[2, pl.ds(h, 1), :]
    lane = _iota2((head_dim, bsz), 1)
    lane1 = _iota2((1, bsz), 1)

    def step(b, carry):
        s_b = s_ref[b]
        k_col = _lane_pick(k_t, lane, b)
        q_col = _lane_pick(q_t, lane, b)
        eg = _lane_pick(eg_row, lane1, b)
        k_s = jnp.sum(s_b * k_col, axis=0, keepdims=True)
        q_s = jnp.sum(s_b * q_col, axis=0, keepdims=True)
        v_new = _lane_pick(beta_row, lane1, b) * (v_scr[h, pl.ds(b, 1), :] - eg * k_s)
        y_scr[h, pl.ds(b, 1), :] = eg * q_s + _lane_pick(qk_row, lane1, b) * v_new
        so_ref[b] = s_b * eg + k_col * v_new
        return carry

    lax.fori_loop(0, bsz, step, 0, unroll=DECODE_UNROLL)

    @pl.when(h == n_heads - 1)
    def _():
        z = z_ref[...]
        outs = []
        for i in range(n_heads):
            sl = slice(i * head_dim, (i + 1) * head_dim)
            outs.append(_gdn_post_head(y_scr[i], z[:, sl], nw_ref[...]))
        o_ref[...] = jnp.concatenate(outs, axis=-1)


def _gdn_decode(qkv, small, z, conv_all, state_all, new_all, prm, *, layer, a_off, b_off):
    bsz = qkv.shape[0]
    n_heads, head_dim = state_all.shape[2], state_all.shape[3]
    width = n_heads * head_dim
    names = ("conv_w", "dtb_row", "alog_row", "norm_w")
    return _layer_state_call(
        functools.partial(_gdn_decode_body, n_heads=n_heads, head_dim=head_dim,
                          a_off=a_off, b_off=b_off), "gdn_decode",
        [qkv, small, z], [conv_all], state_all, new_all, [prm[n] for n in names],
        layer=layer, n_row_out=[width, conv_all.shape[1]],
        scratch_shapes=[pltpu.VMEM((2, n_heads, head_dim, bsz), F32),
                        pltpu.VMEM((n_heads, bsz, head_dim), F32),
                        pltpu.VMEM((3, SMALL_W, bsz), F32),
                        pltpu.VMEM((n_heads, bsz, head_dim), F32)])


def _segment_ones(width, seg):
    i = jnp.arange(2 * width)[:, None] % width
    j = jnp.arange(width)[None, :]
    return ((i // seg) == (j // seg)).astype(BF16)


def _pad_lanes(row, offset):
    out = jnp.zeros((SMALL_W,), F32)
    return lax.dynamic_update_slice(out, row.astype(F32), (offset,)).reshape(1, SMALL_W)


def _row_tile(seq_len, cap):
    best = 8
    for tm in range(8, cap + 1, 8):
        if seq_len % tm == 0:
            best = tm
    return best


def kernel(x_prompt, x_sample, state_rwkv_wkv, state_rwkv_shift, state_ssm, state_ssm_conv,
           state_gdn, state_gdn_conv, meta_tokens, norm_w, w_in, rw_mu, rw_w0, rw_w2, rw_a0,
           rw_a2, rw_k_k, rw_k_a, rw_r_k, rw_gn_w, rw_gn_b, ssm_conv_w, ssm_conv_b,
           ssm_dt_bias, ssm_a_log, ssm_d, ssm_norm_w, gdn_conv_w, gdn_dt_bias, gdn_a_log,
           gdn_norm_w, w_rw_out, w_ssm_out, w_gdn_out, w_out, final_norm_w):
    depth = w_in.shape[0]
    bp, seq, d_model = x_prompt.shape
    bd = x_sample.shape[0]
    n_meta = meta_tokens.shape[0]
    rw_heads, rw_hd = rw_r_k.shape[1], rw_r_k.shape[2]
    rw_w = rw_heads * rw_hd
    rw_shift_w = rw_mu.shape[1]
    ssm_heads, ssm_hd, ssm_state = state_ssm.shape[2], state_ssm.shape[3], state_ssm.shape[4]
    ssm_w = ssm_heads * ssm_hd
    ssm_conv_ch = ssm_conv_w.shape[2]
    ssm_groups = (ssm_conv_ch - ssm_w) // (2 * ssm_state)
    gdn_heads, gdn_hd = state_gdn.shape[2], state_gdn.shape[3]
    gdn_w = gdn_heads * gdn_hd
    gdn_conv_ch = gdn_conv_w.shape[2]
    a_off, b_off = ssm_heads, ssm_heads + gdn_heads
    assert b_off + gdn_heads <= SMALL_W

    off = 0
    cols = {}
    for name, wdt in (("rw", rw_shift_w), ("rw_z", rw_w), ("ssm_z", ssm_w), ("xbc", ssm_conv_ch),
                      ("dt", ssm_heads), ("qkv", gdn_conv_ch), ("gdn_z", gdn_w),
                      ("ga", gdn_heads), ("gb", gdn_heads), ("gate", 3 * d_model)):
        cols[name] = (off, off + wdt)
        off += wdt
    assert off == w_in.shape[2]

    seg_names = ("rw", "rw_z", "ssm_z", "xbc", "qkv", "gdn_z", "gate", "dt", "ga", "gb")
    n_small = ssm_heads + 2 * gdn_heads
    w_all = jnp.concatenate(
        [w_in[:, :, cols[n][0]:cols[n][1]] for n in seg_names]
        + [jnp.zeros((depth, d_model, SMALL_W - n_small), w_in.dtype)], axis=2).astype(BF16)
    widths = [cols[n][1] - cols[n][0] for n in seg_names[:7]] + [SMALL_W]
    assert all(w % 128 == 0 for w in widths)
    w_r, w_s, w_g, w_o = (w.astype(BF16) for w in (w_rw_out, w_ssm_out, w_gdn_out, w_out))

    def layer_params(l):
        rw = dict(mu=rw_mu[l][None], w0=rw_w0[l][None], w2=rw_w2[l], a0=rw_a0[l][None],
                  a2=rw_a2[l], k_k=rw_k_k[l][None], k_a=rw_k_a[l][None],
                  r_k=rw_r_k[l].reshape(1, rw_w), gn_w=rw_gn_w[l][None], gn_b=rw_gn_b[l][None])
        dtb = _pad_lanes(ssm_dt_bias[l], 0)
        alog = _pad_lanes(ssm_a_log[l], 0)
        sm = dict(conv_w=ssm_conv_w[l], conv_b=ssm_conv_b[l][None], dtb_row=dtb,
                  dtb_col=dtb.reshape(SMALL_W, 1), alog_row=alog,
                  alog_col=alog.reshape(SMALL_W, 1),
                  d=jnp.repeat(ssm_d[l], ssm_hd)[None], norm_w=ssm_norm_w[l][None])
        gdtb = _pad_lanes(gdn_dt_bias[l], a_off)
        galog = _pad_lanes(gdn_a_log[l], a_off)
        gd = dict(conv_w=gdn_conv_w[l], dtb_row=gdtb, dtb_col=gdtb.reshape(SMALL_W, 1),
                  alog_row=galog, alog_col=galog.reshape(SMALL_W, 1),
                  norm_w=gdn_norm_w[l][None])
        return rw, sm, gd

    fw = final_norm_w[None]

    real_len = n_meta + seq
    seq_p = -(-real_len // CHUNK) * CHUNK
    n_pad = seq_p - real_len
    n_chunks = seq_p // CHUNK
    n_sub = max(d for d in range(1, MAX_CHUNKS_PER_STEP + 1) if n_chunks % d == 0)
    n_steps = n_chunks // n_sub
    meta = jnp.broadcast_to(meta_tokens.astype(F32)[None], (bp, n_meta, d_model))
    xp = jnp.concatenate([jnp.zeros((bp, n_pad, d_model), F32), meta, x_prompt], axis=1)
    xp = xp.reshape(bp * seq_p, d_model)
    tm_in = _row_tile(seq_p, 384)
    tm_out = _row_tile(seq_p, 384)

    p_states = [[] for _ in range(6)]
    y_p = None
    for l in range(depth):
        rw_p, sm_p, gd_p = layer_params(l)
        p_rw, rw_z, ssm_z, xbc, qkv, gdn_z, gate, small = _inproj(
            xp, norm_w[l][None], w_all, widths, layer=l, tm=tm_in, n_pad=n_pad, seq_len=seq_p)
        rw_o, wkv, shift = _rwkv_prompt(p_rw, rw_z, rw_p, batch=bp, n_steps=n_steps,
                                        n_sub=n_sub, n_heads=rw_heads, head_dim=rw_hd)
        sm_o, ssm, sconv = _ssd_prompt(xbc, small, ssm_z, sm_p, batch=bp, n_steps=n_steps,
                                       n_sub=n_sub, n_heads=ssm_heads, head_dim=ssm_hd,
                                       n_groups=ssm_groups, d_state=ssm_state, n_pad=n_pad)
        gd_o, gdn, gconv = _gdn_prompt(qkv, small, gdn_z, gd_p, batch=bp, n_steps=n_steps,
                                       n_sub=n_sub, n_heads=gdn_heads, head_dim=gdn_hd,
                                       a_off=a_off, b_off=b_off)
        last = l == depth - 1
        res = _outproj(rw_o, sm_o, gd_o, gate, xp, w_r[l], w_s[l], w_g[l], w_o[l], fw,
                       tm=tm_out, with_final=last)
        xp = res[0]
        if last:
            y_p = res[1]
        for acc, s in zip(p_states, (wkv, shift.reshape(bp, rw_shift_w), ssm, sconv, gdn, gconv)):
            acc.append(s)
    y_prompt = y_p.reshape(bp, seq_p, d_model)[:, n_pad + n_meta:]

    xs = x_sample.reshape(bd, d_model)
    shift_all = state_rwkv_shift.reshape(depth * bd, -1)
    sconv_all = state_ssm_conv.reshape(depth * bd, -1)
    gconv_all = state_gdn_conv.reshape(depth * bd, -1)
    wkv_new, ssm_new, gdn_new = (jnp.zeros(a.shape, F32)
                                 for a in (state_rwkv_wkv, state_ssm, state_gdn))
    small_states = [[] for _ in range(3)]
    y_s = None
    for l in range(depth):
        rw_p, sm_p, gd_p = layer_params(l)
        p_rw, rw_z, ssm_z, xbc, qkv, gdn_z, gate, small = _inproj(
            xs, norm_w[l][None], w_all, widths, layer=l, tm=bd, n_pad=0, seq_len=bd)
        rw_o, wkv_new = _rwkv_decode(p_rw, rw_z, shift_all, state_rwkv_wkv, wkv_new, rw_p,
                                     layer=l)
        sm_o, sconv, ssm_new = _ssd_decode(xbc, small, ssm_z, sconv_all, state_ssm, ssm_new, sm_p,
                                           layer=l, n_groups=ssm_groups)
        gd_o, gconv, gdn_new = _gdn_decode(qkv, small, gdn_z, gconv_all, state_gdn, gdn_new, gd_p,
                                           layer=l, a_off=a_off, b_off=b_off)
        last = l == depth - 1
        res = _outproj(rw_o, sm_o, gd_o, gate, xs, w_r[l], w_s[l], w_g[l], w_o[l], fw,
                       tm=bd, with_final=last)
        xs = res[0]
        if last:
            y_s = res[1]
        for acc, s in zip(small_states, (p_rw, sconv, gconv)):
            acc.append(s)
    y_sample = y_s.reshape(bd, 1, d_model)
    s_shift, s_sconv, s_gconv = (jnp.stack(a) for a in small_states)
    s_states = (wkv_new, s_shift, ssm_new, s_sconv.reshape(state_ssm_conv.shape),
                gdn_new, s_gconv.reshape(state_gdn_conv.shape))

    return (y_prompt, y_sample) + tuple(jnp.stack(a) for a in p_states) + s_states
```

```python
import functools
import math

import jax
import jax.numpy as jnp
from jax import lax
from jax.experimental import pallas as pl
from jax.experimental.pallas import tpu as pltpu

F32 = jnp.float32
BF16 = jnp.bfloat16
HIGHEST = lax.Precision.HIGHEST

NORM_EPS = 1e-6
RW_GN_EPS = 64e-5
SSM_NORM_EPS = 1e-5
GDN_NORM_EPS = 1e-6
L2_EPS = 1e-6
CONV_W = 4
CHUNK = 64
MAX_CHUNKS_PER_STEP = 3
INV_BASE = 8
SEG_TILE = 256
DECODE_UNROLL = 8
SMALL_W = 128
CONV_HALO = 8
VMEM_LIMIT = 56 * 1024 * 1024


def _mm(a, b, exact=False):
    if exact:
        return jnp.dot(a, b, preferred_element_type=F32, precision=HIGHEST)
    return jnp.dot(a.astype(BF16), b.astype(BF16), preferred_element_type=F32)


def _mm_nt(a, b, exact=False):
    dn = (((1,), (1,)), ((), ()))
    if exact:
        return lax.dot_general(a, b, dn, preferred_element_type=F32, precision=HIGHEST)
    return lax.dot_general(a.astype(BF16), b.astype(BF16), dn, preferred_element_type=F32)


def _mm_tn(a, b, exact=False):
    dn = (((0,), (0,)), ((), ()))
    if exact:
        return lax.dot_general(a, b, dn, preferred_element_type=F32, precision=HIGHEST)
    return lax.dot_general(a.astype(BF16), b.astype(BF16), dn, preferred_element_type=F32)


def _sigmoid(x):
    return 1.0 / (1.0 + jnp.exp(-x))


def _silu(x):
    return x * _sigmoid(x)


def _softplus(x):
    return jnp.maximum(x, 0.0) + jnp.log(1.0 + jnp.exp(-jnp.abs(x)))


def _iota2(shape, dim):
    return lax.broadcasted_iota(jnp.int32, shape, dim)


def _seg_sum(x, seg_ones, split=False):
    w = seg_ones.shape[1]
    parts = []
    for i in range(x.shape[1] // w):
        xi = x[:, i * w:(i + 1) * w]
        x_hi = xi.astype(BF16)
        if split:
            x_lo = (xi - x_hi.astype(F32)).astype(BF16)
            parts.append(jnp.dot(jnp.concatenate([x_hi, x_lo], axis=1), seg_ones,
                                 preferred_element_type=F32))
        else:
            parts.append(jnp.dot(x_hi, seg_ones[:w], preferred_element_type=F32))
    return jnp.concatenate(parts, axis=1)


def _block_diag_pair(p):
    c = p.shape[0]
    lane = _iota2(p.shape, 1)
    pb = p.astype(BF16)
    zero = jnp.zeros_like(pb)
    return jnp.concatenate([jnp.where(lane < c, pb, zero), jnp.where(lane >= c, pb, zero)], axis=0)


def _mm_pair(x, p):
    return jnp.dot(x.astype(BF16), _block_diag_pair(p), preferred_element_type=F32)


def _unit_lower_inverse(l_mats):
    c = l_mats[0].shape[0]
    pairs = [jnp.concatenate(l_mats[i:i + 2], axis=1) for i in range(0, len(l_mats), 2)]
    ri = _iota2((c, 2 * c), 0)
    ci = _iota2((c, 2 * c), 1) & (c - 1)
    base_shift = int(math.log2(INV_BASE))
    same_base = (ri >> base_shift) == (ci >> base_shift)
    es = [jnp.where(same_base, l, 0.0) for l in pairs]
    ps = es
    n_terms = 1
    while 2 * n_terms < INV_BASE:
        ps = [_mm_pair(p, p) for p in ps]
        eps = [_mm_pair(e, p) for e, p in zip(es, ps)]
        es = [e + p + ep for e, p, ep in zip(es, ps, eps)]
        n_terms *= 2
    shift = base_shift
    while (1 << shift) < c:
        same_outer = (ri >> (shift + 1)) == (ci >> (shift + 1))
        other_inner = (ri >> shift) != (ci >> shift)
        l_off = [jnp.where(same_outer, jnp.where(other_inner, l, 0.0), 0.0) for l in pairs]
        ts = [lo + _mm_pair(lo, e) for lo, e in zip(l_off, es)]
        ets = [_mm_pair(e, t) for e, t in zip(es, ts)]
        es = [e + t + et for e, t, et in zip(es, ts, ets)]
        shift += 1
    out = []
    for e in es:
        out += [e[:, :c], e[:, c:]]
    return out


def _chunk_masks(c):
    ri = _iota2((c, c), 0)
    ci = _iota2((c, c), 1)
    return ri >= ci, ri > ci


def _block_tril(rows, c):
    shift = int(math.log2(c))
    ri = _iota2((rows, rows), 0)
    ci = _iota2((rows, rows), 1)
    return jnp.where((ri >> shift) == (ci >> shift),
                     jnp.where(ri >= ci, 1.0, 0.0), 0.0).astype(F32)


def _conv_chunk(ext_ref, u, w_ref, first_chunk):
    c = u.shape[0]

    @pl.when(first_chunk)
    def _():
        ext_ref[0:CONV_HALO, :] = jnp.zeros((CONV_HALO, u.shape[1]), F32)

    ext_ref[CONV_HALO:CONV_HALO + c, :] = u
    out = u * w_ref[CONV_W - 1:CONV_W, :]
    for i in range(CONV_W - 1):
        off = CONV_HALO - (CONV_W - 1) + i
        out = out + ext_ref[off:off + c, :] * w_ref[i:i + 1, :]
    tail = ext_ref[c:c + CONV_HALO, :]
    ext_ref[0:CONV_HALO, :] = tail
    return out, tail[CONV_HALO - (CONV_W - 1):, :]


def _inproj_body(x_ref, nw_ref, w_ref, *o_refs, n_pad, tiles_per_seq):
    x = x_ref[...]
    h = x * lax.rsqrt(jnp.mean(x * x, axis=-1, keepdims=True) + NORM_EPS) * nw_ref[...]
    if n_pad:
        first = lax.rem(pl.program_id(0), tiles_per_seq) == 0
        row = _iota2((x.shape[0], 1), 0)
        h = jnp.where(jnp.logical_and(first, row < n_pad), 0.0, h)
    hb = h.astype(BF16)
    off = 0
    for o_ref in o_refs:
        width = o_ref.shape[1]
        o_ref[...] = jnp.dot(hb, w_ref[:, off:off + width], preferred_element_type=F32)
        off += width


def _inproj(x, norm_w, w_all, widths, *, layer, tm, n_pad, seq_len):
    t, d = x.shape
    assert w_all.shape[2] == sum(widths)
    in_specs = [pl.BlockSpec((tm, d), lambda i: (i, 0)),
                pl.BlockSpec((1, d), lambda i: (0, 0)),
                pl.BlockSpec((None,) + w_all.shape[1:], lambda i: (layer, 0, 0),
                             pipeline_mode=pl.Buffered(1))]
    out_specs = [pl.BlockSpec((tm, w), lambda i: (i, 0)) for w in widths]
    out_shape = [jax.ShapeDtypeStruct((t, w), F32) for w in widths]
    body = functools.partial(_inproj_body, n_pad=n_pad,
                             tiles_per_seq=(seq_len // tm) if n_pad else 1)
    return pl.pallas_call(
        body, grid=(t // tm,), in_specs=in_specs, out_specs=out_specs, out_shape=out_shape,
        compiler_params=pltpu.CompilerParams(dimension_semantics=("arbitrary",),
                                             vmem_limit_bytes=VMEM_LIMIT),
        name="inproj",
    )(x, norm_w, w_all)


def _pack_weights_body(w_ref, o_ref, *, segments, pad):
    w = w_ref[...]
    parts = [w[:, lo:hi] for lo, hi in segments]
    if pad:
        parts.append(jnp.zeros((w.shape[0], pad), w.dtype))
    o_ref[...] = jnp.concatenate(parts, axis=1).astype(o_ref.dtype)


def _pack_weights(w, segments, pad, *, rows):
    depth, d, n = w.shape
    total = sum(hi - lo for lo, hi in segments) + pad
    return pl.pallas_call(
        functools.partial(_pack_weights_body, segments=tuple(segments), pad=pad),
        grid=(depth, d // rows),
        in_specs=[pl.BlockSpec((None, rows, n), lambda l, i: (l, i, 0))],
        out_specs=pl.BlockSpec((None, rows, total), lambda l, i: (l, i, 0)),
        out_shape=jax.ShapeDtypeStruct((depth, d, total), BF16),
        compiler_params=pltpu.CompilerParams(dimension_semantics=("arbitrary", "arbitrary"),
                                             vmem_limit_bytes=VMEM_LIMIT),
        name="pack_weights",
    )(w)


def _outproj_body(rw_ref, sm_ref, gd_ref, gate_ref, x_ref, wr_ref, ws_ref, wg_ref, wo_ref,
                  fw_ref, o_ref, *maybe_y_ref, d_model):
    merged = None
    for b, (y_ref, w_ref) in enumerate(((rw_ref, wr_ref), (sm_ref, ws_ref), (gd_ref, wg_ref))):
        g = _sigmoid(gate_ref[:, b * d_model:(b + 1) * d_model])
        term = g * jnp.dot(y_ref[...].astype(BF16), w_ref[...], preferred_element_type=F32)
        merged = term if merged is None else merged + term
    x_new = x_ref[...] + jnp.dot(merged.astype(BF16), wo_ref[...], preferred_element_type=F32)
    o_ref[...] = x_new
    if maybe_y_ref:
        ms = jnp.mean(x_new * x_new, axis=-1, keepdims=True)
        maybe_y_ref[0][...] = x_new * lax.rsqrt(ms + NORM_EPS) * fw_ref[...]


def _outproj(rw, sm, gd, gate, x, w_r, w_s, w_g, w_o, final_w, *, tm, with_final):
    t, d = x.shape
    grid = (t // tm,)
    row = lambda i: (i, 0)
    fixed = lambda i: (0, 0)
    in_specs = [pl.BlockSpec((tm, rw.shape[1]), row), pl.BlockSpec((tm, sm.shape[1]), row),
                pl.BlockSpec((tm, gd.shape[1]), row), pl.BlockSpec((tm, gate.shape[1]), row),
                pl.BlockSpec((tm, d), row),
                pl.BlockSpec(w_r.shape, fixed), pl.BlockSpec(w_s.shape, fixed),
                pl.BlockSpec(w_g.shape, fixed), pl.BlockSpec(w_o.shape, fixed),
                pl.BlockSpec((1, d), fixed)]
    n_out = 2 if with_final else 1
    out_specs = [pl.BlockSpec((tm, d), row)] * n_out
    out_shape = [jax.ShapeDtypeStruct((t, d), F32)] * n_out
    return pl.pallas_call(
        functools.partial(_outproj_body, d_model=d),
        grid=grid, in_specs=in_specs, out_specs=out_specs, out_shape=out_shape,
        compiler_params=pltpu.CompilerParams(dimension_semantics=("arbitrary",),
                                             vmem_limit_bytes=VMEM_LIMIT),
        name="outproj",
    )(rw, sm, gd, gate, x, w_r, w_s, w_g, w_o, final_w)


def _rwkv_pre(p, shifted, mu, w0, w2, a0, a2, k_k, k_a, width, lora):
    u = p + (shifted - p) * mu
    r = u[:, :width]
    k = u[:, width:2 * width]
    v = u[:, 2 * width:3 * width]
    wd = u[:, 3 * width:3 * width + lora]
    ad = u[:, 3 * width + lora:]
    w = -_softplus(-(w0 + _mm(jnp.tanh(wd), w2))) - 0.5
    log_decay = -jnp.exp(w)
    a = _sigmoid(a0 + _mm(ad, a2))
    kk_raw = k * k_k
    k_mod = k * (1.0 + (a - 1.0) * k_a)
    return r, k_mod, v, log_decay, a, kk_raw


def _l2norm(x):
    return x * lax.rsqrt(jnp.sum(x * x, axis=-1, keepdims=True) + L2_EPS)


def _rwkv_post_head(y, r, k_mod, v, z, r_k, gn_w, gn_b):
    mean = jnp.mean(y, axis=-1, keepdims=True)
    var = jnp.mean(jnp.square(y - mean), axis=-1, keepdims=True)
    yn = (y - mean) * lax.rsqrt(var + RW_GN_EPS) * gn_w + gn_b
    bonus = jnp.sum(r * k_mod * r_k, axis=-1, keepdims=True) * v
    return (yn + bonus) * _silu(z)


def _rwkv_prompt_body(p_ref, z_ref, ones_ref, mu_ref, w0_ref, w2_ref, a0_ref, a2_ref, kk_ref,
                      ka_ref, rk_ref, gnw_ref, gnb_ref, o_ref, wkv_ref, shift_ref, prev_scr,
                      *, n_heads, head_dim, n_sub):
    width = n_heads * head_dim
    lora = w2_ref.shape[0]
    p = p_ref[...]
    rows = p.shape[0]
    c = rows // n_sub

    @pl.when(pl.program_id(1) == 0)
    def _():
        wkv_ref[...] = jnp.zeros(wkv_ref.shape, F32)
        prev_scr[...] = jnp.zeros(prev_scr.shape, F32)

    row = _iota2((rows, 1), 0)
    shifted = jnp.where(row == 0, prev_scr[...], pltpu.roll(p, 1, 0))
    last_row = p[rows - 1:rows, :]
    prev_scr[...] = last_row
    shift_ref[0] = last_row

    r, k_mod, v, log_decay, a, kk_raw = _rwkv_pre(
        p, shifted, mu_ref[...], w0_ref[...], w2_ref[...], a0_ref[...], a2_ref[...],
        kk_ref[...], ka_ref[...], width, lora)
    cum = _mm(_block_tril(rows, c), log_decay, exact=True)
    e_pos = jnp.exp(cum)
    e_neg = jnp.exp(-cum)
    e_prev = jnp.exp(cum - log_decay)
    z = z_ref[...]
    incl, strict = _chunk_masks(c)

    hs = range(n_heads)
    sls = [slice(h * head_dim, (h + 1) * head_dim) for h in hs]
    units = [(j, h) for j in range(n_sub) for h in hs]
    rsl = [slice(j * c, (j + 1) * c) for j in range(n_sub)]
    seg_ones = ones_ref[...]
    kk = kk_raw * lax.rsqrt(_seg_sum(kk_raw * kk_raw, seg_ones) + L2_EPS)
    a_all = -kk * e_prev
    b_all = kk * a * e_neg
    k_all = k_mod * e_neg
    r_all = r * e_pos
    a_t = [a_all[:, sl] for sl in sls]
    b_t = [b_all[:, sl] for sl in sls]
    k_t = [k_all[:, sl] for sl in sls]
    r_t = [r_all[:, sl] for sl in sls]
    v_h = [v[:, sl] for sl in sls]

    sc = {u: _mm_nt(jnp.concatenate([a_t[u[1]][rsl[u[0]]], r_t[u[1]][rsl[u[0]]]], axis=0),
                    jnp.concatenate([b_t[u[1]][rsl[u[0]]], k_t[u[1]][rsl[u[0]]]], axis=0))
          for u in units}
    l_ab = {u: jnp.where(strict, sc[u][:c, :c], 0.0) for u in units}
    l_ak = {u: jnp.where(strict, sc[u][:c, c:], 0.0) for u in units}
    m_rb = {u: jnp.where(incl, sc[u][c:, :c], 0.0) for u in units}
    m_rk = {u: jnp.where(incl, sc[u][c:, c:], 0.0) for u in units}
    lv = {u: _mm(jnp.concatenate([l_ak[u], m_rk[u]], axis=0), v_h[u[1]][rsl[u[0]]]) for u in units}
    inv = dict(zip(units, _unit_lower_inverse([l_ab[u] for u in units])))
    rhs = {u: jnp.concatenate([a_t[u[1]][rsl[u[0]]], lv[u][:c]], axis=1) for u in units}
    wu = {u: rhs[u] + _mm(inv[u], rhs[u]) for u in units}
    d_end = {(j, h): e_pos[(j + 1) * c - 1:(j + 1) * c, sls[h]] for j, h in units}
    qy = {u: _mm(m_rb[u], wu[u]) for u in units}
    mn = {u: _mm_tn(wu[u], b_t[u[1]][rsl[u[0]]] * d_end[u]) for u in units}
    n2 = {u: _mm_tn(v_h[u[1]][rsl[u[0]]], k_t[u[1]][rsl[u[0]]] * d_end[u]) for u in units}

    states = [wkv_ref[0, h] for h in hs]
    ys = [[] for _ in hs]
    for j in range(n_sub):
        y_dot = [_mm_nt(r_t[h][rsl[j]] + qy[(j, h)][:, :c], states[h]) for h in hs]
        s_dot = [_mm(states[h], mn[(j, h)][:c]) for h in hs]
        for h in hs:
            ys[h].append(y_dot[h] + qy[(j, h)][:, c:] + lv[(j, h)][c:])
            states[h] = states[h] * d_end[(j, h)] + s_dot[h] + (mn[(j, h)][c:] + n2[(j, h)])
    for h in hs:
        wkv_ref[0, h] = states[h]
    y = jnp.concatenate([jnp.concatenate(ys[h], axis=0) for h in hs], axis=-1)
    inv_n = 1.0 / head_dim
    dev = y - _seg_sum(y, seg_ones) * inv_n
    var = _seg_sum(dev * dev, seg_ones) * inv_n
    yn = dev * lax.rsqrt(var + RW_GN_EPS) * gnw_ref[...] + gnb_ref[...]
    bonus = _seg_sum(r * k_mod * rk_ref[...], seg_ones, split=True) * v
    o_ref[...] = (yn + bonus) * _silu(z)


def _rwkv_prompt(p_rw, z, prm, *, batch, n_steps, n_sub, n_heads, head_dim):
    t, shift_w = p_rw.shape
    width = n_heads * head_dim
    rows = n_sub * CHUNK
    row = lambda b, c: (b * n_steps + c, 0)
    fixed = lambda b, c: (0, 0)
    names = ("mu", "w0", "w2", "a0", "a2", "k_k", "k_a", "r_k", "gn_w", "gn_b")
    params = [_segment_ones(SEG_TILE, head_dim)] + [prm[n] for n in names]
    in_specs = [pl.BlockSpec((rows, shift_w), row), pl.BlockSpec((rows, width), row)]
    in_specs += [pl.BlockSpec(a.shape, fixed) for a in params]
    out_specs = [pl.BlockSpec((rows, width), row),
                 pl.BlockSpec((1, n_heads, head_dim, head_dim), lambda b, c: (b, 0, 0, 0)),
                 pl.BlockSpec((1, 1, shift_w), lambda b, c: (b, 0, 0))]
    out_shape = [jax.ShapeDtypeStruct((t, width), F32),
                 jax.ShapeDtypeStruct((batch, n_heads, head_dim, head_dim), F32),
                 jax.ShapeDtypeStruct((batch, 1, shift_w), F32)]
    return pl.pallas_call(
        functools.partial(_rwkv_prompt_body, n_heads=n_heads, head_dim=head_dim, n_sub=n_sub),
        grid=(batch, n_steps), in_specs=in_specs, out_specs=out_specs, out_shape=out_shape,
        scratch_shapes=[pltpu.VMEM((1, shift_w), F32)],
        compiler_params=pltpu.CompilerParams(dimension_semantics=("arbitrary", "arbitrary"),
                                             vmem_limit_bytes=VMEM_LIMIT),
        name="rwkv_prompt",
    )(p_rw, z, *params)


def _ssm_gated_norm(y, z, norm_w, n_groups):
    g = y * _silu(z)
    gw = g.shape[1] // n_groups
    parts = []
    for i in range(n_groups):
        gi = g[:, i * gw:(i + 1) * gw]
        parts.append(gi * lax.rsqrt(jnp.mean(gi * gi, axis=-1, keepdims=True) + SSM_NORM_EPS))
    return jnp.concatenate(parts, axis=-1) * norm_w


def _ssd_prompt_body(xbc_ref, small_ref, z_ref, cw_ref, cb_ref, dtb_row_ref, dtb_col_ref,
                     alog_row_ref, alog_col_ref, d_ref, nw_ref, o_ref, h_ref, conv_ref, ext_scr,
                     *, n_heads, head_dim, n_groups, d_state, n_pad, n_sub):
    first = pl.program_id(1) == 0
    width = n_heads * head_dim
    hpg = n_heads // n_groups
    gw = hpg * head_dim
    x_pre = xbc_ref[...]
    rows = x_pre.shape[0]
    c = rows // n_sub
    rsl = [slice(j * c, (j + 1) * c) for j in range(n_sub)]

    @pl.when(first)
    def _():
        h_ref[...] = jnp.zeros(h_ref.shape, F32)

    conv, tail = _conv_chunk(ext_scr, x_pre, cw_ref, first)
    conv_ref[0] = tail
    xbc = _silu(conv + cb_ref[...])
    xs = xbc[:, :width]
    b_all = xbc[:, width:width + n_groups * d_state]
    c_all = xbc[:, width + n_groups * d_state:]

    small = small_ref[...]
    lane = _iota2((1, SMALL_W), 1)
    sub = _iota2((SMALL_W, 1), 0)
    dt = _softplus(small + dtb_row_ref[...])
    dt_t = [_softplus(small[sl].T + dtb_col_ref[...]) for sl in rsl]
    if n_pad:
        assert n_pad <= c
        not_first = jnp.logical_not(first)
        dt = jnp.where(jnp.logical_or(not_first, _iota2((rows, 1), 0) >= n_pad), dt, 0.0)
        dt_t[0] = jnp.where(jnp.logical_or(not_first, _iota2((1, c), 1) >= n_pad), dt_t[0], 0.0)
    a_row = jnp.where(lane < n_heads, -jnp.exp(alog_row_ref[...]), 0.0)
    a_col = jnp.where(sub < n_heads, -jnp.exp(alog_col_ref[...]), 0.0)
    incl, _ = _chunk_masks(c)
    triu = jnp.where(_iota2((c, c), 0) <= _iota2((c, c), 1), 1.0, 0.0).astype(F32)
    acs = _mm(_block_tril(rows, c), dt * a_row, exact=True)
    acs_t = [_mm(t * a_col, triu, exact=True) for t in dt_t]
    expand = jnp.where((_iota2((SMALL_W, width), 1) // head_dim) == _iota2((SMALL_W, width), 0),
                       1.0, 0.0).astype(F32)
    wide = _mm(jnp.concatenate([acs, dt], axis=0), expand, exact=True)
    acs_w = wide[:rows]
    xdt = xs * wide[rows:]
    e_acs = jnp.exp(acs_w)

    hs = range(n_heads)
    gs = range(n_groups)
    js = range(n_sub)
    b_g = {(j, g): b_all[rsl[j], g * d_state:(g + 1) * d_state] for j in js for g in gs}
    c_g = {(j, g): c_all[rsl[j], g * d_state:(g + 1) * d_state] for j in js for g in gs}
    a_last = {j: acs_w[(j + 1) * c - 1:(j + 1) * c] for j in js}
    xdt_end = {j: xdt[rsl[j]] * jnp.exp(a_last[j] - acs_w[rsl[j]]) for j in js}

    cb = {jg: _mm_nt(c_g[jg], b_g[jg]) for jg in b_g}
    seg = {(j, h): jnp.exp(jnp.where(incl, acs[rsl[j], h:h + 1] - acs_t[j][h:h + 1, :], -jnp.inf))
           for j in js for h in hs}
    y_diag = {(j, h): _mm(cb[(j, h // hpg)] * seg[(j, h)],
                          xdt[rsl[j], h * head_dim:(h + 1) * head_dim])
              for j in js for h in hs}
    st = {(j, g): _mm_tn(xdt_end[j][:, g * gw:(g + 1) * gw], b_g[(j, g)]) for j in js for g in gs}

    states = [h_ref[0, g * hpg:(g + 1) * hpg].reshape(gw, d_state) for g in gs]
    ys = []
    for j in js:
        y_off = [_mm_nt(c_g[(j, g)], states[g]) for g in gs]
        ys.append(jnp.concatenate(y_off, axis=-1) * e_acs[rsl[j]]
                  + jnp.concatenate([y_diag[(j, h)] for h in hs], axis=-1))
        for g in gs:
            scaled = [states[g][i * head_dim:(i + 1) * head_dim]
                      * jnp.exp(acs[(j + 1) * c - 1:(j + 1) * c, g * hpg + i:g * hpg + i + 1])
                      for i in range(hpg)]
            states[g] = jnp.concatenate(scaled, axis=0) + st[(j, g)]
    for g in gs:
        h_ref[0, g * hpg:(g + 1) * hpg] = states[g].reshape(hpg, head_dim, d_state)
    y_all = jnp.concatenate(ys, axis=0) + xs * d_ref[...]
    o_ref[...] = _ssm_gated_norm(y_all, z_ref[...], nw_ref[...], n_groups)


def _ssd_prompt(xbc, small, z, prm, *, batch, n_steps, n_sub, n_heads, head_dim, n_groups,
                d_state, n_pad):
    t, conv_ch = xbc.shape
    width = n_heads * head_dim
    rows = n_sub * CHUNK
    row = lambda b, c: (b * n_steps + c, 0)
    fixed = lambda b, c: (0, 0)
    names = ("conv_w", "conv_b", "dtb_row", "dtb_col", "alog_row", "alog_col", "d", "norm_w")
    params = [prm[n] for n in names]
    in_specs = [pl.BlockSpec((rows, conv_ch), row), pl.BlockSpec((rows, SMALL_W), row),
                pl.BlockSpec((rows, width), row)]
    in_specs += [pl.BlockSpec(a.shape, fixed) for a in params]
    out_specs = [pl.BlockSpec((rows, width), row),
                 pl.BlockSpec((1, n_heads, head_dim, d_state), lambda b, c: (b, 0, 0, 0)),
                 pl.BlockSpec((1, CONV_W - 1, conv_ch), lambda b, c: (b, 0, 0))]
    out_shape = [jax.ShapeDtypeStruct((t, width), F32),
                 jax.ShapeDtypeStruct((batch, n_heads, head_dim, d_state), F32),
                 jax.ShapeDtypeStruct((batch, CONV_W - 1, conv_ch), F32)]
    return pl.pallas_call(
        functools.partial(_ssd_prompt_body, n_heads=n_heads, head_dim=head_dim,
                          n_groups=n_groups, d_state=d_state, n_pad=n_pad, n_sub=n_sub),
        grid=(batch, n_steps), in_specs=in_specs, out_specs=out_specs, out_shape=out_shape,
        scratch_shapes=[pltpu.VMEM((rows + CONV_HALO, conv_ch), F32)],
        compiler_params=pltpu.CompilerParams(dimension_semantics=("arbitrary", "arbitrary"),
                                             vmem_limit_bytes=VMEM_LIMIT),
        name="ssd_prompt",
    )(xbc, small, z, *params)


def _gdn_post_head(o, z, norm_w):
    return o * lax.rsqrt(jnp.mean(o * o, axis=-1, keepdims=True) + GDN_NORM_EPS) * norm_w * _silu(z)


def _gdn_prompt_body(qkv_ref, small_ref, z_ref, cw_ref, dtb_row_ref, dtb_col_ref, alog_row_ref,
                     alog_col_ref, nw_ref, o_ref, s_ref, conv_ref, ext_scr,
                     *, n_heads, head_dim, a_off, b_off, n_sub):
    first = pl.program_id(1) == 0
    width = n_heads * head_dim
    x_pre = qkv_ref[...]
    rows = x_pre.shape[0]
    c = rows // n_sub

    @pl.when(first)
    def _():
        s_ref[...] = jnp.zeros(s_ref.shape, F32)

    conv, tail = _conv_chunk(ext_scr, x_pre, cw_ref, first)
    conv_ref[0] = tail
    qkv = _silu(conv)
    small = small_ref[...]
    rsl = [slice(j * c, (j + 1) * c) for j in range(n_sub)]
    small_t = [small[sl].T for sl in rsl]
    lane = _iota2((1, SMALL_W), 1)
    sub = _iota2((SMALL_W, 1), 0)
    g = jnp.where(jnp.logical_and(lane >= a_off, lane < a_off + n_heads),
                  -jnp.exp(alog_row_ref[...]) * _softplus(small + dtb_row_ref[...]), 0.0)
    g_t = [jnp.where(jnp.logical_and(sub >= a_off, sub < a_off + n_heads),
                     -jnp.exp(alog_col_ref[...]) * _softplus(st + dtb_col_ref[...]), 0.0)
           for st in small_t]
    beta = _sigmoid(small)
    incl, strict = _chunk_masks(c)
    triu = jnp.where(_iota2((c, c), 0) <= _iota2((c, c), 1), 1.0, 0.0).astype(F32)
    gcs = _mm(_block_tril(rows, c), g, exact=True)
    gcs_t = [_mm(gt, triu, exact=True) for gt in g_t]
    z = z_ref[...]
    scale = head_dim ** -0.5

    hs = range(n_heads)
    sls = [slice(h * head_dim, (h + 1) * head_dim) for h in hs]
    units = [(j, h) for j in range(n_sub) for h in hs]
    q_h = [_l2norm(qkv[:, sl]) * scale for sl in sls]
    k_h = [_l2norm(qkv[:, width + h * head_dim:width + (h + 1) * head_dim]) for h in hs]
    v_h = [qkv[:, 2 * width + h * head_dim:2 * width + (h + 1) * head_dim] for h in hs]
    g_c = {(j, h): gcs[rsl[j], a_off + h:a_off + h + 1] for j, h in units}
    g_r = {(j, h): gcs_t[j][a_off + h:a_off + h + 1, :] for j, h in units}
    g_last = {(j, h): gcs[(j + 1) * c - 1:(j + 1) * c, a_off + h:a_off + h + 1] for j, h in units}
    b_c = {(j, h): beta[rsl[j], b_off + h:b_off + h + 1] for j, h in units}
    q = {(j, h): q_h[h][rsl[j]] for j, h in units}
    k = {(j, h): k_h[h][rsl[j]] for j, h in units}
    v = {(j, h): v_h[h][rsl[j]] for j, h in units}

    decay = {u: jnp.exp(jnp.where(incl, g_c[u] - g_r[u], -jnp.inf)) for u in units}
    kb = {u: k[u] * b_c[u] for u in units}
    e_c = {u: jnp.exp(g_c[u]) for u in units}
    sc = {u: _mm_nt(jnp.concatenate([kb[u], q[u]], axis=0), k[u]) for u in units}
    a_mat = {u: jnp.where(strict, sc[u][:c] * decay[u], 0.0) for u in units}
    qk = {u: jnp.where(incl, sc[u][c:] * decay[u], 0.0) for u in units}
    inv = dict(zip(units, _unit_lower_inverse([-a_mat[u] for u in units])))
    rhs = {u: jnp.concatenate([v[u] * b_c[u], kb[u] * e_c[u]], axis=1) for u in units}
    uw = {u: rhs[u] + _mm(inv[u], rhs[u]) for u in units}
    qo = {u: _mm(qk[u], uw[u]) for u in units}
    mn = {u: _mm_tn(k[u] * jnp.exp(g_last[u] - g_c[u]), uw[u]) for u in units}

    states = [s_ref[0, h] for h in hs]
    os_ = [[] for _ in hs]
    for j in range(n_sub):
        o_dot = [_mm(q[(j, h)] * e_c[(j, h)] - qo[(j, h)][:, head_dim:], states[h]) for h in hs]
        s_dot = [_mm(mn[(j, h)][:, head_dim:], states[h]) for h in hs]
        for h in hs:
            os_[h].append(o_dot[h] + qo[(j, h)][:, :head_dim])
            states[h] = (states[h] * jnp.exp(g_last[(j, h)]) - s_dot[h]
                         + mn[(j, h)][:, :head_dim])
    for h in hs:
        s_ref[0, h] = states[h]
    outs = [_gdn_post_head(jnp.concatenate(os_[h], axis=0), z[:, sls[h]], nw_ref[...])
            for h in hs]
    o_ref[...] = jnp.concatenate(outs, axis=-1)


def _gdn_prompt(qkv, small, z, prm, *, batch, n_steps, n_sub, n_heads, head_dim, a_off, b_off):
    t, conv_ch = qkv.shape
    width = n_heads * head_dim
    rows = n_sub * CHUNK
    row = lambda b, c: (b * n_steps + c, 0)
    fixed = lambda b, c: (0, 0)
    names = ("conv_w", "dtb_row", "dtb_col", "alog_row", "alog_col", "norm_w")
    params = [prm[n] for n in names]
    in_specs = [pl.BlockSpec((rows, conv_ch), row), pl.BlockSpec((rows, SMALL_W), row),
                pl.BlockSpec((rows, width), row)]
    in_specs += [pl.BlockSpec(a.shape, fixed) for a in params]
    out_specs = [pl.BlockSpec((rows, width), row),
                 pl.BlockSpec((1, n_heads, head_dim, head_dim), lambda b, c: (b, 0, 0, 0)),
                 pl.BlockSpec((1, CONV_W - 1, conv_ch), lambda b, c: (b, 0, 0))]
    out_shape = [jax.ShapeDtypeStruct((t, width), F32),
                 jax.ShapeDtypeStruct((batch, n_heads, head_dim, head_dim), F32),
                 jax.ShapeDtypeStruct((batch, CONV_W - 1, conv_ch), F32)]
    return pl.pallas_call(
        functools.partial(_gdn_prompt_body, n_heads=n_heads, head_dim=head_dim,
                          a_off=a_off, b_off=b_off, n_sub=n_sub),
        grid=(batch, n_steps), in_specs=in_specs, out_specs=out_specs, out_shape=out_shape,
        scratch_shapes=[pltpu.VMEM((rows + CONV_HALO, conv_ch), F32)],
        compiler_params=pltpu.CompilerParams(dimension_semantics=("arbitrary", "arbitrary"),
                                             vmem_limit_bytes=VMEM_LIMIT),
        name="gdn_prompt",
    )(qkv, small, z, *params)


def _conv_step(u, conv_ref, convo_ref, w_ref):
    ch = u.shape[1]
    out = u * w_ref[CONV_W - 1:CONV_W, :]
    for i in range(CONV_W - 1):
        out = out + conv_ref[:, i * ch:(i + 1) * ch] * w_ref[i:i + 1, :]
    for i in range(CONV_W - 2):
        convo_ref[:, i * ch:(i + 1) * ch] = conv_ref[:, (i + 1) * ch:(i + 2) * ch]
    convo_ref[:, (CONV_W - 2) * ch:] = u
    return out


def _lane_pick(x, lane, idx):
    return jnp.sum(jnp.where(lane == idx, x, 0.0), axis=1, keepdims=True)


def _rwkv_decode_body(p_ref, z_ref, prev_ref, s_ref, mu_ref, w0_ref, w2_ref, a0_ref, a2_ref,
                      kk_ref, ka_ref, rk_ref, gnw_ref, gnb_ref, o_ref, so_ref,
                      t_scr, y_scr, row_scr, *, n_heads, head_dim):
    h = pl.program_id(0)
    width = n_heads * head_dim
    lora = w2_ref.shape[0]
    bsz = p_ref.shape[0]

    @pl.when(h == 0)
    def _():
        r, k_mod, v, log_decay, a, kk_raw = _rwkv_pre(
            p_ref[...], prev_ref[...], mu_ref[...], w0_ref[...], w2_ref[...], a0_ref[...],
            a2_ref[...], kk_ref[...], ka_ref[...], width, lora)
        kk = jnp.concatenate(
            [_l2norm(kk_raw[:, i * head_dim:(i + 1) * head_dim]) for i in range(n_heads)], axis=-1)
        row_scr[0] = r
        row_scr[1] = k_mod
        row_scr[2] = v
        for i, arr in enumerate((r, jnp.exp(log_decay), k_mod, v, -kk, kk * a)):
            t_scr[i] = arr.T.reshape(n_heads, head_dim, bsz)

    r_t = t_scr[0, h]
    w_t = t_scr[1, h]
    k_t = t_scr[2, h]
    a_t = t_scr[4, h]
    b_t = t_scr[5, h]

    def step(i, carry):
        s_v = s_ref[i]
        sa = jnp.sum(s_v * a_t, axis=0, keepdims=True)
        v_i = t_scr[3, h, pl.ds(i, 1), :]
        s_new = s_v * w_t + sa * b_t + v_i * k_t
        so_ref[i] = s_new
        y_scr[h, pl.ds(i, 1), :] = jnp.sum(s_new * r_t, axis=0, keepdims=True)
        return carry

    lax.fori_loop(0, head_dim, step, 0, unroll=DECODE_UNROLL)

    @pl.when(h == n_heads - 1)
    def _():
        y = y_scr[...].reshape(width, bsz).T
        r, k_mod, v, z = row_scr[0], row_scr[1], row_scr[2], z_ref[...]
        outs = []
        for i in range(n_heads):
            sl = slice(i * head_dim, (i + 1) * head_dim)
            outs.append(_rwkv_post_head(y[:, sl], r[:, sl], k_mod[:, sl], v[:, sl], z[:, sl],
                                        rk_ref[:, sl], gnw_ref[:, sl], gnb_ref[:, sl]))
        o_ref[...] = jnp.concatenate(outs, axis=-1)


def _layer_state_call(body, name, row_inputs, layer_inputs, state_all, new_all, params, *,
                      layer, n_row_out, scratch_shapes, batch_minor=False):
    bsz = row_inputs[0].shape[0]
    fixed = lambda h: (0, 0)
    at_layer = lambda h: (layer, 0)
    if batch_minor:
        n_heads = state_all.shape[1]
        state_spec = pl.BlockSpec((None, None) + state_all.shape[2:],
                                  lambda h: (layer, h, 0, 0, 0))
    else:
        n_heads = state_all.shape[2]
        state_spec = pl.BlockSpec((None, bsz, None) + state_all.shape[3:],
                                  lambda h: (layer, 0, h, 0, 0))
    in_specs = [pl.BlockSpec(a.shape, fixed) for a in row_inputs]
    in_specs += [pl.BlockSpec((bsz, a.shape[1]), at_layer) for a in layer_inputs]
    in_specs += [state_spec]
    in_specs += [pl.BlockSpec(a.shape, fixed) for a in params]
    in_specs.append(pl.BlockSpec(memory_space=pl.ANY))
    operands = list(row_inputs) + list(layer_inputs) + [state_all] + list(params) + [new_all]
    n_in = len(operands)
    aliases = {n_in - 1: len(n_row_out)}
    inner = body

    def body(*refs):
        return inner(*refs[:n_in - 1], *refs[n_in:])

    out_specs = [pl.BlockSpec((bsz, w), fixed) for w in n_row_out] + [state_spec]
    out_shape = [jax.ShapeDtypeStruct((bsz, w), F32) for w in n_row_out]
    out_shape += [jax.ShapeDtypeStruct(state_all.shape, F32)]
    return pl.pallas_call(
        body, grid=(n_heads,), in_specs=in_specs, out_specs=out_specs, out_shape=out_shape,
        scratch_shapes=scratch_shapes, input_output_aliases=aliases,
        compiler_params=pltpu.CompilerParams(dimension_semantics=("arbitrary",),
                                             vmem_limit_bytes=VMEM_LIMIT),
        name=name,
    )(*operands)


def _rwkv_decode(p_rw, z, prev_all, state_all, new_all, prm, *, layer):
    bsz = p_rw.shape[0]
    n_heads, head_dim = state_all.shape[1], state_all.shape[2]
    width = n_heads * head_dim
    names = ("mu", "w0", "w2", "a0", "a2", "k_k", "k_a", "r_k", "gn_w", "gn_b")
    return _layer_state_call(
        functools.partial(_rwkv_decode_body, n_heads=n_heads, head_dim=head_dim), "rwkv_decode",
        [p_rw, z], [prev_all], state_all, new_all, [prm[n] for n in names],
        layer=layer, n_row_out=[width], batch_minor=True,
        scratch_shapes=[pltpu.VMEM((6, n_heads, head_dim, bsz), F32),
                        pltpu.VMEM((n_heads, head_dim, bsz), F32),
                        pltpu.VMEM((3, bsz, width), F32)])


def _ssd_decode_body(xbc_ref, small_ref, z_ref, conv_ref, s_ref, cw_ref, cb_ref, dtb_row_ref,
                     alog_row_ref, d_ref, nw_ref, o_ref, convo_ref, so_ref,
                     xt_scr, bc_scr, dec_scr, y_scr, skip_scr,
                     *, n_heads, head_dim, n_groups, d_state):
    h = pl.program_id(0)
    width = n_heads * head_dim
    bsz = xbc_ref.shape[0]
    hpg = n_heads // n_groups

    @pl.when(h == 0)
    def _():
        conv = _conv_step(xbc_ref[...], conv_ref, convo_ref, cw_ref)
        xbc = _silu(conv + cb_ref[...])
        lane_s = _iota2((1, SMALL_W), 1)
        dt = _softplus(small_ref[...] + dtb_row_ref[...])
        a_row = jnp.where(lane_s < n_heads, -jnp.exp(alog_row_ref[...]), 0.0)
        dec_scr[...] = jnp.exp(dt * a_row).T
        expand = jnp.where((_iota2((SMALL_W, width), 1) // head_dim) == _iota2((SMALL_W, width), 0),
                           1.0, 0.0).astype(F32)
        xs = xbc[:, :width]
        xt_scr[...] = (xs * _mm(dt, expand, exact=True)).T.reshape(n_heads, head_dim, bsz)
        skip_scr[...] = xs * d_ref[...]
        for i in range(2 * n_groups):
            bc_scr[i] = xbc[:, width + i * d_state:width + (i + 1) * d_state]

    g = h // hpg
    xdt_t = xt_scr[h]
    dec_row = dec_scr[pl.ds(h, 1), :]
    lane = _iota2((head_dim, bsz), 1)
    lane1 = _iota2((1, bsz), 1)

    def step(b, y_t):
        b_row = bc_scr[g, pl.ds(b, 1), :]
        c_row = bc_scr[n_groups + g, pl.ds(b, 1), :]
        s_new = (s_ref[b] * _lane_pick(dec_row, lane1, b)
                 + _lane_pick(xdt_t, lane, b) * b_row)
        so_ref[b] = s_new
        y_col = jnp.sum(s_new * c_row, axis=1, keepdims=True)
        return jnp.where(lane == b, y_col, y_t)

    y_scr[h] = lax.fori_loop(0, bsz, step, jnp.zeros((head_dim, bsz), F32), unroll=DECODE_UNROLL)

    @pl.when(h == n_heads - 1)
    def _():
        y = y_scr[...].reshape(width, bsz).T + skip_scr[...]
        o_ref[...] = _ssm_gated_norm(y, z_ref[...], nw_ref[...], n_groups)


def _ssd_decode(xbc, small, z, conv_all, state_all, new_all, prm, *, layer, n_groups):
    bsz = xbc.shape[0]
    n_heads, head_dim, d_state = state_all.shape[2:]
    width = n_heads * head_dim
    names = ("conv_w", "conv_b", "dtb_row", "alog_row", "d", "norm_w")
    return _layer_state_call(
        functools.partial(_ssd_decode_body, n_heads=n_heads, head_dim=head_dim,
                          n_groups=n_groups, d_state=d_state), "ssd_decode",
        [xbc, small, z], [conv_all], state_all, new_all, [prm[n] for n in names],
        layer=layer, n_row_out=[width, conv_all.shape[1]],
        scratch_shapes=[pltpu.VMEM((n_heads, head_dim, bsz), F32),
                        pltpu.VMEM((2 * n_groups, bsz, d_state), F32),
                        pltpu.VMEM((SMALL_W, bsz), F32),
                        pltpu.VMEM((n_heads, head_dim, bsz), F32),
                        pltpu.VMEM((bsz, width), F32)])


def _gdn_decode_body(qkv_ref, small_ref, z_ref, conv_ref, s_ref, cw_ref, dtb_row_ref,
                     alog_row_ref, nw_ref, o_ref, convo_ref, so_ref,
                     t_scr, v_scr, gate_scr, y_scr, *, n_heads, head_dim, a_off, b_off):
    h = pl.program_id(0)
    width = n_heads * head_dim
    bsz = qkv_ref.shape[0]

    @pl.when(h == 0)
    def _():
        qkv = _silu(_conv_step(qkv_ref[...], conv_ref, convo_ref, cw_ref))
        scale = head_dim ** -0.5
        q = jnp.concatenate([_l2norm(qkv[:, i * head_dim:(i + 1) * head_dim]) * scale
                             for i in range(n_heads)], axis=-1)
        k = jnp.concatenate([_l2norm(qkv[:, width + i * head_dim:width + (i + 1) * head_dim])
                             for i in range(n_heads)], axis=-1)
        for i, arr in enumerate((q, k)):
            t_scr[i] = arr.T.reshape(n_heads, head_dim, bsz)
        for i in range(n_heads):
            v_scr[i] = qkv[:, 2 * width + i * head_dim:2 * width + (i + 1) * head_dim]
        small = small_ref[...]
        lane_s = _iota2((1, SMALL_W), 1)
        g = jnp.where(jnp.logical_and(lane_s >= a_off, lane_s < a_off + n_heads),
                      -jnp.exp(alog_row_ref[...]) * _softplus(small + dtb_row_ref[...]), 0.0)
        gate_scr[0] = jnp.exp(g).T
        gate_scr[1] = _sigmoid(small).T
        gate_scr[2, 0:n_heads, :] = jnp.concatenate(
            [jnp.sum(q[:, i * head_dim:(i + 1) * head_dim] * k[:, i * head_dim:(i + 1) * head_dim],
                     axis=1, keepdims=True) for i in range(n_heads)], axis=1).T

    q_t = t_scr[0, h]
    k_t = t_scr[1, h]
    eg_row = gate_scr[0, pl.ds(a_off + h, 1), :]
    beta_row = gate_scr[1, pl.ds(b_off + h, 1), :]
    qk_row = gate_scr[2, pl.ds(h, 1), :]
    lane = _iota2((head_dim, bsz), 1)
    lane1 = _iota2((1, bsz), 1)

    def step(b, carry):
        s_b = s_ref[b]
        k_col = _lane_pick(k_t, lane, b)
        q_col = _lane_pick(q_t, lane, b)
        eg = _lane_pick(eg_row, lane1, b)
        k_s = jnp.sum(s_b * k_col, axis=0, keepdims=True)
        q_s = jnp.sum(s_b * q_col, axis=0, keepdims=True)
        v_new = _lane_pick(beta_row, lane1, b) * (v_scr[h, pl.ds(b, 1), :] - eg * k_s)
        y_scr[h, pl.ds(b, 1), :] = eg * q_s + _lane_pick(qk_row, lane1, b) * v_new
        so_ref[b] = s_b * eg + k_col * v_new
        return carry

    lax.fori_loop(0, bsz, step, 0, unroll=DECODE_UNROLL)

    @pl.when(h == n_heads - 1)
    def _():
        z = z_ref[...]
        outs = []
        for i in range(n_heads):
            sl = slice(i * head_dim, (i + 1) * head_dim)
            outs.append(_gdn_post_head(y_scr[i], z[:, sl], nw_ref[...]))
        o_ref[...] = jnp.concatenate(outs, axis=-1)


def _gdn_decode(qkv, small, z, conv_all, state_all, new_all, prm, *, layer, a_off, b_off):
    bsz = qkv.shape[0]
    n_heads, head_dim = state_all.shape[2], state_all.shape[3]
    width = n_heads * head_dim
    names = ("conv_w", "dtb_row", "alog_row", "norm_w")
    return _layer_state_call(
        functools.partial(_gdn_decode_body, n_heads=n_heads, head_dim=head_dim,
                          a_off=a_off, b_off=b_off), "gdn_decode",
        [qkv, small, z], [conv_all], state_all, new_all, [prm[n] for n in names],
        layer=layer, n_row_out=[width, conv_all.shape[1]],
        scratch_shapes=[pltpu.VMEM((2, n_heads, head_dim, bsz), F32),
                        pltpu.VMEM((n_heads, bsz, head_dim), F32),
                        pltpu.VMEM((3, SMALL_W, bsz), F32),
                        pltpu.VMEM((n_heads, bsz, head_dim), F32)])


def _segment_ones(width, seg):
    i = jnp.arange(2 * width)[:, None] % width
    j = jnp.arange(width)[None, :]
    return ((i // seg) == (j // seg)).astype(BF16)


def _pad_lanes(row, offset):
    out = jnp.zeros((SMALL_W,), F32)
    return lax.dynamic_update_slice(out, row.astype(F32), (offset,)).reshape(1, SMALL_W)


def _row_tile(seq_len, cap):
    best = 8
    for tm in range(8, cap + 1, 8):
        if seq_len % tm == 0:
            best = tm
    return best


def kernel(x_prompt, x_sample, state_rwkv_wkv, state_rwkv_shift, state_ssm, state_ssm_conv,
           state_gdn, state_gdn_conv, meta_tokens, norm_w, w_in, rw_mu, rw_w0, rw_w2, rw_a0,
           rw_a2, rw_k_k, rw_k_a, rw_r_k, rw_gn_w, rw_gn_b, ssm_conv_w, ssm_conv_b,
           ssm_dt_bias, ssm_a_log, ssm_d, ssm_norm_w, gdn_conv_w, gdn_dt_bias, gdn_a_log,
           gdn_norm_w, w_rw_out, w_ssm_out, w_gdn_out, w_out, final_norm_w):
    depth = w_in.shape[0]
    bp, seq, d_model = x_prompt.shape
    bd = x_sample.shape[0]
    n_meta = meta_tokens.shape[0]
    rw_heads, rw_hd = rw_r_k.shape[1], rw_r_k.shape[2]
    rw_w = rw_heads * rw_hd
    rw_shift_w = rw_mu.shape[1]
    ssm_heads, ssm_hd, ssm_state = state_ssm.shape[2], state_ssm.shape[3], state_ssm.shape[4]
    ssm_w = ssm_heads * ssm_hd
    ssm_conv_ch = ssm_conv_w.shape[2]
    ssm_groups = (ssm_conv_ch - ssm_w) // (2 * ssm_state)
    gdn_heads, gdn_hd = state_gdn.shape[2], state_gdn.shape[3]
    gdn_w = gdn_heads * gdn_hd
    gdn_conv_ch = gdn_conv_w.shape[2]
    a_off, b_off = ssm_heads, ssm_heads + gdn_heads
    assert b_off + gdn_heads <= SMALL_W

    off = 0
    cols = {}
    for name, wdt in (("rw", rw_shift_w), ("rw_z", rw_w), ("ssm_z", ssm_w), ("xbc", ssm_conv_ch),
                      ("dt", ssm_heads), ("qkv", gdn_conv_ch), ("gdn_z", gdn_w),
                      ("ga", gdn_heads), ("gb", gdn_heads), ("gate", 3 * d_model)):
        cols[name] = (off, off + wdt)
        off += wdt
    assert off == w_in.shape[2]

    seg_names = ("rw", "rw_z", "ssm_z", "xbc", "qkv", "gdn_z", "gate", "dt", "ga", "gb")
    n_small = ssm_heads + 2 * gdn_heads
    w_all = _pack_weights(w_in, [cols[n] for n in seg_names], SMALL_W - n_small,
                          rows=_row_tile(d_model, 256))
    widths = [cols[n][1] - cols[n][0] for n in seg_names[:7]] + [SMALL_W]
    assert all(w % 128 == 0 for w in widths)
    w_r, w_s, w_g, w_o = (w.astype(BF16) for w in (w_rw_out, w_ssm_out, w_gdn_out, w_out))

    def layer_params(l):
        rw = dict(mu=rw_mu[l][None], w0=rw_w0[l][None], w2=rw_w2[l], a0=rw_a0[l][None],
                  a2=rw_a2[l], k_k=rw_k_k[l][None], k_a=rw_k_a[l][None],
                  r_k=rw_r_k[l].reshape(1, rw_w), gn_w=rw_gn_w[l][None], gn_b=rw_gn_b[l][None])
        dtb = _pad_lanes(ssm_dt_bias[l], 0)
        alog = _pad_lanes(ssm_a_log[l], 0)
        sm = dict(conv_w=ssm_conv_w[l], conv_b=ssm_conv_b[l][None], dtb_row=dtb,
                  dtb_col=dtb.reshape(SMALL_W, 1), alog_row=alog,
                  alog_col=alog.reshape(SMALL_W, 1),
                  d=jnp.repeat(ssm_d[l], ssm_hd)[None], norm_w=ssm_norm_w[l][None])
        gdtb = _pad_lanes(gdn_dt_bias[l], a_off)
        galog = _pad_lanes(gdn_a_log[l], a_off)
        gd = dict(conv_w=gdn_conv_w[l], dtb_row=gdtb, dtb_col=gdtb.reshape(SMALL_W, 1),
                  alog_row=galog, alog_col=galog.reshape(SMALL_W, 1),
                  norm_w=gdn_norm_w[l][None])
        return rw, sm, gd

    fw = final_norm_w[None]

    real_len = n_meta + seq
    seq_p = -(-real_len // CHUNK) * CHUNK
    n_pad = seq_p - real_len
    n_chunks = seq_p // CHUNK
    n_sub = max(d for d in range(1, MAX_CHUNKS_PER_STEP + 1) if n_chunks % d == 0)
    n_steps = n_chunks // n_sub
    meta = jnp.broadcast_to(meta_tokens.astype(F32)[None], (bp, n_meta, d_model))
    xp = jnp.concatenate([jnp.zeros((bp, n_pad, d_model), F32), meta, x_prompt], axis=1)
    xp = xp.reshape(bp * seq_p, d_model)
    tm_in = _row_tile(seq_p, 384)
    tm_out = _row_tile(seq_p, 384)

    p_states = [[] for _ in range(6)]
    y_p = None
    for l in range(depth):
        rw_p, sm_p, gd_p = layer_params(l)
        p_rw, rw_z, ssm_z, xbc, qkv, gdn_z, gate, small = _inproj(
            xp, norm_w[l][None], w_all, widths, layer=l, tm=tm_in, n_pad=n_pad, seq_len=seq_p)
        rw_o, wkv, shift = _rwkv_prompt(p_rw, rw_z, rw_p, batch=bp, n_steps=n_steps,
                                        n_sub=n_sub, n_heads=rw_heads, head_dim=rw_hd)
        sm_o, ssm, sconv = _ssd_prompt(xbc, small, ssm_z, sm_p, batch=bp, n_steps=n_steps,
                                       n_sub=n_sub, n_heads=ssm_heads, head_dim=ssm_hd,
                                       n_groups=ssm_groups, d_state=ssm_state, n_pad=n_pad)
        gd_o, gdn, gconv = _gdn_prompt(qkv, small, gdn_z, gd_p, batch=bp, n_steps=n_steps,
                                       n_sub=n_sub, n_heads=gdn_heads, head_dim=gdn_hd,
                                       a_off=a_off, b_off=b_off)
        last = l == depth - 1
        res = _outproj(rw_o, sm_o, gd_o, gate, xp, w_r[l], w_s[l], w_g[l], w_o[l], fw,
                       tm=tm_out, with_final=last)
        xp = res[0]
        if last:
            y_p = res[1]
        for acc, s in zip(p_states, (wkv, shift.reshape(bp, rw_shift_w), ssm, sconv, gdn, gconv)):
            acc.append(s)
    y_prompt = y_p.reshape(bp, seq_p, d_model)[:, n_pad + n_meta:]

    xs = x_sample.reshape(bd, d_model)
    shift_all = state_rwkv_shift.reshape(depth * bd, -1)
    sconv_all = state_ssm_conv.reshape(depth * bd, -1)
    gconv_all = state_gdn_conv.reshape(depth * bd, -1)
    wkv_all = jnp.transpose(state_rwkv_wkv, (0, 2, 3, 4, 1))
    wkv_new, ssm_new, gdn_new = (jnp.zeros(a.shape, F32) for a in (wkv_all, state_ssm, state_gdn))
    small_states = [[] for _ in range(3)]
    y_s = None
    for l in range(depth):
        rw_p, sm_p, gd_p = layer_params(l)
        p_rw, rw_z, ssm_z, xbc, qkv, gdn_z, gate, small = _inproj(
            xs, norm_w[l][None], w_all, widths, layer=l, tm=bd, n_pad=0, seq_len=bd)
        rw_o, wkv_new = _rwkv_decode(p_rw, rw_z, shift_all, wkv_all, wkv_new, rw_p, layer=l)
        sm_o, sconv, ssm_new = _ssd_decode(xbc, small, ssm_z, sconv_all, state_ssm, ssm_new, sm_p,
                                           layer=l, n_groups=ssm_groups)
        gd_o, gconv, gdn_new = _gdn_decode(qkv, small, gdn_z, gconv_all, state_gdn, gdn_new, gd_p,
                                           layer=l, a_off=a_off, b_off=b_off)
        last = l == depth - 1
        res = _outproj(rw_o, sm_o, gd_o, gate, xs, w_r[l], w_s[l], w_g[l], w_o[l], fw,
                       tm=bd, with_final=last)
        xs = res[0]
        if last:
            y_s = res[1]
        for acc, s in zip(small_states, (p_rw, sconv, gconv)):
            acc.append(s)
    y_sample = y_s.reshape(bd, 1, d_model)
    s_shift, s_sconv, s_gconv = (jnp.stack(a) for a in small_states)
    s_states = (jnp.transpose(wkv_new, (0, 4, 1, 2, 3)), s_shift, ssm_new, s_sconv.reshape(state_ssm_conv.shape),
                gdn_new, s_gconv.reshape(state_gdn_conv.shape))

    return (y_prompt, y_sample) + tuple(jnp.stack(a) for a in p_states) + s_states
```

```python
import functools
import math

import jax
import jax.numpy as jnp
from jax import lax
from jax.experimental import pallas as pl
from jax.experimental.pallas import tpu as pltpu

F32 = jnp.float32
BF16 = jnp.bfloat16
HIGHEST = lax.Precision.HIGHEST

NORM_EPS = 1e-6
RW_GN_EPS = 64e-5
SSM_NORM_EPS = 1e-5
GDN_NORM_EPS = 1e-6
L2_EPS = 1e-6
CONV_W = 4
CHUNK = 64
MAX_CHUNKS_PER_STEP = 3
MAX_SEQS_PER_STEP = 2
INV_BASE = 8
SEG_TILE = 256
DECODE_UNROLL = 8
SMALL_W = 128
CONV_HALO = 8
VMEM_LIMIT = 56 * 1024 * 1024


def _mm(a, b, exact=False):
    if exact:
        return jnp.dot(a, b, preferred_element_type=F32, precision=HIGHEST)
    return jnp.dot(a.astype(BF16), b.astype(BF16), preferred_element_type=F32)


def _mm_nt(a, b, exact=False):
    dn = (((1,), (1,)), ((), ()))
    if exact:
        return lax.dot_general(a, b, dn, preferred_element_type=F32, precision=HIGHEST)
    return lax.dot_general(a.astype(BF16), b.astype(BF16), dn, preferred_element_type=F32)


def _mm_tn(a, b, exact=False):
    dn = (((0,), (0,)), ((), ()))
    if exact:
        return lax.dot_general(a, b, dn, preferred_element_type=F32, precision=HIGHEST)
    return lax.dot_general(a.astype(BF16), b.astype(BF16), dn, preferred_element_type=F32)


def _sigmoid(x):
    return 1.0 / (1.0 + jnp.exp(-x))


def _silu(x):
    return x * _sigmoid(x)


def _softplus(x):
    return jnp.maximum(x, 0.0) + jnp.log(1.0 + jnp.exp(-jnp.abs(x)))


def _iota2(shape, dim):
    return lax.broadcasted_iota(jnp.int32, shape, dim)


def _seg_sum(x, seg_ones, split=False):
    w = seg_ones.shape[1]
    parts = []
    for i in range(x.shape[1] // w):
        xi = x[:, i * w:(i + 1) * w]
        x_hi = xi.astype(BF16)
        if split:
            x_lo = (xi - x_hi.astype(F32)).astype(BF16)
            parts.append(jnp.dot(jnp.concatenate([x_hi, x_lo], axis=1), seg_ones,
                                 preferred_element_type=F32))
        else:
            parts.append(jnp.dot(x_hi, seg_ones[:w], preferred_element_type=F32))
    return jnp.concatenate(parts, axis=1)


def _block_diag_pair(p):
    c = p.shape[0]
    lane = _iota2(p.shape, 1)
    pb = p.astype(BF16)
    zero = jnp.zeros_like(pb)
    return jnp.concatenate([jnp.where(lane < c, pb, zero), jnp.where(lane >= c, pb, zero)], axis=0)


def _mm_pair(x, p):
    return jnp.dot(x.astype(BF16), _block_diag_pair(p), preferred_element_type=F32)


def _unit_lower_inverse(l_mats):
    c = l_mats[0].shape[0]
    pairs = [jnp.concatenate(l_mats[i:i + 2], axis=1) for i in range(0, len(l_mats), 2)]
    ri = _iota2((c, 2 * c), 0)
    ci = _iota2((c, 2 * c), 1) & (c - 1)
    base_shift = int(math.log2(INV_BASE))
    same_base = (ri >> base_shift) == (ci >> base_shift)
    es = [jnp.where(same_base, l, 0.0) for l in pairs]
    ps = es
    n_terms = 1
    while 2 * n_terms < INV_BASE:
        ps = [_mm_pair(p, p) for p in ps]
        eps = [_mm_pair(e, p) for e, p in zip(es, ps)]
        es = [e + p + ep for e, p, ep in zip(es, ps, eps)]
        n_terms *= 2
    shift = base_shift
    while (1 << shift) < c:
        same_outer = (ri >> (shift + 1)) == (ci >> (shift + 1))
        other_inner = (ri >> shift) != (ci >> shift)
        l_off = [jnp.where(same_outer, jnp.where(other_inner, l, 0.0), 0.0) for l in pairs]
        ts = [lo + _mm_pair(lo, e) for lo, e in zip(l_off, es)]
        ets = [_mm_pair(e, t) for e, t in zip(es, ts)]
        es = [e + t + et for e, t, et in zip(es, ts, ets)]
        shift += 1
    out = []
    for e in es:
        out += [e[:, :c], e[:, c:]]
    return out


def _chunk_masks(c):
    ri = _iota2((c, c), 0)
    ci = _iota2((c, c), 1)
    return ri >= ci, ri > ci


def _block_tril(rows, c):
    shift = int(math.log2(c))
    ri = _iota2((rows, rows), 0)
    ci = _iota2((rows, rows), 1)
    return jnp.where((ri >> shift) == (ci >> shift),
                     jnp.where(ri >= ci, 1.0, 0.0), 0.0).astype(F32)


def _conv_chunk(ext_ref, u, w_ref, first_chunk):
    c = u.shape[0]

    @pl.when(first_chunk)
    def _():
        ext_ref[0:CONV_HALO, :] = jnp.zeros((CONV_HALO, u.shape[1]), F32)

    ext_ref[CONV_HALO:CONV_HALO + c, :] = u
    out = u * w_ref[CONV_W - 1:CONV_W, :]
    for i in range(CONV_W - 1):
        off = CONV_HALO - (CONV_W - 1) + i
        out = out + ext_ref[off:off + c, :] * w_ref[i:i + 1, :]
    tail = ext_ref[c:c + CONV_HALO, :]
    ext_ref[0:CONV_HALO, :] = tail
    return out, tail[CONV_HALO - (CONV_W - 1):, :]


def _inproj_body(x_ref, nw_ref, w_ref, *o_refs, n_pad, tiles_per_seq):
    x = x_ref[...]
    h = x * lax.rsqrt(jnp.mean(x * x, axis=-1, keepdims=True) + NORM_EPS) * nw_ref[...]
    if n_pad:
        first = lax.rem(pl.program_id(0), tiles_per_seq) == 0
        row = _iota2((x.shape[0], 1), 0)
        h = jnp.where(jnp.logical_and(first, row < n_pad), 0.0, h)
    hb = h.astype(BF16)
    off = 0
    for o_ref in o_refs:
        width = o_ref.shape[1]
        o_ref[...] = jnp.dot(hb, w_ref[:, off:off + width], preferred_element_type=F32)
        off += width


def _inproj(x, norm_w, w_all, widths, *, layer, tm, n_pad, seq_len):
    t, d = x.shape
    assert w_all.shape[2] == sum(widths)
    in_specs = [pl.BlockSpec((tm, d), lambda i: (i, 0)),
                pl.BlockSpec((1, d), lambda i: (0, 0)),
                pl.BlockSpec((None,) + w_all.shape[1:], lambda i: (layer, 0, 0),
                             pipeline_mode=pl.Buffered(1))]
    out_specs = [pl.BlockSpec((tm, w), lambda i: (i, 0)) for w in widths]
    out_shape = [jax.ShapeDtypeStruct((t, w), F32) for w in widths]
    body = functools.partial(_inproj_body, n_pad=n_pad,
                             tiles_per_seq=(seq_len // tm) if n_pad else 1)
    return pl.pallas_call(
        body, grid=(t // tm,), in_specs=in_specs, out_specs=out_specs, out_shape=out_shape,
        compiler_params=pltpu.CompilerParams(dimension_semantics=("arbitrary",),
                                             vmem_limit_bytes=VMEM_LIMIT),
        name="inproj",
    )(x, norm_w, w_all)


def _pack_weights_body(w_ref, o_ref, *, segments, pad):
    off = 0
    tail = []
    for lo, hi in segments:
        if (hi - lo) % 128 == 0 and not tail:
            o_ref[:, off:off + hi - lo] = w_ref[lo:hi, :].T.astype(o_ref.dtype)
            off += hi - lo
        else:
            tail.append(w_ref[lo:hi, :])
    if tail or pad:
        if pad:
            tail.append(jnp.zeros((pad, w_ref.shape[1]), w_ref.dtype))
        o_ref[:, off:] = jnp.concatenate(tail, axis=0).T.astype(o_ref.dtype)


def _pack_weights(w_t, segments, pad, *, kb):
    depth, n, d = w_t.shape
    total = sum(hi - lo for lo, hi in segments) + pad
    return pl.pallas_call(
        functools.partial(_pack_weights_body, segments=tuple(segments), pad=pad),
        grid=(depth, d // kb),
        in_specs=[pl.BlockSpec((None, n, kb), lambda l, i: (l, 0, i))],
        out_specs=pl.BlockSpec((None, kb, total), lambda l, i: (l, i, 0)),
        out_shape=jax.ShapeDtypeStruct((depth, d, total), BF16),
        compiler_params=pltpu.CompilerParams(dimension_semantics=("arbitrary", "arbitrary"),
                                             vmem_limit_bytes=VMEM_LIMIT),
        name="pack_weights",
    )(w_t)


def _outproj_body(rw_ref, sm_ref, gd_ref, gate_ref, x_ref, wr_ref, ws_ref, wg_ref, wo_ref,
                  fw_ref, o_ref, *maybe_y_ref, d_model):
    merged = None
    for b, (y_ref, w_ref) in enumerate(((rw_ref, wr_ref), (sm_ref, ws_ref), (gd_ref, wg_ref))):
        g = _sigmoid(gate_ref[:, b * d_model:(b + 1) * d_model])
        term = g * jnp.dot(y_ref[...].astype(BF16), w_ref[...], preferred_element_type=F32)
        merged = term if merged is None else merged + term
    x_new = x_ref[...] + jnp.dot(merged.astype(BF16), wo_ref[...], preferred_element_type=F32)
    o_ref[...] = x_new
    if maybe_y_ref:
        ms = jnp.mean(x_new * x_new, axis=-1, keepdims=True)
        maybe_y_ref[0][...] = x_new * lax.rsqrt(ms + NORM_EPS) * fw_ref[...]


def _outproj(rw, sm, gd, gate, x, w_r, w_s, w_g, w_o, final_w, *, tm, with_final):
    t, d = x.shape
    grid = (t // tm,)
    row = lambda i: (i, 0)
    fixed = lambda i: (0, 0)
    in_specs = [pl.BlockSpec((tm, rw.shape[1]), row), pl.BlockSpec((tm, sm.shape[1]), row),
                pl.BlockSpec((tm, gd.shape[1]), row), pl.BlockSpec((tm, gate.shape[1]), row),
                pl.BlockSpec((tm, d), row),
                pl.BlockSpec(w_r.shape, fixed), pl.BlockSpec(w_s.shape, fixed),
                pl.BlockSpec(w_g.shape, fixed), pl.BlockSpec(w_o.shape, fixed),
                pl.BlockSpec((1, d), fixed)]
    n_out = 2 if with_final else 1
    out_specs = [pl.BlockSpec((tm, d), row)] * n_out
    out_shape = [jax.ShapeDtypeStruct((t, d), F32)] * n_out
    return pl.pallas_call(
        functools.partial(_outproj_body, d_model=d),
        grid=grid, in_specs=in_specs, out_specs=out_specs, out_shape=out_shape,
        compiler_params=pltpu.CompilerParams(dimension_semantics=("arbitrary",),
                                             vmem_limit_bytes=VMEM_LIMIT),
        name="outproj",
    )(rw, sm, gd, gate, x, w_r, w_s, w_g, w_o, final_w)


def _rwkv_pre(p, shifted, mu, w0, w2, a0, a2, k_k, k_a, width, lora):
    u = p + (shifted - p) * mu
    r = u[:, :width]
    k = u[:, width:2 * width]
    v = u[:, 2 * width:3 * width]
    wd = u[:, 3 * width:3 * width + lora]
    ad = u[:, 3 * width + lora:]
    w = -_softplus(-(w0 + _mm(jnp.tanh(wd), w2))) - 0.5
    log_decay = -jnp.exp(w)
    a = _sigmoid(a0 + _mm(ad, a2))
    kk_raw = k * k_k
    k_mod = k * (1.0 + (a - 1.0) * k_a)
    return r, k_mod, v, log_decay, a, kk_raw


def _l2norm(x):
    return x * lax.rsqrt(jnp.sum(x * x, axis=-1, keepdims=True) + L2_EPS)


def _rwkv_post_head(y, r, k_mod, v, z, r_k, gn_w, gn_b):
    mean = jnp.mean(y, axis=-1, keepdims=True)
    var = jnp.mean(jnp.square(y - mean), axis=-1, keepdims=True)
    yn = (y - mean) * lax.rsqrt(var + RW_GN_EPS) * gn_w + gn_b
    bonus = jnp.sum(r * k_mod * r_k, axis=-1, keepdims=True) * v
    return (yn + bonus) * _silu(z)


def _rwkv_prompt_body(p_ref, z_ref, ones_ref, mu_ref, w0_ref, w2_ref, a0_ref, a2_ref, kk_ref,
                      ka_ref, rk_ref, gnw_ref, gnb_ref, o_ref, wkv_ref, shift_ref, prev_scr,
                      *, n_heads, head_dim, n_sub):
    width = n_heads * head_dim
    lora = w2_ref.shape[0]
    n_seq, rows, shift_w = p_ref.shape
    c = rows // n_sub
    p = p_ref[...].reshape(n_seq * rows, shift_w)

    @pl.when(pl.program_id(1) == 0)
    def _():
        wkv_ref[...] = jnp.zeros(wkv_ref.shape, F32)
        prev_scr[...] = jnp.zeros(prev_scr.shape, F32)

    row = _iota2((n_seq * rows, 1), 0)
    shifted = pltpu.roll(p, 1, 0)
    for s in range(n_seq):
        shifted = jnp.where(row == s * rows, prev_scr[s], shifted)
        last_row = p[(s + 1) * rows - 1:(s + 1) * rows, :]
        prev_scr[s] = last_row
        shift_ref[s] = last_row

    r, k_mod, v, log_decay, a, kk_raw = _rwkv_pre(
        p, shifted, mu_ref[...], w0_ref[...], w2_ref[...], a0_ref[...], a2_ref[...],
        kk_ref[...], ka_ref[...], width, lora)
    tril = _block_tril(rows, c)
    cum = jnp.concatenate([_mm(tril, log_decay[s * rows:(s + 1) * rows], exact=True)
                           for s in range(n_seq)], axis=0)
    e_pos = jnp.exp(cum)
    e_neg = jnp.exp(-cum)
    e_prev = jnp.exp(cum - log_decay)
    z = z_ref[...].reshape(n_seq * rows, width)
    incl, strict = _chunk_masks(c)

    hs = range(n_heads)
    sls = [slice(h * head_dim, (h + 1) * head_dim) for h in hs]
    n_chunks = n_seq * n_sub
    units = [(j, h) for j in range(n_chunks) for h in hs]
    rsl = [slice(j * c, (j + 1) * c) for j in range(n_chunks)]
    seg_ones = ones_ref[...]
    kk = kk_raw * lax.rsqrt(_seg_sum(kk_raw * kk_raw, seg_ones) + L2_EPS)
    a_all = -kk * e_prev
    b_all = kk * a * e_neg
    k_all = k_mod * e_neg
    r_all = r * e_pos
    a_t = [a_all[:, sl] for sl in sls]
    b_t = [b_all[:, sl] for sl in sls]
    k_t = [k_all[:, sl] for sl in sls]
    r_t = [r_all[:, sl] for sl in sls]
    v_h = [v[:, sl] for sl in sls]

    sc = {u: _mm_nt(jnp.concatenate([a_t[u[1]][rsl[u[0]]], r_t[u[1]][rsl[u[0]]]], axis=0),
                    jnp.concatenate([b_t[u[1]][rsl[u[0]]], k_t[u[1]][rsl[u[0]]]], axis=0))
          for u in units}
    l_ab = {u: jnp.where(strict, sc[u][:c, :c], 0.0) for u in units}
    l_ak = {u: jnp.where(strict, sc[u][:c, c:], 0.0) for u in units}
    m_rb = {u: jnp.where(incl, sc[u][c:, :c], 0.0) for u in units}
    m_rk = {u: jnp.where(incl, sc[u][c:, c:], 0.0) for u in units}
    lv = {u: _mm(jnp.concatenate([l_ak[u], m_rk[u]], axis=0), v_h[u[1]][rsl[u[0]]]) for u in units}
    inv = dict(zip(units, _unit_lower_inverse([l_ab[u] for u in units])))
    rhs = {u: jnp.concatenate([a_t[u[1]][rsl[u[0]]], lv[u][:c]], axis=1) for u in units}
    wu = {u: rhs[u] + _mm(inv[u], rhs[u]) for u in units}
    d_end = {(j, h): e_pos[(j + 1) * c - 1:(j + 1) * c, sls[h]] for j, h in units}
    qy = {u: _mm(m_rb[u], wu[u]) for u in units}
    mn = {u: _mm_tn(wu[u], b_t[u[1]][rsl[u[0]]] * d_end[u]) for u in units}
    n2 = {u: _mm_tn(v_h[u[1]][rsl[u[0]]], k_t[u[1]][rsl[u[0]]] * d_end[u]) for u in units}

    sh = [(s, h) for s in range(n_seq) for h in hs]
    states = {(s, h): wkv_ref[s, h] for s, h in sh}
    ys = {k: [] for k in sh}
    for jj in range(n_sub):
        ju = {(s, h): (s * n_sub + jj, h) for s, h in sh}
        y_dot = {k: _mm_nt(r_t[k[1]][rsl[ju[k][0]]] + qy[ju[k]][:, :c], states[k]) for k in sh}
        s_dot = {k: _mm(states[k], mn[ju[k]][:c]) for k in sh}
        for k in sh:
            u = ju[k]
            ys[k].append(y_dot[k] + qy[u][:, c:] + lv[u][c:])
            states[k] = states[k] * d_end[u] + s_dot[k] + (mn[u][c:] + n2[u])
    for s, h in sh:
        wkv_ref[s, h] = states[(s, h)]
    y = jnp.concatenate([jnp.concatenate([t for s in range(n_seq) for t in ys[(s, h)]], axis=0)
                         for h in hs], axis=-1)
    inv_n = 1.0 / head_dim
    dev = y - _seg_sum(y, seg_ones) * inv_n
    var = _seg_sum(dev * dev, seg_ones) * inv_n
    yn = dev * lax.rsqrt(var + RW_GN_EPS) * gnw_ref[...] + gnb_ref[...]
    bonus = _seg_sum(r * k_mod * rk_ref[...], seg_ones, split=True) * v
    o_ref[...] = ((yn + bonus) * _silu(z)).reshape(n_seq, rows, width)


def _seq_specs(n_seq, rows, n_steps, arrays):
    views = [a.reshape(-1, n_steps * rows, a.shape[1]) for a in arrays]
    specs = [pl.BlockSpec((n_seq, rows, a.shape[1]), lambda b, c: (b, c, 0)) for a in arrays]
    return views, specs


def _rwkv_prompt(p_rw, z, prm, *, batch, n_steps, n_sub, n_seq, n_heads, head_dim):
    t, shift_w = p_rw.shape
    width = n_heads * head_dim
    rows = n_sub * CHUNK
    fixed = lambda b, c: (0, 0)
    names = ("mu", "w0", "w2", "a0", "a2", "k_k", "k_a", "r_k", "gn_w", "gn_b")
    params = [_segment_ones(SEG_TILE, head_dim)] + [prm[n] for n in names]
    views, in_specs = _seq_specs(n_seq, rows, n_steps, [p_rw, z])
    in_specs += [pl.BlockSpec(a.shape, fixed) for a in params]
    out_specs = [pl.BlockSpec((n_seq, rows, width), lambda b, c: (b, c, 0)),
                 pl.BlockSpec((n_seq, n_heads, head_dim, head_dim), lambda b, c: (b, 0, 0, 0)),
                 pl.BlockSpec((n_seq, 1, shift_w), lambda b, c: (b, 0, 0))]
    out_shape = [jax.ShapeDtypeStruct((batch, n_steps * rows, width), F32),
                 jax.ShapeDtypeStruct((batch, n_heads, head_dim, head_dim), F32),
                 jax.ShapeDtypeStruct((batch, 1, shift_w), F32)]
    o, wkv, shift = pl.pallas_call(
        functools.partial(_rwkv_prompt_body, n_heads=n_heads, head_dim=head_dim, n_sub=n_sub),
        grid=(batch // n_seq, n_steps), in_specs=in_specs, out_specs=out_specs,
        out_shape=out_shape,
        scratch_shapes=[pltpu.VMEM((n_seq, 1, shift_w), F32)],
        compiler_params=pltpu.CompilerParams(dimension_semantics=("arbitrary", "arbitrary"),
                                             vmem_limit_bytes=VMEM_LIMIT),
        name="rwkv_prompt",
    )(*views, *params)
    return o.reshape(t, width), wkv, shift


def _ssm_gated_norm(y, z, norm_w, n_groups):
    g = y * _silu(z)
    gw = g.shape[1] // n_groups
    parts = []
    for i in range(n_groups):
        gi = g[:, i * gw:(i + 1) * gw]
        parts.append(gi * lax.rsqrt(jnp.mean(gi * gi, axis=-1, keepdims=True) + SSM_NORM_EPS))
    return jnp.concatenate(parts, axis=-1) * norm_w


def _ssd_prompt_body(xbc_ref, small_ref, z_ref, cw_ref, cb_ref, dtb_row_ref, dtb_col_ref,
                     alog_row_ref, alog_col_ref, d_ref, nw_ref, o_ref, h_ref, conv_ref, ext_scr,
                     *, n_heads, head_dim, n_groups, d_state, n_pad, n_sub):
    first = pl.program_id(1) == 0
    width = n_heads * head_dim
    hpg = n_heads // n_groups
    gw = hpg * head_dim
    n_seq, rows, _ = xbc_ref.shape
    c = rows // n_sub
    n_chunks = n_seq * n_sub
    rsl = [slice(j * c, (j + 1) * c) for j in range(n_chunks)]

    @pl.when(first)
    def _():
        h_ref[...] = jnp.zeros(h_ref.shape, F32)

    convs = []
    for s in range(n_seq):
        conv, tail = _conv_chunk(ext_scr.at[s], xbc_ref[s], cw_ref, first)
        conv_ref[s] = tail
        convs.append(conv)
    xbc = _silu(jnp.concatenate(convs, axis=0) + cb_ref[...])
    xs = xbc[:, :width]
    b_all = xbc[:, width:width + n_groups * d_state]
    c_all = xbc[:, width + n_groups * d_state:]

    small = small_ref[...].reshape(n_seq * rows, SMALL_W)
    lane = _iota2((1, SMALL_W), 1)
    sub = _iota2((SMALL_W, 1), 0)
    dt = _softplus(small + dtb_row_ref[...])
    dt_t = [_softplus(small[sl].T + dtb_col_ref[...]) for sl in rsl]
    if n_pad:
        assert n_pad <= c
        not_first = jnp.logical_not(first)
        row = _iota2((n_seq * rows, 1), 0)
        for s in range(n_seq):
            in_pad = jnp.logical_and(row >= s * rows, row < s * rows + n_pad)
            dt = jnp.where(jnp.logical_and(first, in_pad), 0.0, dt)
        for s in range(n_seq):
            j0 = s * n_sub
            dt_t[j0] = jnp.where(jnp.logical_or(not_first, _iota2((1, c), 1) >= n_pad),
                                 dt_t[j0], 0.0)
    a_row = jnp.where(lane < n_heads, -jnp.exp(alog_row_ref[...]), 0.0)
    a_col = jnp.where(sub < n_heads, -jnp.exp(alog_col_ref[...]), 0.0)
    incl, _ = _chunk_masks(c)
    triu = jnp.where(_iota2((c, c), 0) <= _iota2((c, c), 1), 1.0, 0.0).astype(F32)
    tril = _block_tril(rows, c)
    da = dt * a_row
    acs = jnp.concatenate([_mm(tril, da[s * rows:(s + 1) * rows], exact=True)
                           for s in range(n_seq)], axis=0)
    acs_t = [_mm(t * a_col, triu, exact=True) for t in dt_t]
    expand = jnp.where((_iota2((SMALL_W, width), 1) // head_dim) == _iota2((SMALL_W, width), 0),
                       1.0, 0.0).astype(F32)
    wide = _mm(jnp.concatenate([acs, dt], axis=0), expand, exact=True)
    acs_w = wide[:n_seq * rows]
    xdt = xs * wide[n_seq * rows:]
    e_acs = jnp.exp(acs_w)

    hs = range(n_heads)
    gs = range(n_groups)
    js = range(n_chunks)
    b_g = {(j, g): b_all[rsl[j], g * d_state:(g + 1) * d_state] for j in js for g in gs}
    c_g = {(j, g): c_all[rsl[j], g * d_state:(g + 1) * d_state] for j in js for g in gs}
    a_last = {j: acs_w[(j + 1) * c - 1:(j + 1) * c] for j in js}
    xdt_end = {j: xdt[rsl[j]] * jnp.exp(a_last[j] - acs_w[rsl[j]]) for j in js}

    cb = {jg: _mm_nt(c_g[jg], b_g[jg]) for jg in b_g}
    seg = {(j, h): jnp.exp(jnp.where(incl, acs[rsl[j], h:h + 1] - acs_t[j][h:h + 1, :], -jnp.inf))
           for j in js for h in hs}
    y_diag = {(j, h): _mm(cb[(j, h // hpg)] * seg[(j, h)],
                          xdt[rsl[j], h * head_dim:(h + 1) * head_dim])
              for j in js for h in hs}
    st = {(j, g): _mm_tn(xdt_end[j][:, g * gw:(g + 1) * gw], b_g[(j, g)]) for j in js for g in gs}

    sg = [(s, g) for s in range(n_seq) for g in gs]
    states = {(s, g): h_ref[s, g * hpg:(g + 1) * hpg].reshape(gw, d_state) for s, g in sg}
    ys = {s: [] for s in range(n_seq)}
    for jj in range(n_sub):
        y_off = {(s, g): _mm_nt(c_g[(s * n_sub + jj, g)], states[(s, g)]) for s, g in sg}
        for s in range(n_seq):
            j = s * n_sub + jj
            ys[s].append(jnp.concatenate([y_off[(s, g)] for g in gs], axis=-1) * e_acs[rsl[j]]
                         + jnp.concatenate([y_diag[(j, h)] for h in hs], axis=-1))
            for g in gs:
                scaled = [states[(s, g)][i * head_dim:(i + 1) * head_dim]
                          * jnp.exp(acs[(j + 1) * c - 1:(j + 1) * c, g * hpg + i:g * hpg + i + 1])
                          for i in range(hpg)]
                states[(s, g)] = jnp.concatenate(scaled, axis=0) + st[(j, g)]
    for s, g in sg:
        h_ref[s, g * hpg:(g + 1) * hpg] = states[(s, g)].reshape(hpg, head_dim, d_state)
    y_all = jnp.concatenate([t for s in range(n_seq) for t in ys[s]], axis=0) + xs * d_ref[...]
    z = z_ref[...].reshape(n_seq * rows, width)
    o_ref[...] = _ssm_gated_norm(y_all, z, nw_ref[...], n_groups).reshape(n_seq, rows, width)


def _ssd_prompt(xbc, small, z, prm, *, batch, n_steps, n_sub, n_seq, n_heads, head_dim, n_groups,
                d_state, n_pad):
    t, conv_ch = xbc.shape
    width = n_heads * head_dim
    rows = n_sub * CHUNK
    fixed = lambda b, c: (0, 0)
    names = ("conv_w", "conv_b", "dtb_row", "dtb_col", "alog_row", "alog_col", "d", "norm_w")
    params = [prm[n] for n in names]
    views, in_specs = _seq_specs(n_seq, rows, n_steps, [xbc, small, z])
    in_specs += [pl.BlockSpec(a.shape, fixed) for a in params]
    out_specs = [pl.BlockSpec((n_seq, rows, width), lambda b, c: (b, c, 0)),
                 pl.BlockSpec((n_seq, n_heads, head_dim, d_state), lambda b, c: (b, 0, 0, 0)),
                 pl.BlockSpec((n_seq, CONV_W - 1, conv_ch), lambda b, c: (b, 0, 0))]
    out_shape = [jax.ShapeDtypeStruct((batch, n_steps * rows, width), F32),
                 jax.ShapeDtypeStruct((batch, n_heads, head_dim, d_state), F32),
                 jax.ShapeDtypeStruct((batch, CONV_W - 1, conv_ch), F32)]
    o, state, conv = pl.pallas_call(
        functools.partial(_ssd_prompt_body, n_heads=n_heads, head_dim=head_dim,
                          n_groups=n_groups, d_state=d_state, n_pad=n_pad, n_sub=n_sub),
        grid=(batch // n_seq, n_steps), in_specs=in_specs, out_specs=out_specs,
        out_shape=out_shape,
        scratch_shapes=[pltpu.VMEM((n_seq, rows + CONV_HALO, conv_ch), F32)],
        compiler_params=pltpu.CompilerParams(dimension_semantics=("arbitrary", "arbitrary"),
                                             vmem_limit_bytes=VMEM_LIMIT),
        name="ssd_prompt",
    )(*views, *params)
    return o.reshape(t, width), state, conv


def _gdn_post_head(o, z, norm_w):
    return o * lax.rsqrt(jnp.mean(o * o, axis=-1, keepdims=True) + GDN_NORM_EPS) * norm_w * _silu(z)


def _gdn_prompt_body(qkv_ref, small_ref, z_ref, cw_ref, dtb_row_ref, dtb_col_ref, alog_row_ref,
                     alog_col_ref, nw_ref, o_ref, s_ref, conv_ref, ext_scr,
                     *, n_heads, head_dim, a_off, b_off, n_sub):
    first = pl.program_id(1) == 0
    width = n_heads * head_dim
    n_seq, rows, _ = qkv_ref.shape
    c = rows // n_sub
    n_chunks = n_seq * n_sub

    @pl.when(first)
    def _():
        s_ref[...] = jnp.zeros(s_ref.shape, F32)

    convs = []
    for s in range(n_seq):
        conv, tail = _conv_chunk(ext_scr.at[s], qkv_ref[s], cw_ref, first)
        conv_ref[s] = tail
        convs.append(conv)
    qkv = _silu(jnp.concatenate(convs, axis=0))
    small = small_ref[...].reshape(n_seq * rows, SMALL_W)
    rsl = [slice(j * c, (j + 1) * c) for j in range(n_chunks)]
    small_t = [small[sl].T for sl in rsl]
    lane = _iota2((1, SMALL_W), 1)
    sub = _iota2((SMALL_W, 1), 0)
    g = jnp.where(jnp.logical_and(lane >= a_off, lane < a_off + n_heads),
                  -jnp.exp(alog_row_ref[...]) * _softplus(small + dtb_row_ref[...]), 0.0)
    g_t = [jnp.where(jnp.logical_and(sub >= a_off, sub < a_off + n_heads),
                     -jnp.exp(alog_col_ref[...]) * _softplus(st + dtb_col_ref[...]), 0.0)
           for st in small_t]
    beta = _sigmoid(small)
    incl, strict = _chunk_masks(c)
    triu = jnp.where(_iota2((c, c), 0) <= _iota2((c, c), 1), 1.0, 0.0).astype(F32)
    tril = _block_tril(rows, c)
    gcs = jnp.concatenate([_mm(tril, g[s * rows:(s + 1) * rows], exact=True)
                           for s in range(n_seq)], axis=0)
    gcs_t = [_mm(gt, triu, exact=True) for gt in g_t]
    z = z_ref[...].reshape(n_seq * rows, width)
    scale = head_dim ** -0.5

    hs = range(n_heads)
    sls = [slice(h * head_dim, (h + 1) * head_dim) for h in hs]
    units = [(j, h) for j in range(n_chunks) for h in hs]
    q_h = [_l2norm(qkv[:, sl]) * scale for sl in sls]
    k_h = [_l2norm(qkv[:, width + h * head_dim:width + (h + 1) * head_dim]) for h in hs]
    v_h = [qkv[:, 2 * width + h * head_dim:2 * width + (h + 1) * head_dim] for h in hs]
    g_c = {(j, h): gcs[rsl[j], a_off + h:a_off + h + 1] for j, h in units}
    g_r = {(j, h): gcs_t[j][a_off + h:a_off + h + 1, :] for j, h in units}
    g_last = {(j, h): gcs[(j + 1) * c - 1:(j + 1) * c, a_off + h:a_off + h + 1] for j, h in units}
    b_c = {(j, h): beta[rsl[j], b_off + h:b_off + h + 1] for j, h in units}
    q = {(j, h): q_h[h][rsl[j]] for j, h in units}
    k = {(j, h): k_h[h][rsl[j]] for j, h in units}
    v = {(j, h): v_h[h][rsl[j]] for j, h in units}

    decay = {u: jnp.exp(jnp.where(incl, g_c[u] - g_r[u], -jnp.inf)) for u in units}
    kb = {u: k[u] * b_c[u] for u in units}
    e_c = {u: jnp.exp(g_c[u]) for u in units}
    sc = {u: _mm_nt(jnp.concatenate([kb[u], q[u]], axis=0), k[u]) for u in units}
    a_mat = {u: jnp.where(strict, sc[u][:c] * decay[u], 0.0) for u in units}
    qk = {u: jnp.where(incl, sc[u][c:] * decay[u], 0.0) for u in units}
    inv = dict(zip(units, _unit_lower_inverse([-a_mat[u] for u in units])))
    rhs = {u: jnp.concatenate([v[u] * b_c[u], kb[u] * e_c[u]], axis=1) for u in units}
    uw = {u: rhs[u] + _mm(inv[u], rhs[u]) for u in units}
    qo = {u: _mm(qk[u], uw[u]) for u in units}
    mn = {u: _mm_tn(k[u] * jnp.exp(g_last[u] - g_c[u]), uw[u]) for u in units}

    sh = [(s, h) for s in range(n_seq) for h in hs]
    states = {(s, h): s_ref[s, h] for s, h in sh}
    os_ = {k_: [] for k_ in sh}
    for jj in range(n_sub):
        ju = {(s, h): (s * n_sub + jj, h) for s, h in sh}
        o_dot = {k_: _mm(q[ju[k_]] * e_c[ju[k_]] - qo[ju[k_]][:, head_dim:], states[k_])
                 for k_ in sh}
        s_dot = {k_: _mm(mn[ju[k_]][:, head_dim:], states[k_]) for k_ in sh}
        for k_ in sh:
            u = ju[k_]
            os_[k_].append(o_dot[k_] + qo[u][:, :head_dim])
            states[k_] = states[k_] * jnp.exp(g_last[u]) - s_dot[k_] + mn[u][:, :head_dim]
    for s, h in sh:
        s_ref[s, h] = states[(s, h)]
    outs = [_gdn_post_head(jnp.concatenate([t for s in range(n_seq) for t in os_[(s, h)]], axis=0),
                           z[:, sls[h]], nw_ref[...]) for h in hs]
    o_ref[...] = jnp.concatenate(outs, axis=-1).reshape(n_seq, rows, width)


def _gdn_prompt(qkv, small, z, prm, *, batch, n_steps, n_sub, n_seq, n_heads, head_dim, a_off,
                b_off):
    t, conv_ch = qkv.shape
    width = n_heads * head_dim
    rows = n_sub * CHUNK
    fixed = lambda b, c: (0, 0)
    names = ("conv_w", "dtb_row", "dtb_col", "alog_row", "alog_col", "norm_w")
    params = [prm[n] for n in names]
    views, in_specs = _seq_specs(n_seq, rows, n_steps, [qkv, small, z])
    in_specs += [pl.BlockSpec(a.shape, fixed) for a in params]
    out_specs = [pl.BlockSpec((n_seq, rows, width), lambda b, c: (b, c, 0)),
                 pl.BlockSpec((n_seq, n_heads, head_dim, head_dim), lambda b, c: (b, 0, 0, 0)),
                 pl.BlockSpec((n_seq, CONV_W - 1, conv_ch), lambda b, c: (b, 0, 0))]
    out_shape = [jax.ShapeDtypeStruct((batch, n_steps * rows, width), F32),
                 jax.ShapeDtypeStruct((batch, n_heads, head_dim, head_dim), F32),
                 jax.ShapeDtypeStruct((batch, CONV_W - 1, conv_ch), F32)]
    o, state, conv = pl.pallas_call(
        functools.partial(_gdn_prompt_body, n_heads=n_heads, head_dim=head_dim,
                          a_off=a_off, b_off=b_off, n_sub=n_sub),
        grid=(batch // n_seq, n_steps), in_specs=in_specs, out_specs=out_specs,
        out_shape=out_shape,
        scratch_shapes=[pltpu.VMEM((n_seq, rows + CONV_HALO, conv_ch), F32)],
        compiler_params=pltpu.CompilerParams(dimension_semantics=("arbitrary", "arbitrary"),
                                             vmem_limit_bytes=VMEM_LIMIT),
        name="gdn_prompt",
    )(*views, *params)
    return o.reshape(t, width), state, conv


def _conv_step(u, conv_ref, convo_ref, w_ref):
    out = u * w_ref[CONV_W - 1:CONV_W, :]
    for i in range(CONV_W - 1):
        out = out + conv_ref[i] * w_ref[i:i + 1, :]
    for i in range(CONV_W - 2):
        convo_ref[i] = conv_ref[i + 1]
    convo_ref[CONV_W - 2] = u
    return out


def _lane_pick(x, lane, idx):
    return jnp.sum(jnp.where(lane == idx, x, 0.0), axis=1, keepdims=True)


def _rwkv_decode_body(p_ref, z_ref, prev_ref, s_ref, mu_ref, w0_ref, w2_ref, a0_ref, a2_ref,
                      kk_ref, ka_ref, rk_ref, gnw_ref, gnb_ref, o_ref, so_ref,
                      t_scr, y_scr, row_scr, *, n_heads, head_dim):
    h = pl.program_id(0)
    width = n_heads * head_dim
    lora = w2_ref.shape[0]
    bsz = p_ref.shape[0]

    @pl.when(h == 0)
    def _():
        r, k_mod, v, log_decay, a, kk_raw = _rwkv_pre(
            p_ref[...], prev_ref[...], mu_ref[...], w0_ref[...], w2_ref[...], a0_ref[...],
            a2_ref[...], kk_ref[...], ka_ref[...], width, lora)
        kk = jnp.concatenate(
            [_l2norm(kk_raw[:, i * head_dim:(i + 1) * head_dim]) for i in range(n_heads)], axis=-1)
        row_scr[0] = r
        row_scr[1] = k_mod
        row_scr[2] = v
        for i, arr in enumerate((r, jnp.exp(log_decay), k_mod, v, -kk, kk * a)):
            t_scr[i] = arr.T.reshape(n_heads, head_dim, bsz)

    r_t = t_scr[0, h]
    w_t = t_scr[1, h]
    k_t = t_scr[2, h]
    a_t = t_scr[4, h]
    b_t = t_scr[5, h]

    def step(i, carry):
        s_v = s_ref[i]
        sa = jnp.sum(s_v * a_t, axis=0, keepdims=True)
        v_i = t_scr[3, h, pl.ds(i, 1), :]
        s_new = s_v * w_t + sa * b_t + v_i * k_t
        so_ref[i] = s_new
        y_scr[h, pl.ds(i, 1), :] = jnp.sum(s_new * r_t, axis=0, keepdims=True)
        return carry

    lax.fori_loop(0, head_dim, step, 0, unroll=DECODE_UNROLL)

    @pl.when(h == n_heads - 1)
    def _():
        y = y_scr[...].reshape(width, bsz).T
        r, k_mod, v, z = row_scr[0], row_scr[1], row_scr[2], z_ref[...]
        outs = []
        for i in range(n_heads):
            sl = slice(i * head_dim, (i + 1) * head_dim)
            outs.append(_rwkv_post_head(y[:, sl], r[:, sl], k_mod[:, sl], v[:, sl], z[:, sl],
                                        rk_ref[:, sl], gnw_ref[:, sl], gnb_ref[:, sl]))
        o_ref[...] = jnp.concatenate(outs, axis=-1)


def _layer_state_call(body, name, row_inputs, layer_inputs, state_all, new_all, params, *,
                      layer, n_row_out, scratch_shapes, batch_minor=False, tap_inputs=()):
    bsz = row_inputs[0].shape[0]
    fixed = lambda h: (0, 0)
    at_layer = lambda h: (layer, 0)
    if batch_minor:
        n_heads = state_all.shape[1]
        state_spec = pl.BlockSpec((None, None) + state_all.shape[2:],
                                  lambda h: (layer, h, 0, 0, 0))
    else:
        n_heads = state_all.shape[2]
        state_spec = pl.BlockSpec((None, bsz, None) + state_all.shape[3:],
                                  lambda h: (layer, 0, h, 0, 0))
    in_specs = [pl.BlockSpec(a.shape, fixed) for a in row_inputs]
    in_specs += [pl.BlockSpec((bsz, a.shape[1]), at_layer) for a in layer_inputs]
    in_specs += [pl.BlockSpec((None,) + a.shape[1:], lambda h: (layer, 0, 0, 0))
                 for a in tap_inputs]
    in_specs += [state_spec]
    in_specs += [pl.BlockSpec(a.shape, fixed) for a in params]
    in_specs.append(pl.BlockSpec(memory_space=pl.ANY))
    operands = (list(row_inputs) + list(layer_inputs) + list(tap_inputs) + [state_all]
                + list(params) + [new_all])
    n_in = len(operands)
    aliases = {n_in - 1: len(n_row_out) + len(tap_inputs)}
    inner = body

    def body(*refs):
        return inner(*refs[:n_in - 1], *refs[n_in:])

    out_specs = [pl.BlockSpec((bsz, w), fixed) for w in n_row_out]
    out_specs += [pl.BlockSpec(a.shape[1:], lambda h: (0, 0, 0)) for a in tap_inputs]
    out_specs += [state_spec]
    out_shape = [jax.ShapeDtypeStruct((bsz, w), F32) for w in n_row_out]
    out_shape += [jax.ShapeDtypeStruct(a.shape[1:], F32) for a in tap_inputs]
    out_shape += [jax.ShapeDtypeStruct(state_all.shape, F32)]
    return pl.pallas_call(
        body, grid=(n_heads,), in_specs=in_specs, out_specs=out_specs, out_shape=out_shape,
        scratch_shapes=scratch_shapes, input_output_aliases=aliases,
        compiler_params=pltpu.CompilerParams(dimension_semantics=("arbitrary",),
                                             vmem_limit_bytes=VMEM_LIMIT),
        name=name,
    )(*operands)


def _rwkv_decode(p_rw, z, prev_all, state_all, new_all, prm, *, layer):
    bsz = p_rw.shape[0]
    n_heads, head_dim = state_all.shape[1], state_all.shape[2]
    width = n_heads * head_dim
    names = ("mu", "w0", "w2", "a0", "a2", "k_k", "k_a", "r_k", "gn_w", "gn_b")
    return _layer_state_call(
        functools.partial(_rwkv_decode_body, n_heads=n_heads, head_dim=head_dim), "rwkv_decode",
        [p_rw, z], [prev_all], state_all, new_all, [prm[n] for n in names],
        layer=layer, n_row_out=[width], batch_minor=True,
        scratch_shapes=[pltpu.VMEM((6, n_heads, head_dim, bsz), F32),
                        pltpu.VMEM((n_heads, head_dim, bsz), F32),
                        pltpu.VMEM((3, bsz, width), F32)])


def _ssd_decode_body(xbc_ref, small_ref, z_ref, conv_ref, s_ref, cw_ref, cb_ref, dtb_row_ref,
                     alog_row_ref, d_ref, nw_ref, o_ref, convo_ref, so_ref,
                     xt_scr, bc_scr, dec_scr, y_scr, skip_scr,
                     *, n_heads, head_dim, n_groups, d_state):
    h = pl.program_id(0)
    width = n_heads * head_dim
    bsz = xbc_ref.shape[0]
    hpg = n_heads // n_groups

    @pl.when(h == 0)
    def _():
        conv = _conv_step(xbc_ref[...], conv_ref, convo_ref, cw_ref)
        xbc = _silu(conv + cb_ref[...])
        lane_s = _iota2((1, SMALL_W), 1)
        dt = _softplus(small_ref[...] + dtb_row_ref[...])
        a_row = jnp.where(lane_s < n_heads, -jnp.exp(alog_row_ref[...]), 0.0)
        dec_scr[...] = jnp.exp(dt * a_row).T
        expand = jnp.where((_iota2((SMALL_W, width), 1) // head_dim) == _iota2((SMALL_W, width), 0),
                           1.0, 0.0).astype(F32)
        xs = xbc[:, :width]
        xt_scr[...] = (xs * _mm(dt, expand, exact=True)).T.reshape(n_heads, head_dim, bsz)
        skip_scr[...] = xs * d_ref[...]
        for i in range(2 * n_groups):
            bc_scr[i] = xbc[:, width + i * d_state:width + (i + 1) * d_state]

    g = h // hpg
    xdt_t = xt_scr[h]
    dec_row = dec_scr[pl.ds(h, 1), :]
    lane = _iota2((head_dim, bsz), 1)
    lane1 = _iota2((1, bsz), 1)

    def step(b, y_t):
        b_row = bc_scr[g, pl.ds(b, 1), :]
        c_row = bc_scr[n_groups + g, pl.ds(b, 1), :]
        s_new = (s_ref[b] * _lane_pick(dec_row, lane1, b)
                 + _lane_pick(xdt_t, lane, b) * b_row)
        so_ref[b] = s_new
        y_col = jnp.sum(s_new * c_row, axis=1, keepdims=True)
        return jnp.where(lane == b, y_col, y_t)

    y_scr[h] = lax.fori_loop(0, bsz, step, jnp.zeros((head_dim, bsz), F32), unroll=DECODE_UNROLL)

    @pl.when(h == n_heads - 1)
    def _():
        y = y_scr[...].reshape(width, bsz).T + skip_scr[...]
        o_ref[...] = _ssm_gated_norm(y, z_ref[...], nw_ref[...], n_groups)


def _ssd_decode(xbc, small, z, conv_all, state_all, new_all, prm, *, layer, n_groups):
    bsz = xbc.shape[0]
    n_heads, head_dim, d_state = state_all.shape[2:]
    width = n_heads * head_dim
    names = ("conv_w", "conv_b", "dtb_row", "alog_row", "d", "norm_w")
    return _layer_state_call(
        functools.partial(_ssd_decode_body, n_heads=n_heads, head_dim=head_dim,
                          n_groups=n_groups, d_state=d_state), "ssd_decode",
        [xbc, small, z], [], state_all, new_all, [prm[n] for n in names],
        layer=layer, n_row_out=[width], tap_inputs=[conv_all],
        scratch_shapes=[pltpu.VMEM((n_heads, head_dim, bsz), F32),
                        pltpu.VMEM((2 * n_groups, bsz, d_state), F32),
                        pltpu.VMEM((SMALL_W, bsz), F32),
                        pltpu.VMEM((n_heads, head_dim, bsz), F32),
                        pltpu.VMEM((bsz, width), F32)])


def _gdn_decode_body(qkv_ref, small_ref, z_ref, conv_ref, s_ref, cw_ref, dtb_row_ref,
                     alog_row_ref, nw_ref, o_ref, convo_ref, so_ref,
                     t_scr, v_scr, gate_scr, y_scr, *, n_heads, head_dim, a_off, b_off):
    h = pl.program_id(0)
    width = n_heads * head_dim
    bsz = qkv_ref.shape[0]

    @pl.when(h == 0)
    def _():
        qkv = _silu(_conv_step(qkv_ref[...], conv_ref, convo_ref, cw_ref))
        scale = head_dim ** -0.5
        q = jnp.concatenate([_l2norm(qkv[:, i * head_dim:(i + 1) * head_dim]) * scale
                             for i in range(n_heads)], axis=-1)
        k = jnp.concatenate([_l2norm(qkv[:, width + i * head_dim:width + (i + 1) * head_dim])
                             for i in range(n_heads)], axis=-1)
        for i, arr in enumerate((q, k)):
            t_scr[i] = arr.T.reshape(n_heads, head_dim, bsz)
        for i in range(n_heads):
            v_scr[i] = qkv[:, 2 * width + i * head_dim:2 * width + (i + 1) * head_dim]
        small = small_ref[...]
        lane_s = _iota2((1, SMALL_W), 1)
        g = jnp.where(jnp.logical_and(lane_s >= a_off, lane_s < a_off + n_heads),
                      -jnp.exp(alog_row_ref[...]) * _softplus(small + dtb_row_ref[...]), 0.0)
        gate_scr[0] = jnp.exp(g).T
        gate_scr[1] = _sigmoid(small).T
        gate_scr[2, 0:n_heads, :] = jnp.concatenate(
            [jnp.sum(q[:, i * head_dim:(i + 1) * head_dim] * k[:, i * head_dim:(i + 1) * head_dim],
                     axis=1, keepdims=True) for i in range(n_heads)], axis=1).T

    q_t = t_scr[0, h]
    k_t = t_scr[1, h]
    eg_row = gate_scr[0, pl.ds(a_off + h, 1), :]
    beta_row = gate_scr[1, pl.ds(b_off + h, 1), :]
    qk_row = gate_scr[2, pl.ds(h, 1), :]
    lane = _iota2((head_dim, bsz), 1)
    lane1 = _iota2((1, bsz), 1)

    def step(b, carry):
        s_b = s_ref[b]
        k_col = _lane_pick(k_t, lane, b)
        q_col = _lane_pick(q_t, lane, b)
        eg = _lane_pick(eg_row, lane1, b)
        k_s = jnp.sum(s_b * k_col, axis=0, keepdims=True)
        q_s = jnp.sum(s_b * q_col, axis=0, keepdims=True)
        v_new = _lane_pick(beta_row, lane1, b) * (v_scr[h, pl.ds(b, 1), :] - eg * k_s)
        y_scr[h, pl.ds(b, 1), :] = eg * q_s + _lane_pick(qk_row, lane1, b) * v_new
        so_ref[b] = s_b * eg + k_col * v_new
        return carry

    lax.fori_loop(0, bsz, step, 0, unroll=DECODE_UNROLL)

    @pl.when(h == n_heads - 1)
    def _():
        z = z_ref[...]
        outs = []
        for i in range(n_heads):
            sl = slice(i * head_dim, (i + 1) * head_dim)
            outs.append(_gdn_post_head(y_scr[i], z[:, sl], nw_ref[...]))
        o_ref[...] = jnp.concatenate(outs, axis=-1)


def _gdn_decode(qkv, small, z, conv_all, state_all, new_all, prm, *, layer, a_off, b_off):
    bsz = qkv.shape[0]
    n_heads, head_dim = state_all.shape[2], state_all.shape[3]
    width = n_heads * head_dim
    names = ("conv_w", "dtb_row", "alog_row", "norm_w")
    return _layer_state_call(
        functools.partial(_gdn_decode_body, n_heads=n_heads, head_dim=head_dim,
                          a_off=a_off, b_off=b_off), "gdn_decode",
        [qkv, small, z], [], state_all, new_all, [prm[n] for n in names],
        layer=layer, n_row_out=[width], tap_inputs=[conv_all],
        scratch_shapes=[pltpu.VMEM((2, n_heads, head_dim, bsz), F32),
                        pltpu.VMEM((n_heads, bsz, head_dim), F32),
                        pltpu.VMEM((3, SMALL_W, bsz), F32),
                        pltpu.VMEM((n_heads, bsz, head_dim), F32)])


def _segment_ones(width, seg):
    i = jnp.arange(2 * width)[:, None] % width
    j = jnp.arange(width)[None, :]
    return ((i // seg) == (j // seg)).astype(BF16)


def _pad_lanes(row, offset):
    out = jnp.zeros((SMALL_W,), F32)
    return lax.dynamic_update_slice(out, row.astype(F32), (offset,)).reshape(1, SMALL_W)


def _row_tile(seq_len, cap):
    best = 8
    for tm in range(8, cap + 1, 8):
        if seq_len % tm == 0:
            best = tm
    return best


def kernel(x_prompt, x_sample, state_rwkv_wkv, state_rwkv_shift, state_ssm, state_ssm_conv,
           state_gdn, state_gdn_conv, meta_tokens, norm_w, w_in, rw_mu, rw_w0, rw_w2, rw_a0,
           rw_a2, rw_k_k, rw_k_a, rw_r_k, rw_gn_w, rw_gn_b, ssm_conv_w, ssm_conv_b,
           ssm_dt_bias, ssm_a_log, ssm_d, ssm_norm_w, gdn_conv_w, gdn_dt_bias, gdn_a_log,
           gdn_norm_w, w_rw_out, w_ssm_out, w_gdn_out, w_out, final_norm_w):
    depth = w_in.shape[0]
    bp, seq, d_model = x_prompt.shape
    bd = x_sample.shape[0]
    n_meta = meta_tokens.shape[0]
    rw_heads, rw_hd = rw_r_k.shape[1], rw_r_k.shape[2]
    rw_w = rw_heads * rw_hd
    rw_shift_w = rw_mu.shape[1]
    ssm_heads, ssm_hd, ssm_state = state_ssm.shape[2], state_ssm.shape[3], state_ssm.shape[4]
    ssm_w = ssm_heads * ssm_hd
    ssm_conv_ch = ssm_conv_w.shape[2]
    ssm_groups = (ssm_conv_ch - ssm_w) // (2 * ssm_state)
    gdn_heads, gdn_hd = state_gdn.shape[2], state_gdn.shape[3]
    gdn_w = gdn_heads * gdn_hd
    gdn_conv_ch = gdn_conv_w.shape[2]
    a_off, b_off = ssm_heads, ssm_heads + gdn_heads
    assert b_off + gdn_heads <= SMALL_W

    off = 0
    cols = {}
    for name, wdt in (("rw", rw_shift_w), ("rw_z", rw_w), ("ssm_z", ssm_w), ("xbc", ssm_conv_ch),
                      ("dt", ssm_heads), ("qkv", gdn_conv_ch), ("gdn_z", gdn_w),
                      ("ga", gdn_heads), ("gb", gdn_heads), ("gate", 3 * d_model)):
        cols[name] = (off, off + wdt)
        off += wdt
    assert off == w_in.shape[2]

    seg_names = ("rw", "rw_z", "ssm_z", "xbc", "qkv", "gdn_z", "gate", "dt", "ga", "gb")
    n_small = ssm_heads + 2 * gdn_heads
    w_all = _pack_weights(jnp.swapaxes(w_in, 1, 2), [cols[n] for n in seg_names],
                          SMALL_W - n_small, kb=256)
    widths = [cols[n][1] - cols[n][0] for n in seg_names[:7]] + [SMALL_W]
    assert all(w % 128 == 0 for w in widths)
    w_r, w_s, w_g, w_o = (w.astype(BF16) for w in (w_rw_out, w_ssm_out, w_gdn_out, w_out))

    def layer_params(l):
        rw = dict(mu=rw_mu[l][None], w0=rw_w0[l][None], w2=rw_w2[l], a0=rw_a0[l][None],
                  a2=rw_a2[l], k_k=rw_k_k[l][None], k_a=rw_k_a[l][None],
                  r_k=rw_r_k[l].reshape(1, rw_w), gn_w=rw_gn_w[l][None], gn_b=rw_gn_b[l][None])
        dtb = _pad_lanes(ssm_dt_bias[l], 0)
        alog = _pad_lanes(ssm_a_log[l], 0)
        sm = dict(conv_w=ssm_conv_w[l], conv_b=ssm_conv_b[l][None], dtb_row=dtb,
                  dtb_col=dtb.reshape(SMALL_W, 1), alog_row=alog,
                  alog_col=alog.reshape(SMALL_W, 1),
                  d=jnp.repeat(ssm_d[l], ssm_hd)[None], norm_w=ssm_norm_w[l][None])
        gdtb = _pad_lanes(gdn_dt_bias[l], a_off)
        galog = _pad_lanes(gdn_a_log[l], a_off)
        gd = dict(conv_w=gdn_conv_w[l], dtb_row=gdtb, dtb_col=gdtb.reshape(SMALL_W, 1),
                  alog_row=galog, alog_col=galog.reshape(SMALL_W, 1),
                  norm_w=gdn_norm_w[l][None])
        return rw, sm, gd

    fw = final_norm_w[None]

    real_len = n_meta + seq
    seq_p = -(-real_len // CHUNK) * CHUNK
    n_pad = seq_p - real_len
    n_chunks = seq_p // CHUNK
    n_sub = max(d for d in range(1, MAX_CHUNKS_PER_STEP + 1) if n_chunks % d == 0)
    n_steps = n_chunks // n_sub
    n_seq = max(d for d in range(1, MAX_SEQS_PER_STEP + 1) if bp % d == 0)
    meta = jnp.broadcast_to(meta_tokens.astype(F32)[None], (bp, n_meta, d_model))
    xp = jnp.concatenate([jnp.zeros((bp, n_pad, d_model), F32), meta, x_prompt], axis=1)
    xp = xp.reshape(bp * seq_p, d_model)
    tm_in = _row_tile(seq_p, 384)
    tm_out = _row_tile(seq_p, 384)

    p_states = [[] for _ in range(6)]
    y_p = None
    for l in range(depth):
        rw_p, sm_p, gd_p = layer_params(l)
        p_rw, rw_z, ssm_z, xbc, qkv, gdn_z, gate, small = _inproj(
            xp, norm_w[l][None], w_all, widths, layer=l, tm=tm_in, n_pad=n_pad, seq_len=seq_p)
        rw_o, wkv, shift = _rwkv_prompt(p_rw, rw_z, rw_p, batch=bp, n_steps=n_steps,
                                        n_sub=n_sub, n_seq=n_seq, n_heads=rw_heads,
                                        head_dim=rw_hd)
        sm_o, ssm, sconv = _ssd_prompt(xbc, small, ssm_z, sm_p, batch=bp, n_steps=n_steps,
                                       n_sub=n_sub, n_seq=n_seq, n_heads=ssm_heads,
                                       head_dim=ssm_hd, n_groups=ssm_groups, d_state=ssm_state,
                                       n_pad=n_pad)
        gd_o, gdn, gconv = _gdn_prompt(qkv, small, gdn_z, gd_p, batch=bp, n_steps=n_steps,
                                       n_sub=n_sub, n_seq=n_seq, n_heads=gdn_heads,
                                       head_dim=gdn_hd, a_off=a_off, b_off=b_off)
        last = l == depth - 1
        res = _outproj(rw_o, sm_o, gd_o, gate, xp, w_r[l], w_s[l], w_g[l], w_o[l], fw,
                       tm=tm_out, with_final=last)
        xp = res[0]
        if last:
            y_p = res[1]
        for acc, s in zip(p_states, (wkv, shift.reshape(bp, rw_shift_w), ssm, sconv, gdn, gconv)):
            acc.append(s)
    y_prompt = y_p.reshape(bp, seq_p, d_model)[:, n_pad + n_meta:]

    xs = x_sample.reshape(bd, d_model)
    shift_all = state_rwkv_shift.reshape(depth * bd, -1)
    sconv_all = jnp.swapaxes(state_ssm_conv, 1, 2)
    gconv_all = jnp.swapaxes(state_gdn_conv, 1, 2)
    wkv_all = jnp.transpose(state_rwkv_wkv, (0, 2, 3, 4, 1))
    wkv_new, ssm_new, gdn_new = (jnp.zeros(a.shape, F32) for a in (wkv_all, state_ssm, state_gdn))
    small_states = [[] for _ in range(3)]
    y_s = None
    for l in range(depth):
        rw_p, sm_p, gd_p = layer_params(l)
        p_rw, rw_z, ssm_z, xbc, qkv, gdn_z, gate, small = _inproj(
            xs, norm_w[l][None], w_all, widths, layer=l, tm=bd, n_pad=0, seq_len=bd)
        rw_o, wkv_new = _rwkv_decode(p_rw, rw_z, shift_all, wkv_all, wkv_new, rw_p, layer=l)
        sm_o, sconv, ssm_new = _ssd_decode(xbc, small, ssm_z, sconv_all, state_ssm, ssm_new, sm_p,
                                           layer=l, n_groups=ssm_groups)
        gd_o, gconv, gdn_new = _gdn_decode(qkv, small, gdn_z, gconv_all, state_gdn, gdn_new, gd_p,
                                           layer=l, a_off=a_off, b_off=b_off)
        last = l == depth - 1
        res = _outproj(rw_o, sm_o, gd_o, gate, xs, w_r[l], w_s[l], w_g[l], w_o[l], fw,
                       tm=bd, with_final=last)
        xs = res[0]
        if last:
            y_s = res[1]
        for acc, s in zip(small_states, (p_rw, sconv, gconv)):
            acc.append(s)
    y_sample = y_s.reshape(bd, 1, d_model)
    s_shift, s_sconv, s_gconv = (jnp.stack(a) for a in small_states)
    s_states = (jnp.transpose(wkv_new, (0, 4, 1, 2, 3)), s_shift, ssm_new,
                jnp.swapaxes(s_sconv, 1, 2), gdn_new, jnp.swapaxes(s_gconv, 1, 2))

    return (y_prompt, y_sample) + tuple(jnp.stack(a) for a in p_states) + s_states
```

```python
import functools
import math

import jax
import jax.numpy as jnp
from jax import lax
from jax.experimental import pallas as pl
from jax.experimental.pallas import tpu as pltpu

F32 = jnp.float32
BF16 = jnp.bfloat16
HIGHEST = lax.Precision.HIGHEST

NORM_EPS = 1e-6
RW_GN_EPS = 64e-5
SSM_NORM_EPS = 1e-5
GDN_NORM_EPS = 1e-6
L2_EPS = 1e-6
CONV_W = 4
CHUNK = 64
MAX_CHUNKS_PER_STEP = 3
MAX_SEQS_PER_STEP = 2
INV_BASE = 8
SEG_TILE = 256
DECODE_UNROLL = 8
SMALL_W = 128
CONV_HALO = 8
VMEM_LIMIT = 56 * 1024 * 1024


def _mm(a, b, exact=False):
    if exact:
        return jnp.dot(a, b, preferred_element_type=F32, precision=HIGHEST)
    return jnp.dot(a.astype(BF16), b.astype(BF16), preferred_element_type=F32)


def _mm_nt(a, b, exact=False):
    dn = (((1,), (1,)), ((), ()))
    if exact:
        return lax.dot_general(a, b, dn, preferred_element_type=F32, precision=HIGHEST)
    return lax.dot_general(a.astype(BF16), b.astype(BF16), dn, preferred_element_type=F32)


def _mm_tn(a, b, exact=False):
    dn = (((0,), (0,)), ((), ()))
    if exact:
        return lax.dot_general(a, b, dn, preferred_element_type=F32, precision=HIGHEST)
    return lax.dot_general(a.astype(BF16), b.astype(BF16), dn, preferred_element_type=F32)


def _sigmoid(x):
    return 1.0 / (1.0 + jnp.exp(-x))


def _silu(x):
    return x * _sigmoid(x)


def _softplus(x):
    return jnp.maximum(x, 0.0) + jnp.log(1.0 + jnp.exp(-jnp.abs(x)))


def _iota2(shape, dim):
    return lax.broadcasted_iota(jnp.int32, shape, dim)


def _seg_sum(x, seg_ones, split=False):
    w = seg_ones.shape[1]
    parts = []
    for i in range(x.shape[1] // w):
        xi = x[:, i * w:(i + 1) * w]
        x_hi = xi.astype(BF16)
        if split:
            x_lo = (xi - x_hi.astype(F32)).astype(BF16)
            parts.append(jnp.dot(jnp.concatenate([x_hi, x_lo], axis=1), seg_ones,
                                 preferred_element_type=F32))
        else:
            parts.append(jnp.dot(x_hi, seg_ones[:w], preferred_element_type=F32))
    return jnp.concatenate(parts, axis=1)


def _block_diag_pair(p):
    c = p.shape[0]
    lane = _iota2(p.shape, 1)
    pb = p.astype(BF16)
    zero = jnp.zeros_like(pb)
    return jnp.concatenate([jnp.where(lane < c, pb, zero), jnp.where(lane >= c, pb, zero)], axis=0)


def _mm_pair(x, p):
    return jnp.dot(x.astype(BF16), _block_diag_pair(p), preferred_element_type=F32)


def _unit_lower_inverse(l_mats):
    c = l_mats[0].shape[0]
    pairs = [jnp.concatenate(l_mats[i:i + 2], axis=1) for i in range(0, len(l_mats), 2)]
    ri = _iota2((c, 2 * c), 0)
    ci = _iota2((c, 2 * c), 1) & (c - 1)
    base_shift = int(math.log2(INV_BASE))
    same_base = (ri >> base_shift) == (ci >> base_shift)
    es = [jnp.where(same_base, l, 0.0) for l in pairs]
    ps = es
    n_terms = 1
    while 2 * n_terms < INV_BASE:
        ps = [_mm_pair(p, p) for p in ps]
        eps = [_mm_pair(e, p) for e, p in zip(es, ps)]
        es = [e + p + ep for e, p, ep in zip(es, ps, eps)]
        n_terms *= 2
    shift = base_shift
    while (1 << shift) < c:
        same_outer = (ri >> (shift + 1)) == (ci >> (shift + 1))
        other_inner = (ri >> shift) != (ci >> shift)
        l_off = [jnp.where(same_outer, jnp.where(other_inner, l, 0.0), 0.0) for l in pairs]
        ts = [lo + _mm_pair(lo, e) for lo, e in zip(l_off, es)]
        ets = [_mm_pair(e, t) for e, t in zip(es, ts)]
        es = [e + t + et for e, t, et in zip(es, ts, ets)]
        shift += 1
    out = []
    for e in es:
        out += [e[:, :c], e[:, c:]]
    return out


def _chunk_masks(c):
    ri = _iota2((c, c), 0)
    ci = _iota2((c, c), 1)
    return ri >= ci, ri > ci


def _block_tril(rows, c):
    shift = int(math.log2(c))
    ri = _iota2((rows, rows), 0)
    ci = _iota2((rows, rows), 1)
    return jnp.where((ri >> shift) == (ci >> shift),
                     jnp.where(ri >= ci, 1.0, 0.0), 0.0).astype(F32)


def _conv_chunk(ext_ref, u, w_ref, first_chunk):
    c = u.shape[0]

    @pl.when(first_chunk)
    def _():
        ext_ref[0:CONV_HALO, :] = jnp.zeros((CONV_HALO, u.shape[1]), F32)

    ext_ref[CONV_HALO:CONV_HALO + c, :] = u
    out = u * w_ref[CONV_W - 1:CONV_W, :]
    for i in range(CONV_W - 1):
        off = CONV_HALO - (CONV_W - 1) + i
        out = out + ext_ref[off:off + c, :] * w_ref[i:i + 1, :]
    tail = ext_ref[c:c + CONV_HALO, :]
    ext_ref[0:CONV_HALO, :] = tail
    return out, tail[CONV_HALO - (CONV_W - 1):, :]


def _inproj_body(x_ref, nw_ref, w_ref, *o_refs, n_pad, tiles_per_seq):
    x = x_ref[...]
    h = x * lax.rsqrt(jnp.mean(x * x, axis=-1, keepdims=True) + NORM_EPS) * nw_ref[...]
    if n_pad:
        first = lax.rem(pl.program_id(0), tiles_per_seq) == 0
        row = _iota2((x.shape[0], 1), 0)
        h = jnp.where(jnp.logical_and(first, row < n_pad), 0.0, h)
    hb = h.astype(BF16)
    off = 0
    for o_ref in o_refs:
        width = o_ref.shape[1]
        o_ref[...] = jnp.dot(hb, w_ref[:, off:off + width], preferred_element_type=F32)
        off += width


def _inproj(x, norm_w, w_all, widths, *, layer, tm, n_pad, seq_len):
    t, d = x.shape
    assert w_all.shape[2] == sum(widths)
    in_specs = [pl.BlockSpec((tm, d), lambda i: (i, 0)),
                pl.BlockSpec((1, d), lambda i: (0, 0)),
                pl.BlockSpec((None,) + w_all.shape[1:], lambda i: (layer, 0, 0),
                             pipeline_mode=pl.Buffered(1))]
    out_specs = [pl.BlockSpec((tm, w), lambda i: (i, 0)) for w in widths]
    out_shape = [jax.ShapeDtypeStruct((t, w), F32) for w in widths]
    body = functools.partial(_inproj_body, n_pad=n_pad,
                             tiles_per_seq=(seq_len // tm) if n_pad else 1)
    return pl.pallas_call(
        body, grid=(t // tm,), in_specs=in_specs, out_specs=out_specs, out_shape=out_shape,
        compiler_params=pltpu.CompilerParams(dimension_semantics=("arbitrary",),
                                             vmem_limit_bytes=VMEM_LIMIT),
        name="inproj",
    )(x, norm_w, w_all)


def _pack_weights_body(w_ref, o_ref, *, segments, pad):
    off = 0
    tail = []
    for lo, hi in segments:
        if (hi - lo) % 128 == 0 and not tail:
            o_ref[:, off:off + hi - lo] = w_ref[lo:hi, :].T.astype(o_ref.dtype)
            off += hi - lo
        else:
            tail.append(w_ref[lo:hi, :])
    if tail or pad:
        if pad:
            tail.append(jnp.zeros((pad, w_ref.shape[1]), w_ref.dtype))
        o_ref[:, off:] = jnp.concatenate(tail, axis=0).T.astype(o_ref.dtype)


def _pack_weights(w_t, segments, pad, *, kb):
    depth, n, d = w_t.shape
    total = sum(hi - lo for lo, hi in segments) + pad
    return pl.pallas_call(
        functools.partial(_pack_weights_body, segments=tuple(segments), pad=pad),
        grid=(depth, d // kb),
        in_specs=[pl.BlockSpec((None, n, kb), lambda l, i: (l, 0, i))],
        out_specs=pl.BlockSpec((None, kb, total), lambda l, i: (l, i, 0)),
        out_shape=jax.ShapeDtypeStruct((depth, d, total), BF16),
        compiler_params=pltpu.CompilerParams(dimension_semantics=("arbitrary", "arbitrary"),
                                             vmem_limit_bytes=VMEM_LIMIT),
        name="pack_weights",
    )(w_t)


def _outproj_body(rw_ref, sm_ref, gd_ref, gate_ref, x_ref, wr_ref, ws_ref, wg_ref, wo_ref,
                  fw_ref, o_ref, *maybe_y_ref, d_model):
    merged = None
    for b, (y_ref, w_ref) in enumerate(((rw_ref, wr_ref), (sm_ref, ws_ref), (gd_ref, wg_ref))):
        g = _sigmoid(gate_ref[:, b * d_model:(b + 1) * d_model])
        term = g * jnp.dot(y_ref[...].astype(BF16), w_ref[...], preferred_element_type=F32)
        merged = term if merged is None else merged + term
    x_new = x_ref[...] + jnp.dot(merged.astype(BF16), wo_ref[...], preferred_element_type=F32)
    o_ref[...] = x_new
    if maybe_y_ref:
        ms = jnp.mean(x_new * x_new, axis=-1, keepdims=True)
        maybe_y_ref[0][...] = x_new * lax.rsqrt(ms + NORM_EPS) * fw_ref[...]


def _outproj(rw, sm, gd, gate, x, w_r, w_s, w_g, w_o, final_w, *, tm, with_final):
    t, d = x.shape
    grid = (t // tm,)
    row = lambda i: (i, 0)
    fixed = lambda i: (0, 0)
    in_specs = [pl.BlockSpec((tm, rw.shape[1]), row), pl.BlockSpec((tm, sm.shape[1]), row),
                pl.BlockSpec((tm, gd.shape[1]), row), pl.BlockSpec((tm, gate.shape[1]), row),
                pl.BlockSpec((tm, d), row),
                pl.BlockSpec(w_r.shape, fixed), pl.BlockSpec(w_s.shape, fixed),
                pl.BlockSpec(w_g.shape, fixed), pl.BlockSpec(w_o.shape, fixed),
                pl.BlockSpec((1, d), fixed)]
    n_out = 2 if with_final else 1
    out_specs = [pl.BlockSpec((tm, d), row)] * n_out
    out_shape = [jax.ShapeDtypeStruct((t, d), F32)] * n_out
    return pl.pallas_call(
        functools.partial(_outproj_body, d_model=d),
        grid=grid, in_specs=in_specs, out_specs=out_specs, out_shape=out_shape,
        compiler_params=pltpu.CompilerParams(dimension_semantics=("arbitrary",),
                                             vmem_limit_bytes=VMEM_LIMIT),
        name="outproj",
    )(rw, sm, gd, gate, x, w_r, w_s, w_g, w_o, final_w)


def _rwkv_pre(p, shifted, mu, w0, w2, a0, a2, k_k, k_a, width, lora):
    u = p + (shifted - p) * mu
    r = u[:, :width]
    k = u[:, width:2 * width]
    v = u[:, 2 * width:3 * width]
    wd = u[:, 3 * width:3 * width + lora]
    ad = u[:, 3 * width + lora:]
    w = -_softplus(-(w0 + _mm(jnp.tanh(wd), w2))) - 0.5
    log_decay = -jnp.exp(w)
    a = _sigmoid(a0 + _mm(ad, a2))
    kk_raw = k * k_k
    k_mod = k * (1.0 + (a - 1.0) * k_a)
    return r, k_mod, v, log_decay, a, kk_raw


def _l2norm(x):
    return x * lax.rsqrt(jnp.sum(x * x, axis=-1, keepdims=True) + L2_EPS)


def _rwkv_post_head(y, r, k_mod, v, z, r_k, gn_w, gn_b):
    mean = jnp.mean(y, axis=-1, keepdims=True)
    var = jnp.mean(jnp.square(y - mean), axis=-1, keepdims=True)
    yn = (y - mean) * lax.rsqrt(var + RW_GN_EPS) * gn_w + gn_b
    bonus = jnp.sum(r * k_mod * r_k, axis=-1, keepdims=True) * v
    return (yn + bonus) * _silu(z)


def _rwkv_prompt_body(p_ref, z_ref, ones_ref, mu_ref, w0_ref, w2_ref, a0_ref, a2_ref, kk_ref,
                      ka_ref, rk_ref, gnw_ref, gnb_ref, o_ref, wkv_ref, shift_ref, prev_scr,
                      *, n_heads, head_dim, n_sub):
    width = n_heads * head_dim
    lora = w2_ref.shape[0]
    n_seq, rows, shift_w = p_ref.shape
    c = rows // n_sub
    p = p_ref[...].reshape(n_seq * rows, shift_w)

    @pl.when(pl.program_id(1) == 0)
    def _():
        wkv_ref[...] = jnp.zeros(wkv_ref.shape, F32)
        prev_scr[...] = jnp.zeros(prev_scr.shape, F32)

    row = _iota2((n_seq * rows, 1), 0)
    shifted = pltpu.roll(p, 1, 0)
    for s in range(n_seq):
        shifted = jnp.where(row == s * rows, prev_scr[s], shifted)
        last_row = p[(s + 1) * rows - 1:(s + 1) * rows, :]
        prev_scr[s] = last_row
        shift_ref[s] = last_row

    r, k_mod, v, log_decay, a, kk_raw = _rwkv_pre(
        p, shifted, mu_ref[...], w0_ref[...], w2_ref[...], a0_ref[...], a2_ref[...],
        kk_ref[...], ka_ref[...], width, lora)
    tril = _block_tril(rows, c)
    cum = jnp.concatenate([_mm(tril, log_decay[s * rows:(s + 1) * rows], exact=True)
                           for s in range(n_seq)], axis=0)
    e_pos = jnp.exp(cum)
    e_neg = jnp.exp(-cum)
    e_prev = jnp.exp(cum - log_decay)
    z = z_ref[...].reshape(n_seq * rows, width)
    incl, strict = _chunk_masks(c)

    hs = range(n_heads)
    sls = [slice(h * head_dim, (h + 1) * head_dim) for h in hs]
    n_chunks = n_seq * n_sub
    units = [(j, h) for j in range(n_chunks) for h in hs]
    rsl = [slice(j * c, (j + 1) * c) for j in range(n_chunks)]
    seg_ones = ones_ref[...]
    kk = kk_raw * lax.rsqrt(_seg_sum(kk_raw * kk_raw, seg_ones) + L2_EPS)
    a_all = -kk * e_prev
    b_all = kk * a * e_neg
    k_all = k_mod * e_neg
    r_all = r * e_pos
    a_t = [a_all[:, sl] for sl in sls]
    b_t = [b_all[:, sl] for sl in sls]
    k_t = [k_all[:, sl] for sl in sls]
    r_t = [r_all[:, sl] for sl in sls]
    v_h = [v[:, sl] for sl in sls]

    sc = {u: _mm_nt(jnp.concatenate([a_t[u[1]][rsl[u[0]]], r_t[u[1]][rsl[u[0]]]], axis=0),
                    jnp.concatenate([b_t[u[1]][rsl[u[0]]], k_t[u[1]][rsl[u[0]]]], axis=0))
          for u in units}
    l_ab = {u: jnp.where(strict, sc[u][:c, :c], 0.0) for u in units}
    l_ak = {u: jnp.where(strict, sc[u][:c, c:], 0.0) for u in units}
    m_rb = {u: jnp.where(incl, sc[u][c:, :c], 0.0) for u in units}
    m_rk = {u: jnp.where(incl, sc[u][c:, c:], 0.0) for u in units}
    lv = {u: _mm(jnp.concatenate([l_ak[u], m_rk[u]], axis=0), v_h[u[1]][rsl[u[0]]]) for u in units}
    inv = dict(zip(units, _unit_lower_inverse([l_ab[u] for u in units])))
    rhs = {u: jnp.concatenate([a_t[u[1]][rsl[u[0]]], lv[u][:c]], axis=1) for u in units}
    wu = {u: rhs[u] + _mm(inv[u], rhs[u]) for u in units}
    d_end = {(j, h): e_pos[(j + 1) * c - 1:(j + 1) * c, sls[h]] for j, h in units}
    qy = {u: _mm(m_rb[u], wu[u]) for u in units}
    mn = {u: _mm_tn(wu[u], b_t[u[1]][rsl[u[0]]] * d_end[u]) for u in units}
    n2 = {u: _mm_tn(v_h[u[1]][rsl[u[0]]], k_t[u[1]][rsl[u[0]]] * d_end[u]) for u in units}

    sh = [(s, h) for s in range(n_seq) for h in hs]
    states = {(s, h): wkv_ref[s, h] for s, h in sh}
    ys = {k: [] for k in sh}
    for jj in range(n_sub):
        ju = {(s, h): (s * n_sub + jj, h) for s, h in sh}
        y_dot = {k: _mm_nt(r_t[k[1]][rsl[ju[k][0]]] + qy[ju[k]][:, :c], states[k]) for k in sh}
        s_dot = {k: _mm(states[k], mn[ju[k]][:c]) for k in sh}
        for k in sh:
            u = ju[k]
            ys[k].append(y_dot[k] + qy[u][:, c:] + lv[u][c:])
            states[k] = states[k] * d_end[u] + s_dot[k] + (mn[u][c:] + n2[u])
    for s, h in sh:
        wkv_ref[s, h] = states[(s, h)]
    y = jnp.concatenate([jnp.concatenate([t for s in range(n_seq) for t in ys[(s, h)]], axis=0)
                         for h in hs], axis=-1)
    inv_n = 1.0 / head_dim
    dev = y - _seg_sum(y, seg_ones) * inv_n
    var = _seg_sum(dev * dev, seg_ones) * inv_n
    yn = dev * lax.rsqrt(var + RW_GN_EPS) * gnw_ref[...] + gnb_ref[...]
    bonus = _seg_sum(r * k_mod * rk_ref[...], seg_ones, split=True) * v
    o_ref[...] = ((yn + bonus) * _silu(z)).reshape(n_seq, rows, width)


def _seq_specs(n_seq, rows, n_steps, arrays):
    views = [a.reshape(-1, n_steps * rows, a.shape[1]) for a in arrays]
    specs = [pl.BlockSpec((n_seq, rows, a.shape[1]), lambda b, c: (b, c, 0)) for a in arrays]
    return views, specs


def _rwkv_prompt(p_rw, z, prm, *, batch, n_steps, n_sub, n_seq, n_heads, head_dim):
    t, shift_w = p_rw.shape
    width = n_heads * head_dim
    rows = n_sub * CHUNK
    fixed = lambda b, c: (0, 0)
    names = ("mu", "w0", "w2", "a0", "a2", "k_k", "k_a", "r_k", "gn_w", "gn_b")
    params = [_segment_ones(SEG_TILE, head_dim)] + [prm[n] for n in names]
    views, in_specs = _seq_specs(n_seq, rows, n_steps, [p_rw, z])
    in_specs += [pl.BlockSpec(a.shape, fixed) for a in params]
    out_specs = [pl.BlockSpec((n_seq, rows, width), lambda b, c: (b, c, 0)),
                 pl.BlockSpec((n_seq, n_heads, head_dim, head_dim), lambda b, c: (b, 0, 0, 0)),
                 pl.BlockSpec((n_seq, 1, shift_w), lambda b, c: (b, 0, 0))]
    out_shape = [jax.ShapeDtypeStruct((batch, n_steps * rows, width), F32),
                 jax.ShapeDtypeStruct((batch, n_heads, head_dim, head_dim), F32),
                 jax.ShapeDtypeStruct((batch, 1, shift_w), F32)]
    o, wkv, shift = pl.pallas_call(
        functools.partial(_rwkv_prompt_body, n_heads=n_heads, head_dim=head_dim, n_sub=n_sub),
        grid=(batch // n_seq, n_steps), in_specs=in_specs, out_specs=out_specs,
        out_shape=out_shape,
        scratch_shapes=[pltpu.VMEM((n_seq, 1, shift_w), F32)],
        compiler_params=pltpu.CompilerParams(dimension_semantics=("arbitrary", "arbitrary"),
                                             vmem_limit_bytes=VMEM_LIMIT),
        name="rwkv_prompt",
    )(*views, *params)
    return o.reshape(t, width), wkv, shift


def _ssm_gated_norm(y, z, norm_w, n_groups):
    g = y * _silu(z)
    gw = g.shape[1] // n_groups
    parts = []
    for i in range(n_groups):
        gi = g[:, i * gw:(i + 1) * gw]
        parts.append(gi * lax.rsqrt(jnp.mean(gi * gi, axis=-1, keepdims=True) + SSM_NORM_EPS))
    return jnp.concatenate(parts, axis=-1) * norm_w


def _ssd_prompt_body(xbc_ref, small_ref, z_ref, cw_ref, cb_ref, dtb_row_ref, dtb_col_ref,
                     alog_row_ref, alog_col_ref, d_ref, nw_ref, o_ref, h_ref, conv_ref, ext_scr,
                     *, n_heads, head_dim, n_groups, d_state, n_pad, n_sub):
    first = pl.program_id(1) == 0
    width = n_heads * head_dim
    hpg = n_heads // n_groups
    gw = hpg * head_dim
    n_seq, rows, _ = xbc_ref.shape
    c = rows // n_sub
    n_chunks = n_seq * n_sub
    rsl = [slice(j * c, (j + 1) * c) for j in range(n_chunks)]

    @pl.when(first)
    def _():
        h_ref[...] = jnp.zeros(h_ref.shape, F32)

    convs = []
    for s in range(n_seq):
        conv, tail = _conv_chunk(ext_scr.at[s], xbc_ref[s], cw_ref, first)
        conv_ref[s] = tail
        convs.append(conv)
    xbc = _silu(jnp.concatenate(convs, axis=0) + cb_ref[...])
    xs = xbc[:, :width]
    b_all = xbc[:, width:width + n_groups * d_state]
    c_all = xbc[:, width + n_groups * d_state:]

    small = small_ref[...].reshape(n_seq * rows, SMALL_W)
    lane = _iota2((1, SMALL_W), 1)
    sub = _iota2((SMALL_W, 1), 0)
    dt = _softplus(small + dtb_row_ref[...])
    assert n_heads <= 8
    dt_t = [_softplus(small[sl].T[:8] + dtb_col_ref[0:8, :]) for sl in rsl]
    if n_pad:
        assert n_pad <= c
        not_first = jnp.logical_not(first)
        row = _iota2((n_seq * rows, 1), 0)
        for s in range(n_seq):
            in_pad = jnp.logical_and(row >= s * rows, row < s * rows + n_pad)
            dt = jnp.where(jnp.logical_and(first, in_pad), 0.0, dt)
        for s in range(n_seq):
            j0 = s * n_sub
            dt_t[j0] = jnp.where(jnp.logical_or(not_first, _iota2((1, c), 1) >= n_pad),
                                 dt_t[j0], 0.0)
    a_row = jnp.where(lane < n_heads, -jnp.exp(alog_row_ref[...]), 0.0)
    a_col = jnp.where(sub[:8] < n_heads, -jnp.exp(alog_col_ref[0:8, :]), 0.0)
    incl, _ = _chunk_masks(c)
    triu = jnp.where(_iota2((c, c), 0) <= _iota2((c, c), 1), 1.0, 0.0).astype(F32)
    tril = _block_tril(rows, c)
    da = dt * a_row
    acs = jnp.concatenate([_mm(tril, da[s * rows:(s + 1) * rows], exact=True)
                           for s in range(n_seq)], axis=0)
    acs_t = [_mm(t * a_col, triu, exact=True) for t in dt_t]
    expand = jnp.where((_iota2((SMALL_W, width), 1) // head_dim) == _iota2((SMALL_W, width), 0),
                       1.0, 0.0).astype(F32)
    wide = _mm(jnp.concatenate([acs, dt], axis=0), expand, exact=True)
    acs_w = wide[:n_seq * rows]
    xdt = xs * wide[n_seq * rows:]
    e_acs = jnp.exp(acs_w)

    hs = range(n_heads)
    gs = range(n_groups)
    js = range(n_chunks)
    b_g = {(j, g): b_all[rsl[j], g * d_state:(g + 1) * d_state] for j in js for g in gs}
    c_g = {(j, g): c_all[rsl[j], g * d_state:(g + 1) * d_state] for j in js for g in gs}
    a_last = {j: acs_w[(j + 1) * c - 1:(j + 1) * c] for j in js}
    xdt_end = {j: xdt[rsl[j]] * jnp.exp(a_last[j] - acs_w[rsl[j]]) for j in js}

    cb = {jg: _mm_nt(c_g[jg], b_g[jg]) for jg in b_g}
    seg = {(j, h): jnp.exp(jnp.where(incl, acs[rsl[j], h:h + 1] - acs_t[j][h:h + 1, :], -jnp.inf))
           for j in js for h in hs}
    y_diag = {(j, h): _mm(cb[(j, h // hpg)] * seg[(j, h)],
                          xdt[rsl[j], h * head_dim:(h + 1) * head_dim])
              for j in js for h in hs}
    st = {(j, g): _mm_tn(xdt_end[j][:, g * gw:(g + 1) * gw], b_g[(j, g)]) for j in js for g in gs}

    sg = [(s, g) for s in range(n_seq) for g in gs]
    states = {(s, g): h_ref[s, g * hpg:(g + 1) * hpg].reshape(gw, d_state) for s, g in sg}
    ys = {s: [] for s in range(n_seq)}
    for jj in range(n_sub):
        y_off = {(s, g): _mm_nt(c_g[(s * n_sub + jj, g)], states[(s, g)]) for s, g in sg}
        for s in range(n_seq):
            j = s * n_sub + jj
            ys[s].append(jnp.concatenate([y_off[(s, g)] for g in gs], axis=-1) * e_acs[rsl[j]]
                         + jnp.concatenate([y_diag[(j, h)] for h in hs], axis=-1))
            for g in gs:
                scaled = [states[(s, g)][i * head_dim:(i + 1) * head_dim]
                          * jnp.exp(acs[(j + 1) * c - 1:(j + 1) * c, g * hpg + i:g * hpg + i + 1])
                          for i in range(hpg)]
                states[(s, g)] = jnp.concatenate(scaled, axis=0) + st[(j, g)]
    for s, g in sg:
        h_ref[s, g * hpg:(g + 1) * hpg] = states[(s, g)].reshape(hpg, head_dim, d_state)
    y_all = jnp.concatenate([t for s in range(n_seq) for t in ys[s]], axis=0) + xs * d_ref[...]
    z = z_ref[...].reshape(n_seq * rows, width)
    o_ref[...] = _ssm_gated_norm(y_all, z, nw_ref[...], n_groups).reshape(n_seq, rows, width)


def _ssd_prompt(xbc, small, z, prm, *, batch, n_steps, n_sub, n_seq, n_heads, head_dim, n_groups,
                d_state, n_pad):
    t, conv_ch = xbc.shape
    width = n_heads * head_dim
    rows = n_sub * CHUNK
    fixed = lambda b, c: (0, 0)
    names = ("conv_w", "conv_b", "dtb_row", "dtb_col", "alog_row", "alog_col", "d", "norm_w")
    params = [prm[n] for n in names]
    views, in_specs = _seq_specs(n_seq, rows, n_steps, [xbc, small, z])
    in_specs += [pl.BlockSpec(a.shape, fixed) for a in params]
    out_specs = [pl.BlockSpec((n_seq, rows, width), lambda b, c: (b, c, 0)),
                 pl.BlockSpec((n_seq, n_heads, head_dim, d_state), lambda b, c: (b, 0, 0, 0)),
                 pl.BlockSpec((n_seq, CONV_W - 1, conv_ch), lambda b, c: (b, 0, 0))]
    out_shape = [jax.ShapeDtypeStruct((batch, n_steps * rows, width), F32),
                 jax.ShapeDtypeStruct((batch, n_heads, head_dim, d_state), F32),
                 jax.ShapeDtypeStruct((batch, CONV_W - 1, conv_ch), F32)]
    o, state, conv = pl.pallas_call(
        functools.partial(_ssd_prompt_body, n_heads=n_heads, head_dim=head_dim,
                          n_groups=n_groups, d_state=d_state, n_pad=n_pad, n_sub=n_sub),
        grid=(batch // n_seq, n_steps), in_specs=in_specs, out_specs=out_specs,
        out_shape=out_shape,
        scratch_shapes=[pltpu.VMEM((n_seq, rows + CONV_HALO, conv_ch), F32)],
        compiler_params=pltpu.CompilerParams(dimension_semantics=("arbitrary", "arbitrary"),
                                             vmem_limit_bytes=VMEM_LIMIT),
        name="ssd_prompt",
    )(*views, *params)
    return o.reshape(t, width), state, conv


def _gdn_post_head(o, z, norm_w):
    return o * lax.rsqrt(jnp.mean(o * o, axis=-1, keepdims=True) + GDN_NORM_EPS) * norm_w * _silu(z)


def _gdn_prompt_body(qkv_ref, small_ref, z_ref, cw_ref, dtb_row_ref, dtb_col_ref, alog_row_ref,
                     alog_col_ref, nw_ref, o_ref, s_ref, conv_ref, ext_scr,
                     *, n_heads, head_dim, a_off, b_off, n_sub):
    first = pl.program_id(1) == 0
    width = n_heads * head_dim
    n_seq, rows, _ = qkv_ref.shape
    c = rows // n_sub
    n_chunks = n_seq * n_sub

    @pl.when(first)
    def _():
        s_ref[...] = jnp.zeros(s_ref.shape, F32)

    convs = []
    for s in range(n_seq):
        conv, tail = _conv_chunk(ext_scr.at[s], qkv_ref[s], cw_ref, first)
        conv_ref[s] = tail
        convs.append(conv)
    qkv = _silu(jnp.concatenate(convs, axis=0))
    small = small_ref[...].reshape(n_seq * rows, SMALL_W)
    rsl = [slice(j * c, (j + 1) * c) for j in range(n_chunks)]
    small_t = [small[sl].T for sl in rsl]
    lane = _iota2((1, SMALL_W), 1)
    sub = _iota2((SMALL_W, 1), 0)
    g = jnp.where(jnp.logical_and(lane >= a_off, lane < a_off + n_heads),
                  -jnp.exp(alog_row_ref[...]) * _softplus(small + dtb_row_ref[...]), 0.0)
    g_t = [jnp.where(jnp.logical_and(sub >= a_off, sub < a_off + n_heads),
                     -jnp.exp(alog_col_ref[...]) * _softplus(st + dtb_col_ref[...]), 0.0)
           for st in small_t]
    beta = _sigmoid(small)
    incl, strict = _chunk_masks(c)
    triu = jnp.where(_iota2((c, c), 0) <= _iota2((c, c), 1), 1.0, 0.0).astype(F32)
    tril = _block_tril(rows, c)
    gcs = jnp.concatenate([_mm(tril, g[s * rows:(s + 1) * rows], exact=True)
                           for s in range(n_seq)], axis=0)
    gcs_t = [_mm(gt, triu, exact=True) for gt in g_t]
    z = z_ref[...].reshape(n_seq * rows, width)
    scale = head_dim ** -0.5

    hs = range(n_heads)
    sls = [slice(h * head_dim, (h + 1) * head_dim) for h in hs]
    units = [(j, h) for j in range(n_chunks) for h in hs]
    q_h = [_l2norm(qkv[:, sl]) * scale for sl in sls]
    k_h = [_l2norm(qkv[:, width + h * head_dim:width + (h + 1) * head_dim]) for h in hs]
    v_h = [qkv[:, 2 * width + h * head_dim:2 * width + (h + 1) * head_dim] for h in hs]
    g_c = {(j, h): gcs[rsl[j], a_off + h:a_off + h + 1] for j, h in units}
    g_r = {(j, h): gcs_t[j][a_off + h:a_off + h + 1, :] for j, h in units}
    g_last = {(j, h): gcs[(j + 1) * c - 1:(j + 1) * c, a_off + h:a_off + h + 1] for j, h in units}
    b_c = {(j, h): beta[rsl[j], b_off + h:b_off + h + 1] for j, h in units}
    q = {(j, h): q_h[h][rsl[j]] for j, h in units}
    k = {(j, h): k_h[h][rsl[j]] for j, h in units}
    v = {(j, h): v_h[h][rsl[j]] for j, h in units}

    decay = {u: jnp.exp(jnp.where(incl, g_c[u] - g_r[u], -jnp.inf)) for u in units}
    kb = {u: k[u] * b_c[u] for u in units}
    e_c = {u: jnp.exp(g_c[u]) for u in units}
    sc = {u: _mm_nt(jnp.concatenate([kb[u], q[u]], axis=0), k[u]) for u in units}
    a_mat = {u: jnp.where(strict, sc[u][:c] * decay[u], 0.0) for u in units}
    qk = {u: jnp.where(incl, sc[u][c:] * decay[u], 0.0) for u in units}
    inv = dict(zip(units, _unit_lower_inverse([-a_mat[u] for u in units])))
    rhs = {u: jnp.concatenate([v[u] * b_c[u], kb[u] * e_c[u]], axis=1) for u in units}
    uw = {u: rhs[u] + _mm(inv[u], rhs[u]) for u in units}
    qo = {u: _mm(qk[u], uw[u]) for u in units}
    mn = {u: _mm_tn(k[u] * jnp.exp(g_last[u] - g_c[u]), uw[u]) for u in units}

    sh = [(s, h) for s in range(n_seq) for h in hs]
    states = {(s, h): s_ref[s, h] for s, h in sh}
    os_ = {k_: [] for k_ in sh}
    for jj in range(n_sub):
        ju = {(s, h): (s * n_sub + jj, h) for s, h in sh}
        o_dot = {k_: _mm(q[ju[k_]] * e_c[ju[k_]] - qo[ju[k_]][:, head_dim:], states[k_])
                 for k_ in sh}
        s_dot = {k_: _mm(mn[ju[k_]][:, head_dim:], states[k_]) for k_ in sh}
        for k_ in sh:
            u = ju[k_]
            os_[k_].append(o_dot[k_] + qo[u][:, :head_dim])
            states[k_] = states[k_] * jnp.exp(g_last[u]) - s_dot[k_] + mn[u][:, :head_dim]
    for s, h in sh:
        s_ref[s, h] = states[(s, h)]
    outs = [_gdn_post_head(jnp.concatenate([t for s in range(n_seq) for t in os_[(s, h)]], axis=0),
                           z[:, sls[h]], nw_ref[...]) for h in hs]
    o_ref[...] = jnp.concatenate(outs, axis=-1).reshape(n_seq, rows, width)


def _gdn_prompt(qkv, small, z, prm, *, batch, n_steps, n_sub, n_seq, n_heads, head_dim, a_off,
                b_off):
    t, conv_ch = qkv.shape
    width = n_heads * head_dim
    rows = n_sub * CHUNK
    fixed = lambda b, c: (0, 0)
    names = ("conv_w", "dtb_row", "dtb_col", "alog_row", "alog_col", "norm_w")
    params = [prm[n] for n in names]
    views, in_specs = _seq_specs(n_seq, rows, n_steps, [qkv, small, z])
    in_specs += [pl.BlockSpec(a.shape, fixed) for a in params]
    out_specs = [pl.BlockSpec((n_seq, rows, width), lambda b, c: (b, c, 0)),
                 pl.BlockSpec((n_seq, n_heads, head_dim, head_dim), lambda b, c: (b, 0, 0, 0)),
                 pl.BlockSpec((n_seq, CONV_W - 1, conv_ch), lambda b, c: (b, 0, 0))]
    out_shape = [jax.ShapeDtypeStruct((batch, n_steps * rows, width), F32),
                 jax.ShapeDtypeStruct((batch, n_heads, head_dim, head_dim), F32),
                 jax.ShapeDtypeStruct((batch, CONV_W - 1, conv_ch), F32)]
    o, state, conv = pl.pallas_call(
        functools.partial(_gdn_prompt_body, n_heads=n_heads, head_dim=head_dim,
                          a_off=a_off, b_off=b_off, n_sub=n_sub),
        grid=(batch // n_seq, n_steps), in_specs=in_specs, out_specs=out_specs,
        out_shape=out_shape,
        scratch_shapes=[pltpu.VMEM((n_seq, rows + CONV_HALO, conv_ch), F32)],
        compiler_params=pltpu.CompilerParams(dimension_semantics=("arbitrary", "arbitrary"),
                                             vmem_limit_bytes=VMEM_LIMIT),
        name="gdn_prompt",
    )(*views, *params)
    return o.reshape(t, width), state, conv


def _conv_step(u, conv_ref, convo_ref, w_ref):
    out = u * w_ref[CONV_W - 1:CONV_W, :]
    for i in range(CONV_W - 1):
        out = out + conv_ref[i] * w_ref[i:i + 1, :]
    for i in range(CONV_W - 2):
        convo_ref[i] = conv_ref[i + 1]
    convo_ref[CONV_W - 2] = u
    return out


def _lane_pick(x, lane, idx):
    return jnp.sum(jnp.where(lane == idx, x, 0.0), axis=1, keepdims=True)


def _rwkv_decode_body(p_ref, z_ref, prev_ref, s_ref, mu_ref, w0_ref, w2_ref, a0_ref, a2_ref,
                      kk_ref, ka_ref, rk_ref, gnw_ref, gnb_ref, o_ref, so_ref,
                      t_scr, y_scr, row_scr, *, n_heads, head_dim):
    h = pl.program_id(0)
    width = n_heads * head_dim
    lora = w2_ref.shape[0]
    bsz = p_ref.shape[0]

    @pl.when(h == 0)
    def _():
        r, k_mod, v, log_decay, a, kk_raw = _rwkv_pre(
            p_ref[...], prev_ref[...], mu_ref[...], w0_ref[...], w2_ref[...], a0_ref[...],
            a2_ref[...], kk_ref[...], ka_ref[...], width, lora)
        kk = jnp.concatenate(
            [_l2norm(kk_raw[:, i * head_dim:(i + 1) * head_dim]) for i in range(n_heads)], axis=-1)
        row_scr[0] = r
        row_scr[1] = k_mod
        row_scr[2] = v
        for i, arr in enumerate((r, jnp.exp(log_decay), k_mod, v, -kk, kk * a)):
            t_scr[i] = arr.T.reshape(n_heads, head_dim, bsz)

    r_t = t_scr[0, h]
    w_t = t_scr[1, h]
    k_t = t_scr[2, h]
    a_t = t_scr[4, h]
    b_t = t_scr[5, h]

    def step(i, carry):
        s_v = s_ref[i]
        sa = jnp.sum(s_v * a_t, axis=0, keepdims=True)
        v_i = t_scr[3, h, pl.ds(i, 1), :]
        s_new = s_v * w_t + sa * b_t + v_i * k_t
        so_ref[i] = s_new
        y_scr[h, pl.ds(i, 1), :] = jnp.sum(s_new * r_t, axis=0, keepdims=True)
        return carry

    lax.fori_loop(0, head_dim, step, 0, unroll=DECODE_UNROLL)

    @pl.when(h == n_heads - 1)
    def _():
        y = y_scr[...].reshape(width, bsz).T
        r, k_mod, v, z = row_scr[0], row_scr[1], row_scr[2], z_ref[...]
        outs = []
        for i in range(n_heads):
            sl = slice(i * head_dim, (i + 1) * head_dim)
            outs.append(_rwkv_post_head(y[:, sl], r[:, sl], k_mod[:, sl], v[:, sl], z[:, sl],
                                        rk_ref[:, sl], gnw_ref[:, sl], gnb_ref[:, sl]))
        o_ref[...] = jnp.concatenate(outs, axis=-1)


def _layer_state_call(body, name, row_inputs, layer_inputs, state_all, new_all, params, *,
                      layer, n_row_out, scratch_shapes, batch_minor=False, tap_inputs=()):
    bsz = row_inputs[0].shape[0]
    fixed = lambda h: (0, 0)
    at_layer = lambda h: (layer, 0)
    if batch_minor:
        n_heads = state_all.shape[1]
        state_spec = pl.BlockSpec((None, None) + state_all.shape[2:],
                                  lambda h: (layer, h, 0, 0, 0))
    else:
        n_heads = state_all.shape[2]
        state_spec = pl.BlockSpec((None, bsz, None) + state_all.shape[3:],
                                  lambda h: (layer, 0, h, 0, 0))
    in_specs = [pl.BlockSpec(a.shape, fixed) for a in row_inputs]
    in_specs += [pl.BlockSpec((bsz, a.shape[1]), at_layer) for a in layer_inputs]
    in_specs += [pl.BlockSpec((None,) + a.shape[1:], lambda h: (layer, 0, 0, 0))
                 for a in tap_inputs]
    in_specs += [state_spec]
    in_specs += [pl.BlockSpec(a.shape, fixed) for a in params]
    in_specs.append(pl.BlockSpec(memory_space=pl.ANY))
    operands = (list(row_inputs) + list(layer_inputs) + list(tap_inputs) + [state_all]
                + list(params) + [new_all])
    n_in = len(operands)
    aliases = {n_in - 1: len(n_row_out) + len(tap_inputs)}
    inner = body

    def body(*refs):
        return inner(*refs[:n_in - 1], *refs[n_in:])

    out_specs = [pl.BlockSpec((bsz, w), fixed) for w in n_row_out]
    out_specs += [pl.BlockSpec(a.shape[1:], lambda h: (0, 0, 0)) for a in tap_inputs]
    out_specs += [state_spec]
    out_shape = [jax.ShapeDtypeStruct((bsz, w), F32) for w in n_row_out]
    out_shape += [jax.ShapeDtypeStruct(a.shape[1:], F32) for a in tap_inputs]
    out_shape += [jax.ShapeDtypeStruct(state_all.shape, F32)]
    return pl.pallas_call(
        body, grid=(n_heads,), in_specs=in_specs, out_specs=out_specs, out_shape=out_shape,
        scratch_shapes=scratch_shapes, input_output_aliases=aliases,
        compiler_params=pltpu.CompilerParams(dimension_semantics=("arbitrary",),
                                             vmem_limit_bytes=VMEM_LIMIT),
        name=name,
    )(*operands)


def _rwkv_decode(p_rw, z, prev_all, state_all, new_all, prm, *, layer):
    bsz = p_rw.shape[0]
    n_heads, head_dim = state_all.shape[1], state_all.shape[2]
    width = n_heads * head_dim
    names = ("mu", "w0", "w2", "a0", "a2", "k_k", "k_a", "r_k", "gn_w", "gn_b")
    return _layer_state_call(
        functools.partial(_rwkv_decode_body, n_heads=n_heads, head_dim=head_dim), "rwkv_decode",
        [p_rw, z], [prev_all], state_all, new_all, [prm[n] for n in names],
        layer=layer, n_row_out=[width], batch_minor=True,
        scratch_shapes=[pltpu.VMEM((6, n_heads, head_dim, bsz), F32),
                        pltpu.VMEM((n_heads, head_dim, bsz), F32),
                        pltpu.VMEM((3, bsz, width), F32)])


def _ssd_decode_body(xbc_ref, small_ref, z_ref, conv_ref, s_ref, cw_ref, cb_ref, dtb_row_ref,
                     alog_row_ref, d_ref, nw_ref, o_ref, convo_ref, so_ref,
                     xt_scr, bc_scr, dec_scr, y_scr, skip_scr,
                     *, n_heads, head_dim, n_groups, d_state):
    h = pl.program_id(0)
    width = n_heads * head_dim
    bsz = xbc_ref.shape[0]
    hpg = n_heads // n_groups

    @pl.when(h == 0)
    def _():
        conv = _conv_step(xbc_ref[...], conv_ref, convo_ref, cw_ref)
        xbc = _silu(conv + cb_ref[...])
        lane_s = _iota2((1, SMALL_W), 1)
        dt = _softplus(small_ref[...] + dtb_row_ref[...])
        a_row = jnp.where(lane_s < n_heads, -jnp.exp(alog_row_ref[...]), 0.0)
        dec_scr[...] = jnp.exp(dt * a_row).T
        expand = jnp.where((_iota2((SMALL_W, width), 1) // head_dim) == _iota2((SMALL_W, width), 0),
                           1.0, 0.0).astype(F32)
        xs = xbc[:, :width]
        xt_scr[...] = (xs * _mm(dt, expand, exact=True)).T.reshape(n_heads, head_dim, bsz)
        skip_scr[...] = xs * d_ref[...]
        for i in range(2 * n_groups):
            bc_scr[i] = xbc[:, width + i * d_state:width + (i + 1) * d_state]

    g = h // hpg
    xdt_t = xt_scr[h]
    dec_row = dec_scr[pl.ds(h, 1), :]
    lane = _iota2((head_dim, bsz), 1)
    lane1 = _iota2((1, bsz), 1)

    def step(b, y_t):
        b_row = bc_scr[g, pl.ds(b, 1), :]
        c_row = bc_scr[n_groups + g, pl.ds(b, 1), :]
        s_new = (s_ref[b] * _lane_pick(dec_row, lane1, b)
                 + _lane_pick(xdt_t, lane, b) * b_row)
        so_ref[b] = s_new
        y_col = jnp.sum(s_new * c_row, axis=1, keepdims=True)
        return jnp.where(lane == b, y_col, y_t)

    y_scr[h] = lax.fori_loop(0, bsz, step, jnp.zeros((head_dim, bsz), F32), unroll=DECODE_UNROLL)

    @pl.when(h == n_heads - 1)
    def _():
        y = y_scr[...].reshape(width, bsz).T + skip_scr[...]
        o_ref[...] = _ssm_gated_norm(y, z_ref[...], nw_ref[...], n_groups)


def _ssd_decode(xbc, small, z, conv_all, state_all, new_all, prm, *, layer, n_groups):
    bsz = xbc.shape[0]
    n_heads, head_dim, d_state = state_all.shape[2:]
    width = n_heads * head_dim
    names = ("conv_w", "conv_b", "dtb_row", "alog_row", "d", "norm_w")
    return _layer_state_call(
        functools.partial(_ssd_decode_body, n_heads=n_heads, head_dim=head_dim,
                          n_groups=n_groups, d_state=d_state), "ssd_decode",
        [xbc, small, z], [], state_all, new_all, [prm[n] for n in names],
        layer=layer, n_row_out=[width], tap_inputs=[conv_all],
        scratch_shapes=[pltpu.VMEM((n_heads, head_dim, bsz), F32),
                        pltpu.VMEM((2 * n_groups, bsz, d_state), F32),
                        pltpu.VMEM((SMALL_W, bsz), F32),
                        pltpu.VMEM((n_heads, head_dim, bsz), F32),
                        pltpu.VMEM((bsz, width), F32)])


def _gdn_decode_body(qkv_ref, small_ref, z_ref, conv_ref, s_ref, cw_ref, dtb_row_ref,
                     alog_row_ref, nw_ref, o_ref, convo_ref, so_ref,
                     t_scr, v_scr, gate_scr, y_scr, *, n_heads, head_dim, a_off, b_off):
    h = pl.program_id(0)
    width = n_heads * head_dim
    bsz = qkv_ref.shape[0]

    @pl.when(h == 0)
    def _():
        qkv = _silu(_conv_step(qkv_ref[...], conv_ref, convo_ref, cw_ref))
        scale = head_dim ** -0.5
        q = jnp.concatenate([_l2norm(qkv[:, i * head_dim:(i + 1) * head_dim]) * scale
                             for i in range(n_heads)], axis=-1)
        k = jnp.concatenate([_l2norm(qkv[:, width + i * head_dim:width + (i + 1) * head_dim])
                             for i in range(n_heads)], axis=-1)
        for i, arr in enumerate((q, k)):
            t_scr[i] = arr.T.reshape(n_heads, head_dim, bsz)
        for i in range(n_heads):
            v_scr[i] = qkv[:, 2 * width + i * head_dim:2 * width + (i + 1) * head_dim]
        small = small_ref[...]
        lane_s = _iota2((1, SMALL_W), 1)
        g = jnp.where(jnp.logical_and(lane_s >= a_off, lane_s < a_off + n_heads),
                      -jnp.exp(alog_row_ref[...]) * _softplus(small + dtb_row_ref[...]), 0.0)
        gate_scr[0] = jnp.exp(g).T
        gate_scr[1] = _sigmoid(small).T
        gate_scr[2, 0:n_heads, :] = jnp.concatenate(
            [jnp.sum(q[:, i * head_dim:(i + 1) * head_dim] * k[:, i * head_dim:(i + 1) * head_dim],
                     axis=1, keepdims=True) for i in range(n_heads)], axis=1).T

    q_t = t_scr[0, h]
    k_t = t_scr[1, h]
    eg_row = gate_scr[0, pl.ds(a_off + h, 1), :]
    beta_row = gate_scr[1, pl.ds(b_off + h, 1), :]
    qk_row = gate_scr[2, pl.ds(h, 1), :]
    lane = _iota2((head_dim, bsz), 1)
    lane1 = _iota2((1, bsz), 1)

    def step(b, carry):
        s_b = s_ref[b]
        k_col = _lane_pick(k_t, lane, b)
        q_col = _lane_pick(q_t, lane, b)
        eg = _lane_pick(eg_row, lane1, b)
        k_s = jnp.sum(s_b * k_col, axis=0, keepdims=True)
        q_s = jnp.sum(s_b * q_col, axis=0, keepdims=True)
        v_new = _lane_pick(beta_row, lane1, b) * (v_scr[h, pl.ds(b, 1), :] - eg * k_s)
        y_scr[h, pl.ds(b, 1), :] = eg * q_s + _lane_pick(qk_row, lane1, b) * v_new
        so_ref[b] = s_b * eg + k_col * v_new
        return carry

    lax.fori_loop(0, bsz, step, 0, unroll=DECODE_UNROLL)

    @pl.when(h == n_heads - 1)
    def _():
        z = z_ref[...]
        outs = []
        for i in range(n_heads):
            sl = slice(i * head_dim, (i + 1) * head_dim)
            outs.append(_gdn_post_head(y_scr[i], z[:, sl], nw_ref[...]))
        o_ref[...] = jnp.concatenate(outs, axis=-1)


def _gdn_decode(qkv, small, z, conv_all, state_all, new_all, prm, *, layer, a_off, b_off):
    bsz = qkv.shape[0]
    n_heads, head_dim = state_all.shape[2], state_all.shape[3]
    width = n_heads * head_dim
    names = ("conv_w", "dtb_row", "alog_row", "norm_w")
    return _layer_state_call(
        functools.partial(_gdn_decode_body, n_heads=n_heads, head_dim=head_dim,
                          a_off=a_off, b_off=b_off), "gdn_decode",
        [qkv, small, z], [], state_all, new_all, [prm[n] for n in names],
        layer=layer, n_row_out=[width], tap_inputs=[conv_all],
        scratch_shapes=[pltpu.VMEM((2, n_heads, head_dim, bsz), F32),
                        pltpu.VMEM((n_heads, bsz, head_dim), F32),
                        pltpu.VMEM((3, SMALL_W, bsz), F32),
                        pltpu.VMEM((n_heads, bsz, head_dim), F32)])


def _segment_ones(width, seg):
    i = jnp.arange(2 * width)[:, None] % width
    j = jnp.arange(width)[None, :]
    return ((i // seg) == (j // seg)).astype(BF16)


def _pad_lanes(row, offset):
    out = jnp.zeros((SMALL_W,), F32)
    return lax.dynamic_update_slice(out, row.astype(F32), (offset,)).reshape(1, SMALL_W)


def _row_tile(seq_len, cap):
    best = 8
    for tm in range(8, cap + 1, 8):
        if seq_len % tm == 0:
            best = tm
    return best


def kernel(x_prompt, x_sample, state_rwkv_wkv, state_rwkv_shift, state_ssm, state_ssm_conv,
           state_gdn, state_gdn_conv, meta_tokens, norm_w, w_in, rw_mu, rw_w0, rw_w2, rw_a0,
           rw_a2, rw_k_k, rw_k_a, rw_r_k, rw_gn_w, rw_gn_b, ssm_conv_w, ssm_conv_b,
           ssm_dt_bias, ssm_a_log, ssm_d, ssm_norm_w, gdn_conv_w, gdn_dt_bias, gdn_a_log,
           gdn_norm_w, w_rw_out, w_ssm_out, w_gdn_out, w_out, final_norm_w):
    depth = w_in.shape[0]
    bp, seq, d_model = x_prompt.shape
    bd = x_sample.shape[0]
    n_meta = meta_tokens.shape[0]
    rw_heads, rw_hd = rw_r_k.shape[1], rw_r_k.shape[2]
    rw_w = rw_heads * rw_hd
    rw_shift_w = rw_mu.shape[1]
    ssm_heads, ssm_hd, ssm_state = state_ssm.shape[2], state_ssm.shape[3], state_ssm.shape[4]
    ssm_w = ssm_heads * ssm_hd
    ssm_conv_ch = ssm_conv_w.shape[2]
    ssm_groups = (ssm_conv_ch - ssm_w) // (2 * ssm_state)
    gdn_heads, gdn_hd = state_gdn.shape[2], state_gdn.shape[3]
    gdn_w = gdn_heads * gdn_hd
    gdn_conv_ch = gdn_conv_w.shape[2]
    a_off, b_off = ssm_heads, ssm_heads + gdn_heads
    assert b_off + gdn_heads <= SMALL_W

    off = 0
    cols = {}
    for name, wdt in (("rw", rw_shift_w), ("rw_z", rw_w), ("ssm_z", ssm_w), ("xbc", ssm_conv_ch),
                      ("dt", ssm_heads), ("qkv", gdn_conv_ch), ("gdn_z", gdn_w),
                      ("ga", gdn_heads), ("gb", gdn_heads), ("gate", 3 * d_model)):
        cols[name] = (off, off + wdt)
        off += wdt
    assert off == w_in.shape[2]

    seg_names = ("rw", "rw_z", "ssm_z", "xbc", "qkv", "gdn_z", "gate", "dt", "ga", "gb")
    n_small = ssm_heads + 2 * gdn_heads
    w_all = _pack_weights(jnp.swapaxes(w_in, 1, 2), [cols[n] for n in seg_names],
                          SMALL_W - n_small, kb=256)
    widths = [cols[n][1] - cols[n][0] for n in seg_names[:7]] + [SMALL_W]
    assert all(w % 128 == 0 for w in widths)
    w_r, w_s, w_g, w_o = (w.astype(BF16) for w in (w_rw_out, w_ssm_out, w_gdn_out, w_out))

    def layer_params(l):
        rw = dict(mu=rw_mu[l][None], w0=rw_w0[l][None], w2=rw_w2[l], a0=rw_a0[l][None],
                  a2=rw_a2[l], k_k=rw_k_k[l][None], k_a=rw_k_a[l][None],
                  r_k=rw_r_k[l].reshape(1, rw_w), gn_w=rw_gn_w[l][None], gn_b=rw_gn_b[l][None])
        dtb = _pad_lanes(ssm_dt_bias[l], 0)
        alog = _pad_lanes(ssm_a_log[l], 0)
        sm = dict(conv_w=ssm_conv_w[l], conv_b=ssm_conv_b[l][None], dtb_row=dtb,
                  dtb_col=dtb.reshape(SMALL_W, 1), alog_row=alog,
                  alog_col=alog.reshape(SMALL_W, 1),
                  d=jnp.repeat(ssm_d[l], ssm_hd)[None], norm_w=ssm_norm_w[l][None])
        gdtb = _pad_lanes(gdn_dt_bias[l], a_off)
        galog = _pad_lanes(gdn_a_log[l], a_off)
        gd = dict(conv_w=gdn_conv_w[l], dtb_row=gdtb, dtb_col=gdtb.reshape(SMALL_W, 1),
                  alog_row=galog, alog_col=galog.reshape(SMALL_W, 1),
                  norm_w=gdn_norm_w[l][None])
        return rw, sm, gd

    fw = final_norm_w[None]

    real_len = n_meta + seq
    seq_p = -(-real_len // CHUNK) * CHUNK
    n_pad = seq_p - real_len
    n_chunks = seq_p // CHUNK
    n_sub = max(d for d in range(1, MAX_CHUNKS_PER_STEP + 1) if n_chunks % d == 0)
    n_steps = n_chunks // n_sub
    n_seq = max(d for d in range(1, MAX_SEQS_PER_STEP + 1) if bp % d == 0)
    meta = jnp.broadcast_to(meta_tokens.astype(F32)[None], (bp, n_meta, d_model))
    xp = jnp.concatenate([jnp.zeros((bp, n_pad, d_model), F32), meta, x_prompt], axis=1)
    xp = xp.reshape(bp * seq_p, d_model)
    tm_in = _row_tile(seq_p, 384)
    tm_out = _row_tile(seq_p, 384)

    p_states = [[] for _ in range(6)]
    y_p = None
    for l in range(depth):
        rw_p, sm_p, gd_p = layer_params(l)
        p_rw, rw_z, ssm_z, xbc, qkv, gdn_z, gate, small = _inproj(
            xp, norm_w[l][None], w_all, widths, layer=l, tm=tm_in, n_pad=n_pad, seq_len=seq_p)
        rw_o, wkv, shift = _rwkv_prompt(p_rw, rw_z, rw_p, batch=bp, n_steps=n_steps,
                                        n_sub=n_sub, n_seq=n_seq, n_heads=rw_heads,
                                        head_dim=rw_hd)
        sm_o, ssm, sconv = _ssd_prompt(xbc, small, ssm_z, sm_p, batch=bp, n_steps=n_steps,
                                       n_sub=n_sub, n_seq=n_seq, n_heads=ssm_heads,
                                       head_dim=ssm_hd, n_groups=ssm_groups, d_state=ssm_state,
                                       n_pad=n_pad)
        gd_o, gdn, gconv = _gdn_prompt(qkv, small, gdn_z, gd_p, batch=bp, n_steps=n_steps,
                                       n_sub=n_sub, n_seq=n_seq, n_heads=gdn_heads,
                                       head_dim=gdn_hd, a_off=a_off, b_off=b_off)
        last = l == depth - 1
        res = _outproj(rw_o, sm_o, gd_o, gate, xp, w_r[l], w_s[l], w_g[l], w_o[l], fw,
                       tm=tm_out, with_final=last)
        xp = res[0]
        if last:
            y_p = res[1]
        for acc, s in zip(p_states, (wkv, shift.reshape(bp, rw_shift_w), ssm, sconv, gdn, gconv)):
            acc.append(s)
    y_prompt = y_p.reshape(bp, seq_p, d_model)[:, n_pad + n_meta:]

    xs = x_sample.reshape(bd, d_model)
    shift_all = state_rwkv_shift.reshape(depth * bd, -1)
    sconv_all = jnp.swapaxes(state_ssm_conv, 1, 2)
    gconv_all = jnp.swapaxes(state_gdn_conv, 1, 2)
    wkv_all = jnp.transpose(state_rwkv_wkv, (0, 2, 3, 4, 1))
    wkv_new, ssm_new, gdn_new = (jnp.zeros(a.shape, F32) for a in (wkv_all, state_ssm, state_gdn))
    small_states = [[] for _ in range(3)]
    y_s = None
    for l in range(depth):
        rw_p, sm_p, gd_p = layer_params(l)
        p_rw, rw_z, ssm_z, xbc, qkv, gdn_z, gate, small = _inproj(
            xs, norm_w[l][None], w_all, widths, layer=l, tm=bd, n_pad=0, seq_len=bd)
        rw_o, wkv_new = _rwkv_decode(p_rw, rw_z, shift_all, wkv_all, wkv_new, rw_p, layer=l)
        sm_o, sconv, ssm_new = _ssd_decode(xbc, small, ssm_z, sconv_all, state_ssm, ssm_new, sm_p,
                                           layer=l, n_groups=ssm_groups)
        gd_o, gconv, gdn_new = _gdn_decode(qkv, small, gdn_z, gconv_all, state_gdn, gdn_new, gd_p,
                                           layer=l, a_off=a_off, b_off=b_off)
        last = l == depth - 1
        res = _outproj(rw_o, sm_o, gd_o, gate, xs, w_r[l], w_s[l], w_g[l], w_o[l], fw,
                       tm=bd, with_final=last)
        xs = res[0]
        if last:
            y_s = res[1]
        for acc, s in zip(small_states, (p_rw, sconv, gconv)):
            acc.append(s)
    y_sample = y_s.reshape(bd, 1, d_model)
    s_shift, s_sconv, s_gconv = (jnp.stack(a) for a in small_states)
    s_states = (jnp.transpose(wkv_new, (0, 4, 1, 2, 3)), s_shift, ssm_new,
                jnp.swapaxes(s_sconv, 1, 2), gdn_new, jnp.swapaxes(s_gconv, 1, 2))

    return (y_prompt, y_sample) + tuple(jnp.stack(a) for a in p_states) + s_states
```

```python
import functools
import math

import jax
import jax.numpy as jnp
from jax import lax
from jax.experimental import pallas as pl
from jax.experimental.pallas import tpu as pltpu

F32 = jnp.float32
BF16 = jnp.bfloat16
HIGHEST = lax.Precision.HIGHEST

NORM_EPS = 1e-6
RW_GN_EPS = 64e-5
SSM_NORM_EPS = 1e-5
GDN_NORM_EPS = 1e-6
L2_EPS = 1e-6
CONV_W = 4
LANE = 128
SUBLANE = 8
CHUNK = 64
MAX_CHUNKS_PER_STEP = 3
RWKV_SEQS_PER_STEP = 2
SSD_SEQS_PER_STEP = 2
GDN_SEQS_PER_STEP = 4
ROW_TILE_CAP = 384
PACK_KB = 256
INV_BASE = 8
SEG_TILE = 256
DECODE_UNROLL = 8
SMALL_W = 128
CONV_HALO = 8
VMEM_LIMIT = 56 * 1024 * 1024


def _mm(a, b, exact=False):
    if exact:
        return jnp.dot(a, b, preferred_element_type=F32, precision=HIGHEST)
    return jnp.dot(a.astype(BF16), b.astype(BF16), preferred_element_type=F32)


def _mm_nt(a, b, exact=False):
    dn = (((1,), (1,)), ((), ()))
    if exact:
        return lax.dot_general(a, b, dn, preferred_element_type=F32, precision=HIGHEST)
    return lax.dot_general(a.astype(BF16), b.astype(BF16), dn, preferred_element_type=F32)


def _mm_tn(a, b, exact=False):
    dn = (((0,), (0,)), ((), ()))
    if exact:
        return lax.dot_general(a, b, dn, preferred_element_type=F32, precision=HIGHEST)
    return lax.dot_general(a.astype(BF16), b.astype(BF16), dn, preferred_element_type=F32)


def _sigmoid(x):
    return 1.0 / (1.0 + jnp.exp(-x))


def _silu(x):
    return x * _sigmoid(x)


def _softplus(x):
    return jnp.maximum(x, 0.0) + jnp.log(1.0 + jnp.exp(-jnp.abs(x)))


def _iota2(shape, dim):
    return lax.broadcasted_iota(jnp.int32, shape, dim)


def _seg_sum(x, seg_ones, split=False):
    w = seg_ones.shape[1]
    parts = []
    for i in range(x.shape[1] // w):
        xi = x[:, i * w:(i + 1) * w]
        x_hi = xi.astype(BF16)
        if split:
            x_lo = (xi - x_hi.astype(F32)).astype(BF16)
            parts.append(jnp.dot(jnp.concatenate([x_hi, x_lo], axis=1), seg_ones,
                                 preferred_element_type=F32))
        else:
            parts.append(jnp.dot(x_hi, seg_ones[:w], preferred_element_type=F32))
    return jnp.concatenate(parts, axis=1)


def _block_diag_pair(p):
    c = p.shape[0]
    lane = _iota2(p.shape, 1)
    pb = p.astype(BF16)
    zero = jnp.zeros_like(pb)
    return jnp.concatenate([jnp.where(lane < c, pb, zero), jnp.where(lane >= c, pb, zero)], axis=0)


def _mm_pair(x, p):
    return jnp.dot(x.astype(BF16), _block_diag_pair(p), preferred_element_type=F32)


def _unit_lower_inverse(l_mats):
    c = l_mats[0].shape[0]
    pairs = [jnp.concatenate(l_mats[i:i + 2], axis=1) for i in range(0, len(l_mats), 2)]
    ri = _iota2((c, 2 * c), 0)
    ci = _iota2((c, 2 * c), 1) & (c - 1)
    base_shift = int(math.log2(INV_BASE))
    same_base = (ri >> base_shift) == (ci >> base_shift)
    es = [jnp.where(same_base, l, 0.0) for l in pairs]
    ps = es
    n_terms = 1
    while 2 * n_terms < INV_BASE:
        ps = [_mm_pair(p, p) for p in ps]
        eps = [_mm_pair(e, p) for e, p in zip(es, ps)]
        es = [e + p + ep for e, p, ep in zip(es, ps, eps)]
        n_terms *= 2
    shift = base_shift
    while (1 << shift) < c:
        same_outer = (ri >> (shift + 1)) == (ci >> (shift + 1))
        other_inner = (ri >> shift) != (ci >> shift)
        l_off = [jnp.where(same_outer, jnp.where(other_inner, l, 0.0), 0.0) for l in pairs]
        ts = [lo + _mm_pair(lo, e) for lo, e in zip(l_off, es)]
        ets = [_mm_pair(e, t) for e, t in zip(es, ts)]
        es = [e + t + et for e, t, et in zip(es, ts, ets)]
        shift += 1
    out = []
    for e in es:
        out += [e[:, :c], e[:, c:]]
    return out


def _chunk_masks(c):
    ri = _iota2((c, c), 0)
    ci = _iota2((c, c), 1)
    return ri >= ci, ri > ci


def _block_tril(rows, c):
    shift = int(math.log2(c))
    ri = _iota2((rows, rows), 0)
    ci = _iota2((rows, rows), 1)
    return jnp.where((ri >> shift) == (ci >> shift),
                     jnp.where(ri >= ci, 1.0, 0.0), 0.0).astype(F32)


def _conv_chunk(ext_ref, u, w_ref, first_chunk):
    c = u.shape[0]

    @pl.when(first_chunk)
    def _():
        ext_ref[0:CONV_HALO, :] = jnp.zeros((CONV_HALO, u.shape[1]), F32)

    ext_ref[CONV_HALO:CONV_HALO + c, :] = u
    out = u * w_ref[CONV_W - 1:CONV_W, :]
    for i in range(CONV_W - 1):
        off = CONV_HALO - (CONV_W - 1) + i
        out = out + ext_ref[off:off + c, :] * w_ref[i:i + 1, :]
    tail = ext_ref[c:c + CONV_HALO, :]
    ext_ref[0:CONV_HALO, :] = tail
    return out, tail[CONV_HALO - (CONV_W - 1):, :]


def _inproj_body(x_ref, nw_ref, w_ref, *o_refs, n_pad, tiles_per_seq):
    x = x_ref[...]
    h = x * lax.rsqrt(jnp.mean(x * x, axis=-1, keepdims=True) + NORM_EPS) * nw_ref[...]
    if n_pad:
        first = lax.rem(pl.program_id(0), tiles_per_seq) == 0
        row = _iota2((x.shape[0], 1), 0)
        h = jnp.where(jnp.logical_and(first, row < n_pad), 0.0, h)
    hb = h.astype(BF16)
    off = 0
    for o_ref in o_refs:
        width = o_ref.shape[1]
        o_ref[...] = jnp.dot(hb, w_ref[:, off:off + width], preferred_element_type=F32)
        off += width


def _inproj(x, norm_w, w_all, widths, *, layer, tm, n_pad, seq_len):
    t, d = x.shape
    assert w_all.shape[2] == sum(widths)
    in_specs = [pl.BlockSpec((tm, d), lambda i: (i, 0)),
                pl.BlockSpec((1, d), lambda i: (0, 0)),
                pl.BlockSpec((None,) + w_all.shape[1:], lambda i: (layer, 0, 0),
                             pipeline_mode=pl.Buffered(1))]
    out_specs = [pl.BlockSpec((tm, w), lambda i: (i, 0)) for w in widths]
    out_shape = [jax.ShapeDtypeStruct((t, w), F32) for w in widths]
    body = functools.partial(_inproj_body, n_pad=n_pad,
                             tiles_per_seq=(seq_len // tm) if n_pad else 1)
    return pl.pallas_call(
        body, grid=(t // tm,), in_specs=in_specs, out_specs=out_specs, out_shape=out_shape,
        compiler_params=pltpu.CompilerParams(dimension_semantics=("arbitrary",),
                                             vmem_limit_bytes=VMEM_LIMIT),
        name="inproj",
    )(x, norm_w, w_all)


def _pack_weights_body(w_ref, o_ref, *, segments, pad):
    off = 0
    tail = []
    for lo, hi in segments:
        if (hi - lo) % LANE == 0 and not tail:
            o_ref[:, off:off + hi - lo] = w_ref[lo:hi, :].T.astype(o_ref.dtype)
            off += hi - lo
        else:
            tail.append(w_ref[lo:hi, :])
    if tail or pad:
        if pad:
            tail.append(jnp.zeros((pad, w_ref.shape[1]), w_ref.dtype))
        o_ref[:, off:] = jnp.concatenate(tail, axis=0).T.astype(o_ref.dtype)


def _pack_weights(w_t, segments, pad, *, kb):
    depth, n, d = w_t.shape
    total = sum(hi - lo for lo, hi in segments) + pad
    return pl.pallas_call(
        functools.partial(_pack_weights_body, segments=tuple(segments), pad=pad),
        grid=(depth, d // kb),
        in_specs=[pl.BlockSpec((None, n, kb), lambda l, i: (l, 0, i))],
        out_specs=pl.BlockSpec((None, kb, total), lambda l, i: (l, i, 0)),
        out_shape=jax.ShapeDtypeStruct((depth, d, total), BF16),
        compiler_params=pltpu.CompilerParams(dimension_semantics=("arbitrary", "arbitrary"),
                                             vmem_limit_bytes=VMEM_LIMIT),
        name="pack_weights",
    )(w_t)


def _outproj_body(rw_ref, sm_ref, gd_ref, gate_ref, x_ref, wr_ref, ws_ref, wg_ref, wo_ref,
                  fw_ref, o_ref, *maybe_y_ref, d_model):
    merged = None
    for b, (y_ref, w_ref) in enumerate(((rw_ref, wr_ref), (sm_ref, ws_ref), (gd_ref, wg_ref))):
        g = _sigmoid(gate_ref[:, b * d_model:(b + 1) * d_model])
        term = g * jnp.dot(y_ref[...].astype(BF16), w_ref[...], preferred_element_type=F32)
        merged = term if merged is None else merged + term
    x_new = x_ref[...] + jnp.dot(merged.astype(BF16), wo_ref[...], preferred_element_type=F32)
    o_ref[...] = x_new
    if maybe_y_ref:
        ms = jnp.mean(x_new * x_new, axis=-1, keepdims=True)
        maybe_y_ref[0][...] = x_new * lax.rsqrt(ms + NORM_EPS) * fw_ref[...]


def _outproj(rw, sm, gd, gate, x, w_r, w_s, w_g, w_o, final_w, *, tm, with_final):
    t, d = x.shape
    grid = (t // tm,)
    row = lambda i: (i, 0)
    fixed = lambda i: (0, 0)
    in_specs = [pl.BlockSpec((tm, rw.shape[1]), row), pl.BlockSpec((tm, sm.shape[1]), row),
                pl.BlockSpec((tm, gd.shape[1]), row), pl.BlockSpec((tm, gate.shape[1]), row),
                pl.BlockSpec((tm, d), row),
                pl.BlockSpec(w_r.shape, fixed), pl.BlockSpec(w_s.shape, fixed),
                pl.BlockSpec(w_g.shape, fixed), pl.BlockSpec(w_o.shape, fixed),
                pl.BlockSpec((1, d), fixed)]
    n_out = 2 if with_final else 1
    out_specs = [pl.BlockSpec((tm, d), row)] * n_out
    out_shape = [jax.ShapeDtypeStruct((t, d), F32)] * n_out
    return pl.pallas_call(
        functools.partial(_outproj_body, d_model=d),
        grid=grid, in_specs=in_specs, out_specs=out_specs, out_shape=out_shape,
        compiler_params=pltpu.CompilerParams(dimension_semantics=("arbitrary",),
                                             vmem_limit_bytes=VMEM_LIMIT),
        name="outproj",
    )(rw, sm, gd, gate, x, w_r, w_s, w_g, w_o, final_w)


def _rwkv_pre(p, shifted, mu, w0, w2, a0, a2, k_k, k_a, width, lora):
    u = p + (shifted - p) * mu
    r = u[:, :width]
    k = u[:, width:2 * width]
    v = u[:, 2 * width:3 * width]
    wd = u[:, 3 * width:3 * width + lora]
    ad = u[:, 3 * width + lora:]
    w = -_softplus(-(w0 + _mm(jnp.tanh(wd), w2))) - 0.5
    log_decay = -jnp.exp(w)
    a = _sigmoid(a0 + _mm(ad, a2))
    kk_raw = k * k_k
    k_mod = k * (1.0 + (a - 1.0) * k_a)
    return r, k_mod, v, log_decay, a, kk_raw


def _l2norm(x):
    return x * lax.rsqrt(jnp.sum(x * x, axis=-1, keepdims=True) + L2_EPS)


def _rwkv_post_head(y, r, k_mod, v, z, r_k, gn_w, gn_b):
    mean = jnp.mean(y, axis=-1, keepdims=True)
    var = jnp.mean(jnp.square(y - mean), axis=-1, keepdims=True)
    yn = (y - mean) * lax.rsqrt(var + RW_GN_EPS) * gn_w + gn_b
    bonus = jnp.sum(r * k_mod * r_k, axis=-1, keepdims=True) * v
    return (yn + bonus) * _silu(z)


def _rwkv_prompt_body(p_ref, z_ref, ones_ref, mu_ref, w0_ref, w2_ref, a0_ref, a2_ref, kk_ref,
                      ka_ref, rk_ref, gnw_ref, gnb_ref, o_ref, wkv_ref, shift_ref, prev_scr,
                      *, n_heads, head_dim, n_sub):
    width = n_heads * head_dim
    lora = w2_ref.shape[0]
    n_seq, rows, shift_w = p_ref.shape
    c = rows // n_sub
    p = p_ref[...].reshape(n_seq * rows, shift_w)

    @pl.when(pl.program_id(1) == 0)
    def _():
        wkv_ref[...] = jnp.zeros(wkv_ref.shape, F32)
        prev_scr[...] = jnp.zeros(prev_scr.shape, F32)

    row = _iota2((n_seq * rows, 1), 0)
    shifted = pltpu.roll(p, 1, 0)
    for s in range(n_seq):
        shifted = jnp.where(row == s * rows, prev_scr[s], shifted)
        last_row = p[(s + 1) * rows - 1:(s + 1) * rows, :]
        prev_scr[s] = last_row
        shift_ref[s] = last_row

    r, k_mod, v, log_decay, a, kk_raw = _rwkv_pre(
        p, shifted, mu_ref[...], w0_ref[...], w2_ref[...], a0_ref[...], a2_ref[...],
        kk_ref[...], ka_ref[...], width, lora)
    tril = _block_tril(rows, c)
    cum = jnp.concatenate([_mm(tril, log_decay[s * rows:(s + 1) * rows], exact=True)
                           for s in range(n_seq)], axis=0)
    e_pos = jnp.exp(cum)
    e_neg = jnp.exp(-cum)
    e_prev = jnp.exp(cum - log_decay)
    z = z_ref[...].reshape(n_seq * rows, width)
    incl, strict = _chunk_masks(c)

    hs = range(n_heads)
    sls = [slice(h * head_dim, (h + 1) * head_dim) for h in hs]
    n_chunks = n_seq * n_sub
    units = [(j, h) for j in range(n_chunks) for h in hs]
    rsl = [slice(j * c, (j + 1) * c) for j in range(n_chunks)]
    seg_ones = ones_ref[...]
    kk = kk_raw * lax.rsqrt(_seg_sum(kk_raw * kk_raw, seg_ones) + L2_EPS)
    a_all = -kk * e_prev
    b_all = kk * a * e_neg
    k_all = k_mod * e_neg
    r_all = r * e_pos
    a_t = [a_all[:, sl] for sl in sls]
    b_t = [b_all[:, sl] for sl in sls]
    k_t = [k_all[:, sl] for sl in sls]
    r_t = [r_all[:, sl] for sl in sls]
    v_h = [v[:, sl] for sl in sls]

    sc = {u: _mm_nt(jnp.concatenate([a_t[u[1]][rsl[u[0]]], r_t[u[1]][rsl[u[0]]]], axis=0),
                    jnp.concatenate([b_t[u[1]][rsl[u[0]]], k_t[u[1]][rsl[u[0]]]], axis=0))
          for u in units}
    l_ab = {u: jnp.where(strict, sc[u][:c, :c], 0.0) for u in units}
    l_ak = {u: jnp.where(strict, sc[u][:c, c:], 0.0) for u in units}
    m_rb = {u: jnp.where(incl, sc[u][c:, :c], 0.0) for u in units}
    m_rk = {u: jnp.where(incl, sc[u][c:, c:], 0.0) for u in units}
    lv = {u: _mm(jnp.concatenate([l_ak[u], m_rk[u]], axis=0), v_h[u[1]][rsl[u[0]]]) for u in units}
    inv = dict(zip(units, _unit_lower_inverse([l_ab[u] for u in units])))
    rhs = {u: jnp.concatenate([a_t[u[1]][rsl[u[0]]], lv[u][:c]], axis=1) for u in units}
    wu = {u: rhs[u] + _mm(inv[u], rhs[u]) for u in units}
    d_end = {(j, h): e_pos[(j + 1) * c - 1:(j + 1) * c, sls[h]] for j, h in units}
    qy = {u: _mm(m_rb[u], wu[u]) for u in units}
    mn = {u: _mm_tn(wu[u], b_t[u[1]][rsl[u[0]]] * d_end[u]) for u in units}
    n2 = {u: _mm_tn(v_h[u[1]][rsl[u[0]]], k_t[u[1]][rsl[u[0]]] * d_end[u]) for u in units}

    sh = [(s, h) for s in range(n_seq) for h in hs]
    states = {(s, h): wkv_ref[s, h] for s, h in sh}
    ys = {k: [] for k in sh}
    for jj in range(n_sub):
        ju = {(s, h): (s * n_sub + jj, h) for s, h in sh}
        y_dot = {k: _mm_nt(r_t[k[1]][rsl[ju[k][0]]] + qy[ju[k]][:, :c], states[k]) for k in sh}
        s_dot = {k: _mm(states[k], mn[ju[k]][:c]) for k in sh}
        for k in sh:
            u = ju[k]
            ys[k].append(y_dot[k] + qy[u][:, c:] + lv[u][c:])
            states[k] = states[k] * d_end[u] + s_dot[k] + (mn[u][c:] + n2[u])
    for s, h in sh:
        wkv_ref[s, h] = states[(s, h)]
    y = jnp.concatenate([jnp.concatenate([t for s in range(n_seq) for t in ys[(s, h)]], axis=0)
                         for h in hs], axis=-1)
    inv_n = 1.0 / head_dim
    dev = y - _seg_sum(y, seg_ones) * inv_n
    var = _seg_sum(dev * dev, seg_ones) * inv_n
    yn = dev * lax.rsqrt(var + RW_GN_EPS) * gnw_ref[...] + gnb_ref[...]
    bonus = _seg_sum(r * k_mod * rk_ref[...], seg_ones, split=True) * v
    o_ref[...] = ((yn + bonus) * _silu(z)).reshape(n_seq, rows, width)


def _seq_specs(n_seq, rows, n_steps, arrays):
    views = [a.reshape(-1, n_steps * rows, a.shape[1]) for a in arrays]
    specs = [pl.BlockSpec((n_seq, rows, a.shape[1]), lambda b, c: (b, c, 0)) for a in arrays]
    return views, specs


def _rwkv_prompt(p_rw, z, prm, *, batch, n_steps, n_sub, n_seq, n_heads, head_dim):
    t, shift_w = p_rw.shape
    width = n_heads * head_dim
    rows = n_sub * CHUNK
    fixed = lambda b, c: (0, 0)
    names = ("mu", "w0", "w2", "a0", "a2", "k_k", "k_a", "r_k", "gn_w", "gn_b")
    params = [_segment_ones(SEG_TILE, head_dim)] + [prm[n] for n in names]
    views, in_specs = _seq_specs(n_seq, rows, n_steps, [p_rw, z])
    in_specs += [pl.BlockSpec(a.shape, fixed) for a in params]
    out_specs = [pl.BlockSpec((n_seq, rows, width), lambda b, c: (b, c, 0)),
                 pl.BlockSpec((n_seq, n_heads, head_dim, head_dim), lambda b, c: (b, 0, 0, 0)),
                 pl.BlockSpec((n_seq, 1, shift_w), lambda b, c: (b, 0, 0))]
    out_shape = [jax.ShapeDtypeStruct((batch, n_steps * rows, width), F32),
                 jax.ShapeDtypeStruct((batch, n_heads, head_dim, head_dim), F32),
                 jax.ShapeDtypeStruct((batch, 1, shift_w), F32)]
    o, wkv, shift = pl.pallas_call(
        functools.partial(_rwkv_prompt_body, n_heads=n_heads, head_dim=head_dim, n_sub=n_sub),
        grid=(batch // n_seq, n_steps), in_specs=in_specs, out_specs=out_specs,
        out_shape=out_shape,
        scratch_shapes=[pltpu.VMEM((n_seq, 1, shift_w), F32)],
        compiler_params=pltpu.CompilerParams(dimension_semantics=("arbitrary", "arbitrary"),
                                             vmem_limit_bytes=VMEM_LIMIT),
        name="rwkv_prompt",
    )(*views, *params)
    return o.reshape(t, width), wkv, shift


def _ssm_gated_norm(y, z, norm_w, n_groups):
    g = y * _silu(z)
    gw = g.shape[1] // n_groups
    parts = []
    for i in range(n_groups):
        gi = g[:, i * gw:(i + 1) * gw]
        parts.append(gi * lax.rsqrt(jnp.mean(gi * gi, axis=-1, keepdims=True) + SSM_NORM_EPS))
    return jnp.concatenate(parts, axis=-1) * norm_w


def _ssd_prompt_body(xbc_ref, small_ref, z_ref, cw_ref, cb_ref, dtb_row_ref, dtb_col_ref,
                     alog_row_ref, alog_col_ref, d_ref, nw_ref, o_ref, h_ref, conv_ref, ext_scr,
                     *, n_heads, head_dim, n_groups, d_state, n_pad, n_sub):
    first = pl.program_id(1) == 0
    width = n_heads * head_dim
    hpg = n_heads // n_groups
    gw = hpg * head_dim
    n_seq, rows, _ = xbc_ref.shape
    c = rows // n_sub
    n_chunks = n_seq * n_sub
    rsl = [slice(j * c, (j + 1) * c) for j in range(n_chunks)]

    @pl.when(first)
    def _():
        h_ref[...] = jnp.zeros(h_ref.shape, F32)

    convs = []
    for s in range(n_seq):
        conv, tail = _conv_chunk(ext_scr.at[s], xbc_ref[s], cw_ref, first)
        conv_ref[s] = tail
        convs.append(conv)
    xbc = _silu(jnp.concatenate(convs, axis=0) + cb_ref[...])
    xs = xbc[:, :width]
    b_all = xbc[:, width:width + n_groups * d_state]
    c_all = xbc[:, width + n_groups * d_state:]

    small = small_ref[...].reshape(n_seq * rows, SMALL_W)
    lane = _iota2((1, SMALL_W), 1)
    sub = _iota2((SMALL_W, 1), 0)
    dt = _softplus(small + dtb_row_ref[...])
    assert n_heads <= SUBLANE
    dt_t = [_softplus(small[sl].T[:SUBLANE] + dtb_col_ref[0:SUBLANE, :]) for sl in rsl]
    if n_pad:
        assert n_pad <= c
        not_first = jnp.logical_not(first)
        row = _iota2((n_seq * rows, 1), 0)
        for s in range(n_seq):
            in_pad = jnp.logical_and(row >= s * rows, row < s * rows + n_pad)
            dt = jnp.where(jnp.logical_and(first, in_pad), 0.0, dt)
        for s in range(n_seq):
            j0 = s * n_sub
            dt_t[j0] = jnp.where(jnp.logical_or(not_first, _iota2((1, c), 1) >= n_pad),
                                 dt_t[j0], 0.0)
    a_row = jnp.where(lane < n_heads, -jnp.exp(alog_row_ref[...]), 0.0)
    a_col = jnp.where(sub[:SUBLANE] < n_heads, -jnp.exp(alog_col_ref[0:SUBLANE, :]), 0.0)
    incl, _ = _chunk_masks(c)
    triu = jnp.where(_iota2((c, c), 0) <= _iota2((c, c), 1), 1.0, 0.0).astype(F32)
    tril = _block_tril(rows, c)
    da = dt * a_row
    acs = jnp.concatenate([_mm(tril, da[s * rows:(s + 1) * rows], exact=True)
                           for s in range(n_seq)], axis=0)
    acs_t = [_mm(t * a_col, triu, exact=True) for t in dt_t]
    expand = jnp.where((_iota2((SMALL_W, width), 1) // head_dim) == _iota2((SMALL_W, width), 0),
                       1.0, 0.0).astype(F32)
    wide = _mm(jnp.concatenate([acs, dt], axis=0), expand, exact=True)
    acs_w = wide[:n_seq * rows]
    xdt = xs * wide[n_seq * rows:]
    e_acs = jnp.exp(acs_w)

    hs = range(n_heads)
    gs = range(n_groups)
    js = range(n_chunks)
    b_g = {(j, g): b_all[rsl[j], g * d_state:(g + 1) * d_state] for j in js for g in gs}
    c_g = {(j, g): c_all[rsl[j], g * d_state:(g + 1) * d_state] for j in js for g in gs}
    a_last = {j: acs_w[(j + 1) * c - 1:(j + 1) * c] for j in js}
    xdt_end = {j: xdt[rsl[j]] * jnp.exp(a_last[j] - acs_w[rsl[j]]) for j in js}

    cb = {jg: _mm_nt(c_g[jg], b_g[jg]) for jg in b_g}
    seg = {(j, h): jnp.exp(jnp.where(incl, acs[rsl[j], h:h + 1] - acs_t[j][h:h + 1, :], -jnp.inf))
           for j in js for h in hs}
    y_diag = {(j, h): _mm(cb[(j, h // hpg)] * seg[(j, h)],
                          xdt[rsl[j], h * head_dim:(h + 1) * head_dim])
              for j in js for h in hs}
    st = {(j, g): _mm_tn(xdt_end[j][:, g * gw:(g + 1) * gw], b_g[(j, g)]) for j in js for g in gs}

    sg = [(s, g) for s in range(n_seq) for g in gs]
    states = {(s, g): h_ref[s, g * hpg:(g + 1) * hpg].reshape(gw, d_state) for s, g in sg}
    ys = {s: [] for s in range(n_seq)}
    for jj in range(n_sub):
        y_off = {(s, g): _mm_nt(c_g[(s * n_sub + jj, g)], states[(s, g)]) for s, g in sg}
        for s in range(n_seq):
            j = s * n_sub + jj
            ys[s].append(jnp.concatenate([y_off[(s, g)] for g in gs], axis=-1) * e_acs[rsl[j]]
                         + jnp.concatenate([y_diag[(j, h)] for h in hs], axis=-1))
            for g in gs:
                scaled = [states[(s, g)][i * head_dim:(i + 1) * head_dim]
                          * jnp.exp(acs[(j + 1) * c - 1:(j + 1) * c, g * hpg + i:g * hpg + i + 1])
                          for i in range(hpg)]
                states[(s, g)] = jnp.concatenate(scaled, axis=0) + st[(j, g)]
    for s, g in sg:
        h_ref[s, g * hpg:(g + 1) * hpg] = states[(s, g)].reshape(hpg, head_dim, d_state)
    y_all = jnp.concatenate([t for s in range(n_seq) for t in ys[s]], axis=0) + xs * d_ref[...]
    z = z_ref[...].reshape(n_seq * rows, width)
    o_ref[...] = _ssm_gated_norm(y_all, z, nw_ref[...], n_groups).reshape(n_seq, rows, width)


def _ssd_prompt(xbc, small, z, prm, *, batch, n_steps, n_sub, n_seq, n_heads, head_dim, n_groups,
                d_state, n_pad):
    t, conv_ch = xbc.shape
    width = n_heads * head_dim
    rows = n_sub * CHUNK
    fixed = lambda b, c: (0, 0)
    names = ("conv_w", "conv_b", "dtb_row", "dtb_col", "alog_row", "alog_col", "d", "norm_w")
    params = [prm[n] for n in names]
    views, in_specs = _seq_specs(n_seq, rows, n_steps, [xbc, small, z])
    in_specs += [pl.BlockSpec(a.shape, fixed) for a in params]
    out_specs = [pl.BlockSpec((n_seq, rows, width), lambda b, c: (b, c, 0)),
                 pl.BlockSpec((n_seq, n_heads, head_dim, d_state), lambda b, c: (b, 0, 0, 0)),
                 pl.BlockSpec((n_seq, CONV_W - 1, conv_ch), lambda b, c: (b, 0, 0))]
    out_shape = [jax.ShapeDtypeStruct((batch, n_steps * rows, width), F32),
                 jax.ShapeDtypeStruct((batch, n_heads, head_dim, d_state), F32),
                 jax.ShapeDtypeStruct((batch, CONV_W - 1, conv_ch), F32)]
    o, state, conv = pl.pallas_call(
        functools.partial(_ssd_prompt_body, n_heads=n_heads, head_dim=head_dim,
                          n_groups=n_groups, d_state=d_state, n_pad=n_pad, n_sub=n_sub),
        grid=(batch // n_seq, n_steps), in_specs=in_specs, out_specs=out_specs,
        out_shape=out_shape,
        scratch_shapes=[pltpu.VMEM((n_seq, rows + CONV_HALO, conv_ch), F32)],
        compiler_params=pltpu.CompilerParams(dimension_semantics=("arbitrary", "arbitrary"),
                                             vmem_limit_bytes=VMEM_LIMIT),
        name="ssd_prompt",
    )(*views, *params)
    return o.reshape(t, width), state, conv


def _gdn_post_head(o, z, norm_w):
    return o * lax.rsqrt(jnp.mean(o * o, axis=-1, keepdims=True) + GDN_NORM_EPS) * norm_w * _silu(z)


def _gdn_prompt_body(qkv_ref, small_ref, z_ref, cw_ref, dtb_row_ref, dtb_col_ref, alog_row_ref,
                     alog_col_ref, nw_ref, o_ref, s_ref, conv_ref, ext_scr,
                     *, n_heads, head_dim, a_off, b_off, n_sub):
    first = pl.program_id(1) == 0
    width = n_heads * head_dim
    n_seq, rows, _ = qkv_ref.shape
    c = rows // n_sub
    n_chunks = n_seq * n_sub

    @pl.when(first)
    def _():
        s_ref[...] = jnp.zeros(s_ref.shape, F32)

    convs = []
    for s in range(n_seq):
        conv, tail = _conv_chunk(ext_scr.at[s], qkv_ref[s], cw_ref, first)
        conv_ref[s] = tail
        convs.append(conv)
    qkv = _silu(jnp.concatenate(convs, axis=0))
    small = small_ref[...].reshape(n_seq * rows, SMALL_W)
    rsl = [slice(j * c, (j + 1) * c) for j in range(n_chunks)]
    small_t = [small[sl].T for sl in rsl]
    lane = _iota2((1, SMALL_W), 1)
    sub = _iota2((SMALL_W, 1), 0)
    g = jnp.where(jnp.logical_and(lane >= a_off, lane < a_off + n_heads),
                  -jnp.exp(alog_row_ref[...]) * _softplus(small + dtb_row_ref[...]), 0.0)
    g_t = [jnp.where(jnp.logical_and(sub >= a_off, sub < a_off + n_heads),
                     -jnp.exp(alog_col_ref[...]) * _softplus(st + dtb_col_ref[...]), 0.0)
           for st in small_t]
    beta = _sigmoid(small)
    incl, strict = _chunk_masks(c)
    triu = jnp.where(_iota2((c, c), 0) <= _iota2((c, c), 1), 1.0, 0.0).astype(F32)
    tril = _block_tril(rows, c)
    gcs = jnp.concatenate([_mm(tril, g[s * rows:(s + 1) * rows], exact=True)
                           for s in range(n_seq)], axis=0)
    gcs_t = [_mm(gt, triu, exact=True) for gt in g_t]
    z = z_ref[...].reshape(n_seq * rows, width)
    scale = head_dim ** -0.5

    hs = range(n_heads)
    sls = [slice(h * head_dim, (h + 1) * head_dim) for h in hs]
    units = [(j, h) for j in range(n_chunks) for h in hs]
    q_h = [_l2norm(qkv[:, sl]) * scale for sl in sls]
    k_h = [_l2norm(qkv[:, width + h * head_dim:width + (h + 1) * head_dim]) for h in hs]
    v_h = [qkv[:, 2 * width + h * head_dim:2 * width + (h + 1) * head_dim] for h in hs]
    g_c = {(j, h): gcs[rsl[j], a_off + h:a_off + h + 1] for j, h in units}
    g_r = {(j, h): gcs_t[j][a_off + h:a_off + h + 1, :] for j, h in units}
    g_last = {(j, h): gcs[(j + 1) * c - 1:(j + 1) * c, a_off + h:a_off + h + 1] for j, h in units}
    b_c = {(j, h): beta[rsl[j], b_off + h:b_off + h + 1] for j, h in units}
    q = {(j, h): q_h[h][rsl[j]] for j, h in units}
    k = {(j, h): k_h[h][rsl[j]] for j, h in units}
    v = {(j, h): v_h[h][rsl[j]] for j, h in units}

    decay = {u: jnp.exp(jnp.where(incl, g_c[u] - g_r[u], -jnp.inf)) for u in units}
    kb = {u: k[u] * b_c[u] for u in units}
    e_c = {u: jnp.exp(g_c[u]) for u in units}
    sc = {u: _mm_nt(jnp.concatenate([kb[u], q[u]], axis=0), k[u]) for u in units}
    a_mat = {u: jnp.where(strict, sc[u][:c] * decay[u], 0.0) for u in units}
    qk = {u: jnp.where(incl, sc[u][c:] * decay[u], 0.0) for u in units}
    inv = dict(zip(units, _unit_lower_inverse([-a_mat[u] for u in units])))
    rhs = {u: jnp.concatenate([v[u] * b_c[u], kb[u] * e_c[u]], axis=1) for u in units}
    uw = {u: rhs[u] + _mm(inv[u], rhs[u]) for u in units}
    qo = {u: _mm(qk[u], uw[u]) for u in units}
    mn = {u: _mm_tn(k[u] * jnp.exp(g_last[u] - g_c[u]), uw[u]) for u in units}

    sh = [(s, h) for s in range(n_seq) for h in hs]
    states = {(s, h): s_ref[s, h] for s, h in sh}
    os_ = {k_: [] for k_ in sh}
    for jj in range(n_sub):
        ju = {(s, h): (s * n_sub + jj, h) for s, h in sh}
        o_dot = {k_: _mm(q[ju[k_]] * e_c[ju[k_]] - qo[ju[k_]][:, head_dim:], states[k_])
                 for k_ in sh}
        s_dot = {k_: _mm(mn[ju[k_]][:, head_dim:], states[k_]) for k_ in sh}
        for k_ in sh:
            u = ju[k_]
            os_[k_].append(o_dot[k_] + qo[u][:, :head_dim])
            states[k_] = states[k_] * jnp.exp(g_last[u]) - s_dot[k_] + mn[u][:, :head_dim]
    for s, h in sh:
        s_ref[s, h] = states[(s, h)]
    outs = [_gdn_post_head(jnp.concatenate([t for s in range(n_seq) for t in os_[(s, h)]], axis=0),
                           z[:, sls[h]], nw_ref[...]) for h in hs]
    o_ref[...] = jnp.concatenate(outs, axis=-1).reshape(n_seq, rows, width)


def _gdn_prompt(qkv, small, z, prm, *, batch, n_steps, n_sub, n_seq, n_heads, head_dim, a_off,
                b_off):
    t, conv_ch = qkv.shape
    width = n_heads * head_dim
    rows = n_sub * CHUNK
    fixed = lambda b, c: (0, 0)
    names = ("conv_w", "dtb_row", "dtb_col", "alog_row", "alog_col", "norm_w")
    params = [prm[n] for n in names]
    views, in_specs = _seq_specs(n_seq, rows, n_steps, [qkv, small, z])
    in_specs += [pl.BlockSpec(a.shape, fixed) for a in params]
    out_specs = [pl.BlockSpec((n_seq, rows, width), lambda b, c: (b, c, 0)),
                 pl.BlockSpec((n_seq, n_heads, head_dim, head_dim), lambda b, c: (b, 0, 0, 0)),
                 pl.BlockSpec((n_seq, CONV_W - 1, conv_ch), lambda b, c: (b, 0, 0))]
    out_shape = [jax.ShapeDtypeStruct((batch, n_steps * rows, width), F32),
                 jax.ShapeDtypeStruct((batch, n_heads, head_dim, head_dim), F32),
                 jax.ShapeDtypeStruct((batch, CONV_W - 1, conv_ch), F32)]
    o, state, conv = pl.pallas_call(
        functools.partial(_gdn_prompt_body, n_heads=n_heads, head_dim=head_dim,
                          a_off=a_off, b_off=b_off, n_sub=n_sub),
        grid=(batch // n_seq, n_steps), in_specs=in_specs, out_specs=out_specs,
        out_shape=out_shape,
        scratch_shapes=[pltpu.VMEM((n_seq, rows + CONV_HALO, conv_ch), F32)],
        compiler_params=pltpu.CompilerParams(dimension_semantics=("arbitrary", "arbitrary"),
                                             vmem_limit_bytes=VMEM_LIMIT),
        name="gdn_prompt",
    )(*views, *params)
    return o.reshape(t, width), state, conv


def _conv_step(u, conv_ref, convo_ref, w_ref):
    out = u * w_ref[CONV_W - 1:CONV_W, :]
    for i in range(CONV_W - 1):
        out = out + conv_ref[i] * w_ref[i:i + 1, :]
    for i in range(CONV_W - 2):
        convo_ref[i] = conv_ref[i + 1]
    convo_ref[CONV_W - 2] = u
    return out


def _lane_pick(x, lane, idx):
    return jnp.sum(jnp.where(lane == idx, x, 0.0), axis=1, keepdims=True)


def _rwkv_decode_body(p_ref, z_ref, prev_ref, s_ref, mu_ref, w0_ref, w2_ref, a0_ref, a2_ref,
                      kk_ref, ka_ref, rk_ref, gnw_ref, gnb_ref, o_ref, so_ref,
                      t_scr, y_scr, row_scr, *, n_heads, head_dim):
    h = pl.program_id(0)
    width = n_heads * head_dim
    lora = w2_ref.shape[0]
    bsz = p_ref.shape[0]

    @pl.when(h == 0)
    def _():
        r, k_mod, v, log_decay, a, kk_raw = _rwkv_pre(
            p_ref[...], prev_ref[...], mu_ref[...], w0_ref[...], w2_ref[...], a0_ref[...],
            a2_ref[...], kk_ref[...], ka_ref[...], width, lora)
        kk = jnp.concatenate(
            [_l2norm(kk_raw[:, i * head_dim:(i + 1) * head_dim]) for i in range(n_heads)], axis=-1)
        row_scr[0] = r
        row_scr[1] = k_mod
        row_scr[2] = v
        for i, arr in enumerate((r, jnp.exp(log_decay), k_mod, v, -kk, kk * a)):
            t_scr[i] = arr.T.reshape(n_heads, head_dim, bsz)

    r_t = t_scr[0, h]
    w_t = t_scr[1, h]
    k_t = t_scr[2, h]
    a_t = t_scr[4, h]
    b_t = t_scr[5, h]

    def step(i, carry):
        s_v = s_ref[i]
        sa = jnp.sum(s_v * a_t, axis=0, keepdims=True)
        v_i = t_scr[3, h, pl.ds(i, 1), :]
        s_new = s_v * w_t + sa * b_t + v_i * k_t
        so_ref[i] = s_new
        y_scr[h, pl.ds(i, 1), :] = jnp.sum(s_new * r_t, axis=0, keepdims=True)
        return carry

    lax.fori_loop(0, head_dim, step, 0, unroll=DECODE_UNROLL)

    @pl.when(h == n_heads - 1)
    def _():
        y = y_scr[...].reshape(width, bsz).T
        r, k_mod, v, z = row_scr[0], row_scr[1], row_scr[2], z_ref[...]
        outs = []
        for i in range(n_heads):
            sl = slice(i * head_dim, (i + 1) * head_dim)
            outs.append(_rwkv_post_head(y[:, sl], r[:, sl], k_mod[:, sl], v[:, sl], z[:, sl],
                                        rk_ref[:, sl], gnw_ref[:, sl], gnb_ref[:, sl]))
        o_ref[...] = jnp.concatenate(outs, axis=-1)


def _layer_state_call(body, name, row_inputs, layer_inputs, state_all, new_all, params, *,
                      layer, n_row_out, scratch_shapes, batch_minor=False, tap_inputs=()):
    bsz = row_inputs[0].shape[0]
    fixed = lambda h: (0, 0)
    at_layer = lambda h: (layer, 0)
    if batch_minor:
        n_heads = state_all.shape[1]
        state_spec = pl.BlockSpec((None, None) + state_all.shape[2:],
                                  lambda h: (layer, h, 0, 0, 0))
    else:
        n_heads = state_all.shape[2]
        state_spec = pl.BlockSpec((None, bsz, None) + state_all.shape[3:],
                                  lambda h: (layer, 0, h, 0, 0))
    in_specs = [pl.BlockSpec(a.shape, fixed) for a in row_inputs]
    in_specs += [pl.BlockSpec((bsz, a.shape[1]), at_layer) for a in layer_inputs]
    in_specs += [pl.BlockSpec((None,) + a.shape[1:], lambda h: (layer, 0, 0, 0))
                 for a in tap_inputs]
    in_specs += [state_spec]
    in_specs += [pl.BlockSpec(a.shape, fixed) for a in params]
    in_specs.append(pl.BlockSpec(memory_space=pl.ANY))
    operands = (list(row_inputs) + list(layer_inputs) + list(tap_inputs) + [state_all]
                + list(params) + [new_all])
    n_in = len(operands)
    aliases = {n_in - 1: len(n_row_out) + len(tap_inputs)}
    inner = body

    def body(*refs):
        return inner(*refs[:n_in - 1], *refs[n_in:])

    out_specs = [pl.BlockSpec((bsz, w), fixed) for w in n_row_out]
    out_specs += [pl.BlockSpec(a.shape[1:], lambda h: (0, 0, 0)) for a in tap_inputs]
    out_specs += [state_spec]
    out_shape = [jax.ShapeDtypeStruct((bsz, w), F32) for w in n_row_out]
    out_shape += [jax.ShapeDtypeStruct(a.shape[1:], F32) for a in tap_inputs]
    out_shape += [jax.ShapeDtypeStruct(state_all.shape, F32)]
    return pl.pallas_call(
        body, grid=(n_heads,), in_specs=in_specs, out_specs=out_specs, out_shape=out_shape,
        scratch_shapes=scratch_shapes, input_output_aliases=aliases,
        compiler_params=pltpu.CompilerParams(dimension_semantics=("arbitrary",),
                                             vmem_limit_bytes=VMEM_LIMIT),
        name=name,
    )(*operands)


def _rwkv_decode(p_rw, z, prev_all, state_all, new_all, prm, *, layer):
    bsz = p_rw.shape[0]
    n_heads, head_dim = state_all.shape[1], state_all.shape[2]
    width = n_heads * head_dim
    names = ("mu", "w0", "w2", "a0", "a2", "k_k", "k_a", "r_k", "gn_w", "gn_b")
    return _layer_state_call(
        functools.partial(_rwkv_decode_body, n_heads=n_heads, head_dim=head_dim), "rwkv_decode",
        [p_rw, z], [prev_all], state_all, new_all, [prm[n] for n in names],
        layer=layer, n_row_out=[width], batch_minor=True,
        scratch_shapes=[pltpu.VMEM((6, n_heads, head_dim, bsz), F32),
                        pltpu.VMEM((n_heads, head_dim, bsz), F32),
                        pltpu.VMEM((3, bsz, width), F32)])


def _ssd_decode_body(xbc_ref, small_ref, z_ref, conv_ref, s_ref, cw_ref, cb_ref, dtb_row_ref,
                     alog_row_ref, d_ref, nw_ref, o_ref, convo_ref, so_ref,
                     xt_scr, bc_scr, dec_scr, y_scr, skip_scr,
                     *, n_heads, head_dim, n_groups, d_state):
    h = pl.program_id(0)
    width = n_heads * head_dim
    bsz = xbc_ref.shape[0]
    hpg = n_heads // n_groups

    @pl.when(h == 0)
    def _():
        conv = _conv_step(xbc_ref[...], conv_ref, convo_ref, cw_ref)
        xbc = _silu(conv + cb_ref[...])
        lane_s = _iota2((1, SMALL_W), 1)
        dt = _softplus(small_ref[...] + dtb_row_ref[...])
        a_row = jnp.where(lane_s < n_heads, -jnp.exp(alog_row_ref[...]), 0.0)
        dec_scr[...] = jnp.exp(dt * a_row).T
        expand = jnp.where((_iota2((SMALL_W, width), 1) // head_dim) == _iota2((SMALL_W, width), 0),
                           1.0, 0.0).astype(F32)
        xs = xbc[:, :width]
        xt_scr[...] = (xs * _mm(dt, expand, exact=True)).T.reshape(n_heads, head_dim, bsz)
        skip_scr[...] = xs * d_ref[...]
        for i in range(2 * n_groups):
            bc_scr[i] = xbc[:, width + i * d_state:width + (i + 1) * d_state]

    g = h // hpg
    xdt_t = xt_scr[h]
    dec_row = dec_scr[pl.ds(h, 1), :]
    lane = _iota2((head_dim, bsz), 1)
    lane1 = _iota2((1, bsz), 1)

    def step(b, y_t):
        b_row = bc_scr[g, pl.ds(b, 1), :]
        c_row = bc_scr[n_groups + g, pl.ds(b, 1), :]
        s_new = (s_ref[b] * _lane_pick(dec_row, lane1, b)
                 + _lane_pick(xdt_t, lane, b) * b_row)
        so_ref[b] = s_new
        y_col = jnp.sum(s_new * c_row, axis=1, keepdims=True)
        return jnp.where(lane == b, y_col, y_t)

    y_scr[h] = lax.fori_loop(0, bsz, step, jnp.zeros((head_dim, bsz), F32), unroll=DECODE_UNROLL)

    @pl.when(h == n_heads - 1)
    def _():
        y = y_scr[...].reshape(width, bsz).T + skip_scr[...]
        o_ref[...] = _ssm_gated_norm(y, z_ref[...], nw_ref[...], n_groups)


def _ssd_decode(xbc, small, z, conv_all, state_all, new_all, prm, *, layer, n_groups):
    bsz = xbc.shape[0]
    n_heads, head_dim, d_state = state_all.shape[2:]
    width = n_heads * head_dim
    names = ("conv_w", "conv_b", "dtb_row", "alog_row", "d", "norm_w")
    return _layer_state_call(
        functools.partial(_ssd_decode_body, n_heads=n_heads, head_dim=head_dim,
                          n_groups=n_groups, d_state=d_state), "ssd_decode",
        [xbc, small, z], [], state_all, new_all, [prm[n] for n in names],
        layer=layer, n_row_out=[width], tap_inputs=[conv_all],
        scratch_shapes=[pltpu.VMEM((n_heads, head_dim, bsz), F32),
                        pltpu.VMEM((2 * n_groups, bsz, d_state), F32),
                        pltpu.VMEM((SMALL_W, bsz), F32),
                        pltpu.VMEM((n_heads, head_dim, bsz), F32),
                        pltpu.VMEM((bsz, width), F32)])


def _gdn_decode_body(qkv_ref, small_ref, z_ref, conv_ref, s_ref, cw_ref, dtb_row_ref,
                     alog_row_ref, nw_ref, o_ref, convo_ref, so_ref,
                     t_scr, v_scr, gate_scr, y_scr, *, n_heads, head_dim, a_off, b_off):
    h = pl.program_id(0)
    width = n_heads * head_dim
    bsz = qkv_ref.shape[0]

    @pl.when(h == 0)
    def _():
        qkv = _silu(_conv_step(qkv_ref[...], conv_ref, convo_ref, cw_ref))
        scale = head_dim ** -0.5
        q = jnp.concatenate([_l2norm(qkv[:, i * head_dim:(i + 1) * head_dim]) * scale
                             for i in range(n_heads)], axis=-1)
        k = jnp.concatenate([_l2norm(qkv[:, width + i * head_dim:width + (i + 1) * head_dim])
                             for i in range(n_heads)], axis=-1)
        for i, arr in enumerate((q, k)):
            t_scr[i] = arr.T.reshape(n_heads, head_dim, bsz)
        for i in range(n_heads):
            v_scr[i] = qkv[:, 2 * width + i * head_dim:2 * width + (i + 1) * head_dim]
        small = small_ref[...]
        lane_s = _iota2((1, SMALL_W), 1)
        g = jnp.where(jnp.logical_and(lane_s >= a_off, lane_s < a_off + n_heads),
                      -jnp.exp(alog_row_ref[...]) * _softplus(small + dtb_row_ref[...]), 0.0)
        gate_scr[0] = jnp.exp(g).T
        gate_scr[1] = _sigmoid(small).T
        gate_scr[2, 0:n_heads, :] = jnp.concatenate(
            [jnp.sum(q[:, i * head_dim:(i + 1) * head_dim] * k[:, i * head_dim:(i + 1) * head_dim],
                     axis=1, keepdims=True) for i in range(n_heads)], axis=1).T

    q_t = t_scr[0, h]
    k_t = t_scr[1, h]
    eg_row = gate_scr[0, pl.ds(a_off + h, 1), :]
    beta_row = gate_scr[1, pl.ds(b_off + h, 1), :]
    qk_row = gate_scr[2, pl.ds(h, 1), :]
    lane = _iota2((head_dim, bsz), 1)
    lane1 = _iota2((1, bsz), 1)

    def step(b, carry):
        s_b = s_ref[b]
        k_col = _lane_pick(k_t, lane, b)
        q_col = _lane_pick(q_t, lane, b)
        eg = _lane_pick(eg_row, lane1, b)
        k_s = jnp.sum(s_b * k_col, axis=0, keepdims=True)
        q_s = jnp.sum(s_b * q_col, axis=0, keepdims=True)
        v_new = _lane_pick(beta_row, lane1, b) * (v_scr[h, pl.ds(b, 1), :] - eg * k_s)
        y_scr[h, pl.ds(b, 1), :] = eg * q_s + _lane_pick(qk_row, lane1, b) * v_new
        so_ref[b] = s_b * eg + k_col * v_new
        return carry

    lax.fori_loop(0, bsz, step, 0, unroll=DECODE_UNROLL)

    @pl.when(h == n_heads - 1)
    def _():
        z = z_ref[...]
        outs = []
        for i in range(n_heads):
            sl = slice(i * head_dim, (i + 1) * head_dim)
            outs.append(_gdn_post_head(y_scr[i], z[:, sl], nw_ref[...]))
        o_ref[...] = jnp.concatenate(outs, axis=-1)


def _gdn_decode(qkv, small, z, conv_all, state_all, new_all, prm, *, layer, a_off, b_off):
    bsz = qkv.shape[0]
    n_heads, head_dim = state_all.shape[2], state_all.shape[3]
    width = n_heads * head_dim
    names = ("conv_w", "dtb_row", "alog_row", "norm_w")
    return _layer_state_call(
        functools.partial(_gdn_decode_body, n_heads=n_heads, head_dim=head_dim,
                          a_off=a_off, b_off=b_off), "gdn_decode",
        [qkv, small, z], [], state_all, new_all, [prm[n] for n in names],
        layer=layer, n_row_out=[width], tap_inputs=[conv_all],
        scratch_shapes=[pltpu.VMEM((2, n_heads, head_dim, bsz), F32),
                        pltpu.VMEM((n_heads, bsz, head_dim), F32),
                        pltpu.VMEM((3, SMALL_W, bsz), F32),
                        pltpu.VMEM((n_heads, bsz, head_dim), F32)])


def _segment_ones(width, seg):
    i = jnp.arange(2 * width)[:, None] % width
    j = jnp.arange(width)[None, :]
    return ((i // seg) == (j // seg)).astype(BF16)


def _pad_lanes(row, offset):
    out = jnp.zeros((SMALL_W,), F32)
    return lax.dynamic_update_slice(out, row.astype(F32), (offset,)).reshape(1, SMALL_W)


def _row_tile(seq_len, cap):
    best = 8
    for tm in range(8, cap + 1, 8):
        if seq_len % tm == 0:
            best = tm
    return best


def kernel(x_prompt, x_sample, state_rwkv_wkv, state_rwkv_shift, state_ssm, state_ssm_conv,
           state_gdn, state_gdn_conv, meta_tokens, norm_w, w_in, rw_mu, rw_w0, rw_w2, rw_a0,
           rw_a2, rw_k_k, rw_k_a, rw_r_k, rw_gn_w, rw_gn_b, ssm_conv_w, ssm_conv_b,
           ssm_dt_bias, ssm_a_log, ssm_d, ssm_norm_w, gdn_conv_w, gdn_dt_bias, gdn_a_log,
           gdn_norm_w, w_rw_out, w_ssm_out, w_gdn_out, w_out, final_norm_w):
    depth = w_in.shape[0]
    bp, seq, d_model = x_prompt.shape
    bd = x_sample.shape[0]
    n_meta = meta_tokens.shape[0]
    rw_heads, rw_hd = rw_r_k.shape[1], rw_r_k.shape[2]
    rw_w = rw_heads * rw_hd
    rw_shift_w = rw_mu.shape[1]
    ssm_heads, ssm_hd, ssm_state = state_ssm.shape[2], state_ssm.shape[3], state_ssm.shape[4]
    ssm_w = ssm_heads * ssm_hd
    ssm_conv_ch = ssm_conv_w.shape[2]
    ssm_groups = (ssm_conv_ch - ssm_w) // (2 * ssm_state)
    gdn_heads, gdn_hd = state_gdn.shape[2], state_gdn.shape[3]
    gdn_w = gdn_heads * gdn_hd
    gdn_conv_ch = gdn_conv_w.shape[2]
    a_off, b_off = ssm_heads, ssm_heads + gdn_heads
    assert b_off + gdn_heads <= SMALL_W

    off = 0
    cols = {}
    for name, wdt in (("rw", rw_shift_w), ("rw_z", rw_w), ("ssm_z", ssm_w), ("xbc", ssm_conv_ch),
                      ("dt", ssm_heads), ("qkv", gdn_conv_ch), ("gdn_z", gdn_w),
                      ("ga", gdn_heads), ("gb", gdn_heads), ("gate", 3 * d_model)):
        cols[name] = (off, off + wdt)
        off += wdt
    assert off == w_in.shape[2]

    seg_names = ("rw", "rw_z", "ssm_z", "xbc", "qkv", "gdn_z", "gate", "dt", "ga", "gb")
    n_small = ssm_heads + 2 * gdn_heads
    w_all = _pack_weights(jnp.swapaxes(w_in, 1, 2), [cols[n] for n in seg_names],
                          SMALL_W - n_small, kb=PACK_KB)
    widths = [cols[n][1] - cols[n][0] for n in seg_names[:7]] + [SMALL_W]
    assert all(w % LANE == 0 for w in widths)
    w_r, w_s, w_g, w_o = (w.astype(BF16) for w in (w_rw_out, w_ssm_out, w_gdn_out, w_out))

    def layer_params(l):
        rw = dict(mu=rw_mu[l][None], w0=rw_w0[l][None], w2=rw_w2[l], a0=rw_a0[l][None],
                  a2=rw_a2[l], k_k=rw_k_k[l][None], k_a=rw_k_a[l][None],
                  r_k=rw_r_k[l].reshape(1, rw_w), gn_w=rw_gn_w[l][None], gn_b=rw_gn_b[l][None])
        dtb = _pad_lanes(ssm_dt_bias[l], 0)
        alog = _pad_lanes(ssm_a_log[l], 0)
        sm = dict(conv_w=ssm_conv_w[l], conv_b=ssm_conv_b[l][None], dtb_row=dtb,
                  dtb_col=dtb.reshape(SMALL_W, 1), alog_row=alog,
                  alog_col=alog.reshape(SMALL_W, 1),
                  d=jnp.repeat(ssm_d[l], ssm_hd)[None], norm_w=ssm_norm_w[l][None])
        gdtb = _pad_lanes(gdn_dt_bias[l], a_off)
        galog = _pad_lanes(gdn_a_log[l], a_off)
        gd = dict(conv_w=gdn_conv_w[l], dtb_row=gdtb, dtb_col=gdtb.reshape(SMALL_W, 1),
                  alog_row=galog, alog_col=galog.reshape(SMALL_W, 1),
                  norm_w=gdn_norm_w[l][None])
        return rw, sm, gd

    fw = final_norm_w[None]

    real_len = n_meta + seq
    seq_p = -(-real_len // CHUNK) * CHUNK
    n_pad = seq_p - real_len
    n_chunks = seq_p // CHUNK
    n_sub = max(d for d in range(1, MAX_CHUNKS_PER_STEP + 1) if n_chunks % d == 0)
    n_steps = n_chunks // n_sub
    seqs_per_step = lambda cap: max(d for d in range(1, cap + 1) if bp % d == 0)
    meta = jnp.broadcast_to(meta_tokens.astype(F32)[None], (bp, n_meta, d_model))
    xp = jnp.concatenate([jnp.zeros((bp, n_pad, d_model), F32), meta, x_prompt], axis=1)
    xp = xp.reshape(bp * seq_p, d_model)
    tm_in = tm_out = _row_tile(seq_p, ROW_TILE_CAP)

    p_states = [[] for _ in range(6)]
    y_p = None
    for l in range(depth):
        rw_p, sm_p, gd_p = layer_params(l)
        p_rw, rw_z, ssm_z, xbc, qkv, gdn_z, gate, small = _inproj(
            xp, norm_w[l][None], w_all, widths, layer=l, tm=tm_in, n_pad=n_pad, seq_len=seq_p)
        rw_o, wkv, shift = _rwkv_prompt(p_rw, rw_z, rw_p, batch=bp, n_steps=n_steps,
                                        n_sub=n_sub, n_seq=seqs_per_step(RWKV_SEQS_PER_STEP),
                                        n_heads=rw_heads,
                                        head_dim=rw_hd)
        sm_o, ssm, sconv = _ssd_prompt(xbc, small, ssm_z, sm_p, batch=bp, n_steps=n_steps,
                                       n_sub=n_sub, n_seq=seqs_per_step(SSD_SEQS_PER_STEP),
                                       n_heads=ssm_heads,
                                       head_dim=ssm_hd, n_groups=ssm_groups, d_state=ssm_state,
                                       n_pad=n_pad)
        gd_o, gdn, gconv = _gdn_prompt(qkv, small, gdn_z, gd_p, batch=bp, n_steps=n_steps,
                                       n_sub=n_sub, n_seq=seqs_per_step(GDN_SEQS_PER_STEP),
                                       n_heads=gdn_heads,
                                       head_dim=gdn_hd, a_off=a_off, b_off=b_off)
        last = l == depth - 1
        res = _outproj(rw_o, sm_o, gd_o, gate, xp, w_r[l], w_s[l], w_g[l], w_o[l], fw,
                       tm=tm_out, with_final=last)
        xp = res[0]
        if last:
            y_p = res[1]
        for acc, s in zip(p_states, (wkv, shift.reshape(bp, rw_shift_w), ssm, sconv, gdn, gconv)):
            acc.append(s)
    y_prompt = y_p.reshape(bp, seq_p, d_model)[:, n_pad + n_meta:]

    xs = x_sample.reshape(bd, d_model)
    shift_all = state_rwkv_shift.reshape(depth * bd, -1)
    sconv_all = jnp.swapaxes(state_ssm_conv, 1, 2)
    gconv_all = jnp.swapaxes(state_gdn_conv, 1, 2)
    wkv_all = jnp.transpose(state_rwkv_wkv, (0, 2, 3, 4, 1))
    wkv_new, ssm_new, gdn_new = (jnp.zeros(a.shape, F32) for a in (wkv_all, state_ssm, state_gdn))
    small_states = [[] for _ in range(3)]
    y_s = None
    for l in range(depth):
        rw_p, sm_p, gd_p = layer_params(l)
        p_rw, rw_z, ssm_z, xbc, qkv, gdn_z, gate, small = _inproj(
            xs, norm_w[l][None], w_all, widths, layer=l, tm=bd, n_pad=0, seq_len=bd)
        rw_o, wkv_new = _rwkv_decode(p_rw, rw_z, shift_all, wkv_all, wkv_new, rw_p, layer=l)
        sm_o, sconv, ssm_new = _ssd_decode(xbc, small, ssm_z, sconv_all, state_ssm, ssm_new, sm_p,
                                           layer=l, n_groups=ssm_groups)
        gd_o, gconv, gdn_new = _gdn_decode(qkv, small, gdn_z, gconv_all, state_gdn, gdn_new, gd_p,
                                           layer=l, a_off=a_off, b_off=b_off)
        last = l == depth - 1
        res = _outproj(rw_o, sm_o, gd_o, gate, xs, w_r[l], w_s[l], w_g[l], w_o[l], fw,
                       tm=bd, with_final=last)
        xs = res[0]
        if last:
            y_s = res[1]
        for acc, s in zip(small_states, (p_rw, sconv, gconv)):
            acc.append(s)
    y_sample = y_s.reshape(bd, 1, d_model)
    s_shift, s_sconv, s_gconv = (jnp.stack(a) for a in small_states)
    s_states = (jnp.transpose(wkv_new, (0, 4, 1, 2, 3)), s_shift, ssm_new,
                jnp.swapaxes(s_sconv, 1, 2), gdn_new, jnp.swapaxes(s_gconv, 1, 2))

    return (y_prompt, y_sample) + tuple(jnp.stack(a) for a in p_states) + s_states
```

```python
import functools
import math

import jax
import jax.numpy as jnp
from jax import lax
from jax.experimental import pallas as pl
from jax.experimental.pallas import tpu as pltpu

F32 = jnp.float32
BF16 = jnp.bfloat16
HIGHEST = lax.Precision.HIGHEST

NORM_EPS = 1e-6
RW_GN_EPS = 64e-5
SSM_NORM_EPS = 1e-5
GDN_NORM_EPS = 1e-6
L2_EPS = 1e-6
CONV_W = 4
LANE = 128
SUBLANE = 8
CHUNK = 64
MAX_CHUNKS_PER_STEP = 3
RWKV_SEQS_PER_STEP = 4
SSD_SEQS_PER_STEP = 2
GDN_SEQS_PER_STEP = 4
ROW_TILE_CAP = 384
PACK_KB = 256
INV_BASE = 8
SEG_TILE = 256
DECODE_UNROLL = 8
SMALL_W = 128
CONV_HALO = 8
VMEM_LIMIT = 56 * 1024 * 1024


def _mm(a, b, exact=False):
    if exact:
        return jnp.dot(a, b, preferred_element_type=F32, precision=HIGHEST)
    return jnp.dot(a.astype(BF16), b.astype(BF16), preferred_element_type=F32)


def _mm_nt(a, b, exact=False):
    dn = (((1,), (1,)), ((), ()))
    if exact:
        return lax.dot_general(a, b, dn, preferred_element_type=F32, precision=HIGHEST)
    return lax.dot_general(a.astype(BF16), b.astype(BF16), dn, preferred_element_type=F32)


def _mm_tn(a, b, exact=False):
    dn = (((0,), (0,)), ((), ()))
    if exact:
        return lax.dot_general(a, b, dn, preferred_element_type=F32, precision=HIGHEST)
    return lax.dot_general(a.astype(BF16), b.astype(BF16), dn, preferred_element_type=F32)


def _sigmoid(x):
    return 1.0 / (1.0 + jnp.exp(-x))


def _silu(x):
    return x * _sigmoid(x)


def _softplus(x):
    return jnp.maximum(x, 0.0) + jnp.log(1.0 + jnp.exp(-jnp.abs(x)))


def _iota2(shape, dim):
    return lax.broadcasted_iota(jnp.int32, shape, dim)


def _seg_sum(x, seg_ones, split=False):
    w = seg_ones.shape[1]
    parts = []
    for i in range(x.shape[1] // w):
        xi = x[:, i * w:(i + 1) * w]
        x_hi = xi.astype(BF16)
        if split:
            x_lo = (xi - x_hi.astype(F32)).astype(BF16)
            parts.append(jnp.dot(jnp.concatenate([x_hi, x_lo], axis=1), seg_ones,
                                 preferred_element_type=F32))
        else:
            parts.append(jnp.dot(x_hi, seg_ones[:w], preferred_element_type=F32))
    return jnp.concatenate(parts, axis=1)


def _block_diag_pair(p):
    c = p.shape[0]
    lane = _iota2(p.shape, 1)
    pb = p.astype(BF16)
    zero = jnp.zeros_like(pb)
    return jnp.concatenate([jnp.where(lane < c, pb, zero), jnp.where(lane >= c, pb, zero)], axis=0)


def _mm_pair(x, p):
    return jnp.dot(x.astype(BF16), _block_diag_pair(p), preferred_element_type=F32)


def _unit_lower_inverse(l_mats):
    c = l_mats[0].shape[0]
    pairs = [jnp.concatenate(l_mats[i:i + 2], axis=1) for i in range(0, len(l_mats), 2)]
    ri = _iota2((c, 2 * c), 0)
    ci = _iota2((c, 2 * c), 1) & (c - 1)
    base_shift = int(math.log2(INV_BASE))
    same_base = (ri >> base_shift) == (ci >> base_shift)
    es = [jnp.where(same_base, l, 0.0) for l in pairs]
    ps = es
    n_terms = 1
    while 2 * n_terms < INV_BASE:
        ps = [_mm_pair(p, p) for p in ps]
        eps = [_mm_pair(e, p) for e, p in zip(es, ps)]
        es = [e + p + ep for e, p, ep in zip(es, ps, eps)]
        n_terms *= 2
    shift = base_shift
    while (1 << shift) < c:
        same_outer = (ri >> (shift + 1)) == (ci >> (shift + 1))
        other_inner = (ri >> shift) != (ci >> shift)
        l_off = [jnp.where(same_outer, jnp.where(other_inner, l, 0.0), 0.0) for l in pairs]
        ts = [lo + _mm_pair(lo, e) for lo, e in zip(l_off, es)]
        ets = [_mm_pair(e, t) for e, t in zip(es, ts)]
        es = [e + t + et for e, t, et in zip(es, ts, ets)]
        shift += 1
    out = []
    for e in es:
        out += [e[:, :c], e[:, c:]]
    return out


def _chunk_masks(c):
    ri = _iota2((c, c), 0)
    ci = _iota2((c, c), 1)
    return ri >= ci, ri > ci


def _block_tril(rows, c):
    shift = int(math.log2(c))
    ri = _iota2((rows, rows), 0)
    ci = _iota2((rows, rows), 1)
    return jnp.where((ri >> shift) == (ci >> shift),
                     jnp.where(ri >= ci, 1.0, 0.0), 0.0).astype(F32)


def _conv_chunk(ext_ref, u, w_ref, first_chunk):
    c = u.shape[0]

    @pl.when(first_chunk)
    def _():
        ext_ref[0:CONV_HALO, :] = jnp.zeros((CONV_HALO, u.shape[1]), F32)

    ext_ref[CONV_HALO:CONV_HALO + c, :] = u
    out = u * w_ref[CONV_W - 1:CONV_W, :]
    for i in range(CONV_W - 1):
        off = CONV_HALO - (CONV_W - 1) + i
        out = out + ext_ref[off:off + c, :] * w_ref[i:i + 1, :]
    tail = ext_ref[c:c + CONV_HALO, :]
    ext_ref[0:CONV_HALO, :] = tail
    return out, tail[CONV_HALO - (CONV_W - 1):, :]


def _inproj_body(x_ref, nw_ref, w_ref, *o_refs, n_pad, tiles_per_seq):
    x = x_ref[...]
    h = x * lax.rsqrt(jnp.mean(x * x, axis=-1, keepdims=True) + NORM_EPS) * nw_ref[...]
    if n_pad:
        first = lax.rem(pl.program_id(0), tiles_per_seq) == 0
        row = _iota2((x.shape[0], 1), 0)
        h = jnp.where(jnp.logical_and(first, row < n_pad), 0.0, h)
    hb = h.astype(BF16)
    off = 0
    for o_ref in o_refs:
        width = o_ref.shape[1]
        o_ref[...] = jnp.dot(hb, w_ref[:, off:off + width], preferred_element_type=F32)
        off += width


def _inproj(x, norm_w, w_all, widths, *, layer, tm, n_pad, seq_len):
    t, d = x.shape
    assert w_all.shape[2] == sum(widths)
    in_specs = [pl.BlockSpec((tm, d), lambda i: (i, 0)),
                pl.BlockSpec((1, d), lambda i: (0, 0)),
                pl.BlockSpec((None,) + w_all.shape[1:], lambda i: (layer, 0, 0),
                             pipeline_mode=pl.Buffered(1))]
    out_specs = [pl.BlockSpec((tm, w), lambda i: (i, 0)) for w in widths]
    out_shape = [jax.ShapeDtypeStruct((t, w), F32) for w in widths]
    body = functools.partial(_inproj_body, n_pad=n_pad,
                             tiles_per_seq=(seq_len // tm) if n_pad else 1)
    return pl.pallas_call(
        body, grid=(t // tm,), in_specs=in_specs, out_specs=out_specs, out_shape=out_shape,
        compiler_params=pltpu.CompilerParams(dimension_semantics=("arbitrary",),
                                             vmem_limit_bytes=VMEM_LIMIT),
        name="inproj",
    )(x, norm_w, w_all)


def _pack_weights_body(w_ref, o_ref, *, segments, pad):
    off = 0
    tail = []
    for lo, hi in segments:
        if (hi - lo) % LANE == 0 and not tail:
            o_ref[:, off:off + hi - lo] = w_ref[lo:hi, :].T.astype(o_ref.dtype)
            off += hi - lo
        else:
            tail.append(w_ref[lo:hi, :])
    if tail or pad:
        if pad:
            tail.append(jnp.zeros((pad, w_ref.shape[1]), w_ref.dtype))
        o_ref[:, off:] = jnp.concatenate(tail, axis=0).T.astype(o_ref.dtype)


def _pack_weights(w_t, segments, pad, *, kb):
    depth, n, d = w_t.shape
    total = sum(hi - lo for lo, hi in segments) + pad
    return pl.pallas_call(
        functools.partial(_pack_weights_body, segments=tuple(segments), pad=pad),
        grid=(depth, d // kb),
        in_specs=[pl.BlockSpec((None, n, kb), lambda l, i: (l, 0, i))],
        out_specs=pl.BlockSpec((None, kb, total), lambda l, i: (l, i, 0)),
        out_shape=jax.ShapeDtypeStruct((depth, d, total), BF16),
        compiler_params=pltpu.CompilerParams(dimension_semantics=("arbitrary", "arbitrary"),
                                             vmem_limit_bytes=VMEM_LIMIT),
        name="pack_weights",
    )(w_t)


def _outproj_body(rw_ref, sm_ref, gd_ref, gate_ref, x_ref, wr_ref, ws_ref, wg_ref, wo_ref,
                  fw_ref, o_ref, *maybe_y_ref, d_model):
    merged = None
    for b, (y_ref, w_ref) in enumerate(((rw_ref, wr_ref), (sm_ref, ws_ref), (gd_ref, wg_ref))):
        g = _sigmoid(gate_ref[:, b * d_model:(b + 1) * d_model])
        term = g * jnp.dot(y_ref[...].astype(BF16), w_ref[...], preferred_element_type=F32)
        merged = term if merged is None else merged + term
    x_new = x_ref[...] + jnp.dot(merged.astype(BF16), wo_ref[...], preferred_element_type=F32)
    o_ref[...] = x_new
    if maybe_y_ref:
        ms = jnp.mean(x_new * x_new, axis=-1, keepdims=True)
        maybe_y_ref[0][...] = x_new * lax.rsqrt(ms + NORM_EPS) * fw_ref[...]


def _outproj(rw, sm, gd, gate, x, w_r, w_s, w_g, w_o, final_w, *, tm, with_final):
    t, d = x.shape
    grid = (t // tm,)
    row = lambda i: (i, 0)
    fixed = lambda i: (0, 0)
    in_specs = [pl.BlockSpec((tm, rw.shape[1]), row), pl.BlockSpec((tm, sm.shape[1]), row),
                pl.BlockSpec((tm, gd.shape[1]), row), pl.BlockSpec((tm, gate.shape[1]), row),
                pl.BlockSpec((tm, d), row),
                pl.BlockSpec(w_r.shape, fixed), pl.BlockSpec(w_s.shape, fixed),
                pl.BlockSpec(w_g.shape, fixed), pl.BlockSpec(w_o.shape, fixed),
                pl.BlockSpec((1, d), fixed)]
    n_out = 2 if with_final else 1
    out_specs = [pl.BlockSpec((tm, d), row)] * n_out
    out_shape = [jax.ShapeDtypeStruct((t, d), F32)] * n_out
    return pl.pallas_call(
        functools.partial(_outproj_body, d_model=d),
        grid=grid, in_specs=in_specs, out_specs=out_specs, out_shape=out_shape,
        compiler_params=pltpu.CompilerParams(dimension_semantics=("arbitrary",),
                                             vmem_limit_bytes=VMEM_LIMIT),
        name="outproj",
    )(rw, sm, gd, gate, x, w_r, w_s, w_g, w_o, final_w)


def _rwkv_pre(p, shifted, mu, w0, w2, a0, a2, k_k, k_a, width, lora):
    u = p + (shifted - p) * mu
    r = u[:, :width]
    k = u[:, width:2 * width]
    v = u[:, 2 * width:3 * width]
    wd = u[:, 3 * width:3 * width + lora]
    ad = u[:, 3 * width + lora:]
    w = -_softplus(-(w0 + _mm(jnp.tanh(wd), w2))) - 0.5
    log_decay = -jnp.exp(w)
    a = _sigmoid(a0 + _mm(ad, a2))
    kk_raw = k * k_k
    k_mod = k * (1.0 + (a - 1.0) * k_a)
    return r, k_mod, v, log_decay, a, kk_raw


def _l2norm(x):
    return x * lax.rsqrt(jnp.sum(x * x, axis=-1, keepdims=True) + L2_EPS)


def _rwkv_post_head(y, r, k_mod, v, z, r_k, gn_w, gn_b):
    mean = jnp.mean(y, axis=-1, keepdims=True)
    var = jnp.mean(jnp.square(y - mean), axis=-1, keepdims=True)
    yn = (y - mean) * lax.rsqrt(var + RW_GN_EPS) * gn_w + gn_b
    bonus = jnp.sum(r * k_mod * r_k, axis=-1, keepdims=True) * v
    return (yn + bonus) * _silu(z)


def _rwkv_prompt_body(p_ref, z_ref, ones_ref, mu_ref, w0_ref, w2_ref, a0_ref, a2_ref, kk_ref,
                      ka_ref, rk_ref, gnw_ref, gnb_ref, o_ref, wkv_ref, shift_ref, prev_scr,
                      *, n_heads, head_dim, n_sub):
    width = n_heads * head_dim
    lora = w2_ref.shape[0]
    n_seq, rows, shift_w = p_ref.shape
    c = rows // n_sub
    p = p_ref[...].reshape(n_seq * rows, shift_w)

    @pl.when(pl.program_id(1) == 0)
    def _():
        wkv_ref[...] = jnp.zeros(wkv_ref.shape, F32)
        prev_scr[...] = jnp.zeros(prev_scr.shape, F32)

    row = _iota2((n_seq * rows, 1), 0)
    shifted = pltpu.roll(p, 1, 0)
    for s in range(n_seq):
        shifted = jnp.where(row == s * rows, prev_scr[s], shifted)
        last_row = p[(s + 1) * rows - 1:(s + 1) * rows, :]
        prev_scr[s] = last_row
        shift_ref[s] = last_row

    r, k_mod, v, log_decay, a, kk_raw = _rwkv_pre(
        p, shifted, mu_ref[...], w0_ref[...], w2_ref[...], a0_ref[...], a2_ref[...],
        kk_ref[...], ka_ref[...], width, lora)
    tril = _block_tril(rows, c)
    cum = jnp.concatenate([_mm(tril, log_decay[s * rows:(s + 1) * rows], exact=True)
                           for s in range(n_seq)], axis=0)
    e_pos = jnp.exp(cum)
    e_neg = jnp.exp(-cum)
    e_prev = jnp.exp(cum - log_decay)
    z = z_ref[...].reshape(n_seq * rows, width)
    incl, strict = _chunk_masks(c)

    hs = range(n_heads)
    sls = [slice(h * head_dim, (h + 1) * head_dim) for h in hs]
    n_chunks = n_seq * n_sub
    units = [(j, h) for j in range(n_chunks) for h in hs]
    rsl = [slice(j * c, (j + 1) * c) for j in range(n_chunks)]
    seg_ones = ones_ref[...]
    kk = kk_raw * lax.rsqrt(_seg_sum(kk_raw * kk_raw, seg_ones) + L2_EPS)
    a_all = -kk * e_prev
    b_all = kk * a * e_neg
    k_all = k_mod * e_neg
    r_all = r * e_pos
    a_t = [a_all[:, sl] for sl in sls]
    b_t = [b_all[:, sl] for sl in sls]
    k_t = [k_all[:, sl] for sl in sls]
    r_t = [r_all[:, sl] for sl in sls]
    v_h = [v[:, sl] for sl in sls]

    sc = {u: _mm_nt(jnp.concatenate([a_t[u[1]][rsl[u[0]]], r_t[u[1]][rsl[u[0]]]], axis=0),
                    jnp.concatenate([b_t[u[1]][rsl[u[0]]], k_t[u[1]][rsl[u[0]]]], axis=0))
          for u in units}
    l_ab = {u: jnp.where(strict, sc[u][:c, :c], 0.0) for u in units}
    l_ak = {u: jnp.where(strict, sc[u][:c, c:], 0.0) for u in units}
    m_rb = {u: jnp.where(incl, sc[u][c:, :c], 0.0) for u in units}
    m_rk = {u: jnp.where(incl, sc[u][c:, c:], 0.0) for u in units}
    lv = {u: _mm(jnp.concatenate([l_ak[u], m_rk[u]], axis=0), v_h[u[1]][rsl[u[0]]]) for u in units}
    inv = dict(zip(units, _unit_lower_inverse([l_ab[u] for u in units])))
    rhs = {u: jnp.concatenate([a_t[u[1]][rsl[u[0]]], lv[u][:c]], axis=1) for u in units}
    wu = {u: rhs[u] + _mm(inv[u], rhs[u]) for u in units}
    d_end = {(j, h): e_pos[(j + 1) * c - 1:(j + 1) * c, sls[h]] for j, h in units}
    qy = {u: _mm(m_rb[u], wu[u]) for u in units}
    mn = {u: _mm_tn(wu[u], b_t[u[1]][rsl[u[0]]] * d_end[u]) for u in units}
    n2 = {u: _mm_tn(v_h[u[1]][rsl[u[0]]], k_t[u[1]][rsl[u[0]]] * d_end[u]) for u in units}

    sh = [(s, h) for s in range(n_seq) for h in hs]
    states = {(s, h): wkv_ref[s, h] for s, h in sh}
    ys = {k: [] for k in sh}
    for jj in range(n_sub):
        ju = {(s, h): (s * n_sub + jj, h) for s, h in sh}
        y_dot = {k: _mm_nt(r_t[k[1]][rsl[ju[k][0]]] + qy[ju[k]][:, :c], states[k]) for k in sh}
        s_dot = {k: _mm(states[k], mn[ju[k]][:c]) for k in sh}
        for k in sh:
            u = ju[k]
            ys[k].append(y_dot[k] + qy[u][:, c:] + lv[u][c:])
            states[k] = states[k] * d_end[u] + s_dot[k] + (mn[u][c:] + n2[u])
    for s, h in sh:
        wkv_ref[s, h] = states[(s, h)]
    y = jnp.concatenate([jnp.concatenate([t for s in range(n_seq) for t in ys[(s, h)]], axis=0)
                         for h in hs], axis=-1)
    inv_n = 1.0 / head_dim
    dev = y - _seg_sum(y, seg_ones) * inv_n
    var = _seg_sum(dev * dev, seg_ones) * inv_n
    yn = dev * lax.rsqrt(var + RW_GN_EPS) * gnw_ref[...] + gnb_ref[...]
    bonus = _seg_sum(r * k_mod * rk_ref[...], seg_ones, split=True) * v
    o_ref[...] = ((yn + bonus) * _silu(z)).reshape(n_seq, rows, width)


def _seq_specs(n_seq, rows, n_steps, arrays):
    views = [a.reshape(-1, n_steps * rows, a.shape[1]) for a in arrays]
    specs = [pl.BlockSpec((n_seq, rows, a.shape[1]), lambda b, c: (b, c, 0)) for a in arrays]
    return views, specs


def _rwkv_prompt(p_rw, z, prm, *, batch, n_steps, n_sub, n_seq, n_heads, head_dim):
    t, shift_w = p_rw.shape
    width = n_heads * head_dim
    rows = n_sub * CHUNK
    fixed = lambda b, c: (0, 0)
    names = ("mu", "w0", "w2", "a0", "a2", "k_k", "k_a", "r_k", "gn_w", "gn_b")
    params = [_segment_ones(SEG_TILE, head_dim)] + [prm[n] for n in names]
    views, in_specs = _seq_specs(n_seq, rows, n_steps, [p_rw, z])
    in_specs += [pl.BlockSpec(a.shape, fixed) for a in params]
    out_specs = [pl.BlockSpec((n_seq, rows, width), lambda b, c: (b, c, 0)),
                 pl.BlockSpec((n_seq, n_heads, head_dim, head_dim), lambda b, c: (b, 0, 0, 0)),
                 pl.BlockSpec((n_seq, 1, shift_w), lambda b, c: (b, 0, 0))]
    out_shape = [jax.ShapeDtypeStruct((batch, n_steps * rows, width), F32),
                 jax.ShapeDtypeStruct((batch, n_heads, head_dim, head_dim), F32),
                 jax.ShapeDtypeStruct((batch, 1, shift_w), F32)]
    o, wkv, shift = pl.pallas_call(
        functools.partial(_rwkv_prompt_body, n_heads=n_heads, head_dim=head_dim, n_sub=n_sub),
        grid=(batch // n_seq, n_steps), in_specs=in_specs, out_specs=out_specs,
        out_shape=out_shape,
        scratch_shapes=[pltpu.VMEM((n_seq, 1, shift_w), F32)],
        compiler_params=pltpu.CompilerParams(dimension_semantics=("arbitrary", "arbitrary"),
                                             vmem_limit_bytes=VMEM_LIMIT),
        name="rwkv_prompt",
    )(*views, *params)
    return o.reshape(t, width), wkv, shift


def _ssm_gated_norm(y, z, norm_w, n_groups):
    g = y * _silu(z)
    gw = g.shape[1] // n_groups
    parts = []
    for i in range(n_groups):
        gi = g[:, i * gw:(i + 1) * gw]
        parts.append(gi * lax.rsqrt(jnp.mean(gi * gi, axis=-1, keepdims=True) + SSM_NORM_EPS))
    return jnp.concatenate(parts, axis=-1) * norm_w


def _ssd_prompt_body(xbc_ref, small_ref, z_ref, cw_ref, cb_ref, dtb_row_ref, dtb_col_ref,
                     alog_row_ref, alog_col_ref, d_ref, nw_ref, o_ref, h_ref, conv_ref, ext_scr,
                     *, n_heads, head_dim, n_groups, d_state, n_pad, n_sub):
    first = pl.program_id(1) == 0
    width = n_heads * head_dim
    hpg = n_heads // n_groups
    gw = hpg * head_dim
    n_seq, rows, _ = xbc_ref.shape
    c = rows // n_sub
    n_chunks = n_seq * n_sub
    rsl = [slice(j * c, (j + 1) * c) for j in range(n_chunks)]

    @pl.when(first)
    def _():
        h_ref[...] = jnp.zeros(h_ref.shape, F32)

    convs = []
    for s in range(n_seq):
        conv, tail = _conv_chunk(ext_scr.at[s], xbc_ref[s], cw_ref, first)
        conv_ref[s] = tail
        convs.append(conv)
    xbc = _silu(jnp.concatenate(convs, axis=0) + cb_ref[...])
    xs = xbc[:, :width]
    b_all = xbc[:, width:width + n_groups * d_state]
    c_all = xbc[:, width + n_groups * d_state:]

    small = small_ref[...].reshape(n_seq * rows, SMALL_W)
    lane = _iota2((1, SMALL_W), 1)
    sub = _iota2((SMALL_W, 1), 0)
    dt = _softplus(small + dtb_row_ref[...])
    assert n_heads <= SUBLANE
    dt_t = [_softplus(small[sl].T[:SUBLANE] + dtb_col_ref[0:SUBLANE, :]) for sl in rsl]
    if n_pad:
        assert n_pad <= c
        not_first = jnp.logical_not(first)
        row = _iota2((n_seq * rows, 1), 0)
        for s in range(n_seq):
            in_pad = jnp.logical_and(row >= s * rows, row < s * rows + n_pad)
            dt = jnp.where(jnp.logical_and(first, in_pad), 0.0, dt)
        for s in range(n_seq):
            j0 = s * n_sub
            dt_t[j0] = jnp.where(jnp.logical_or(not_first, _iota2((1, c), 1) >= n_pad),
                                 dt_t[j0], 0.0)
    a_row = jnp.where(lane < n_heads, -jnp.exp(alog_row_ref[...]), 0.0)
    a_col = jnp.where(sub[:SUBLANE] < n_heads, -jnp.exp(alog_col_ref[0:SUBLANE, :]), 0.0)
    incl, _ = _chunk_masks(c)
    triu = jnp.where(_iota2((c, c), 0) <= _iota2((c, c), 1), 1.0, 0.0).astype(F32)
    tril = _block_tril(rows, c)
    da = dt * a_row
    acs = jnp.concatenate([_mm(tril, da[s * rows:(s + 1) * rows], exact=True)
                           for s in range(n_seq)], axis=0)
    acs_t = [_mm(t * a_col, triu, exact=True) for t in dt_t]
    expand = jnp.where((_iota2((SMALL_W, width), 1) // head_dim) == _iota2((SMALL_W, width), 0),
                       1.0, 0.0).astype(F32)
    wide = _mm(jnp.concatenate([acs, dt], axis=0), expand, exact=True)
    acs_w = wide[:n_seq * rows]
    xdt = xs * wide[n_seq * rows:]
    e_acs = jnp.exp(acs_w)

    hs = range(n_heads)
    gs = range(n_groups)
    js = range(n_chunks)
    b_g = {(j, g): b_all[rsl[j], g * d_state:(g + 1) * d_state] for j in js for g in gs}
    c_g = {(j, g): c_all[rsl[j], g * d_state:(g + 1) * d_state] for j in js for g in gs}
    a_last = {j: acs_w[(j + 1) * c - 1:(j + 1) * c] for j in js}
    xdt_end = {j: xdt[rsl[j]] * jnp.exp(a_last[j] - acs_w[rsl[j]]) for j in js}

    cb = {jg: _mm_nt(c_g[jg], b_g[jg]) for jg in b_g}
    seg = {(j, h): jnp.exp(jnp.where(incl, acs[rsl[j], h:h + 1] - acs_t[j][h:h + 1, :], -jnp.inf))
           for j in js for h in hs}
    y_diag = {(j, h): _mm(cb[(j, h // hpg)] * seg[(j, h)],
                          xdt[rsl[j], h * head_dim:(h + 1) * head_dim])
              for j in js for h in hs}
    st = {(j, g): _mm_tn(xdt_end[j][:, g * gw:(g + 1) * gw], b_g[(j, g)]) for j in js for g in gs}

    sg = [(s, g) for s in range(n_seq) for g in gs]
    states = {(s, g): h_ref[s, g * hpg:(g + 1) * hpg].reshape(gw, d_state) for s, g in sg}
    ys = {s: [] for s in range(n_seq)}
    for jj in range(n_sub):
        y_off = {(s, g): _mm_nt(c_g[(s * n_sub + jj, g)], states[(s, g)]) for s, g in sg}
        for s in range(n_seq):
            j = s * n_sub + jj
            ys[s].append(jnp.concatenate([y_off[(s, g)] for g in gs], axis=-1) * e_acs[rsl[j]]
                         + jnp.concatenate([y_diag[(j, h)] for h in hs], axis=-1))
            for g in gs:
                scaled = [states[(s, g)][i * head_dim:(i + 1) * head_dim]
                          * jnp.exp(acs[(j + 1) * c - 1:(j + 1) * c, g * hpg + i:g * hpg + i + 1])
                          for i in range(hpg)]
                states[(s, g)] = jnp.concatenate(scaled, axis=0) + st[(j, g)]
    for s, g in sg:
        h_ref[s, g * hpg:(g + 1) * hpg] = states[(s, g)].reshape(hpg, head_dim, d_state)
    y_all = jnp.concatenate([t for s in range(n_seq) for t in ys[s]], axis=0) + xs * d_ref[...]
    z = z_ref[...].reshape(n_seq * rows, width)
    o_ref[...] = _ssm_gated_norm(y_all, z, nw_ref[...], n_groups).reshape(n_seq, rows, width)


def _ssd_prompt(xbc, small, z, prm, *, batch, n_steps, n_sub, n_seq, n_heads, head_dim, n_groups,
                d_state, n_pad):
    t, conv_ch = xbc.shape
    width = n_heads * head_dim
    rows = n_sub * CHUNK
    fixed = lambda b, c: (0, 0)
    names = ("conv_w", "conv_b", "dtb_row", "dtb_col", "alog_row", "alog_col", "d", "norm_w")
    params = [prm[n] for n in names]
    views, in_specs = _seq_specs(n_seq, rows, n_steps, [xbc, small, z])
    in_specs += [pl.BlockSpec(a.shape, fixed) for a in params]
    out_specs = [pl.BlockSpec((n_seq, rows, width), lambda b, c: (b, c, 0)),
                 pl.BlockSpec((n_seq, n_heads, head_dim, d_state), lambda b, c: (b, 0, 0, 0)),
                 pl.BlockSpec((n_seq, CONV_W - 1, conv_ch), lambda b, c: (b, 0, 0))]
    out_shape = [jax.ShapeDtypeStruct((batch, n_steps * rows, width), F32),
                 jax.ShapeDtypeStruct((batch, n_heads, head_dim, d_state), F32),
                 jax.ShapeDtypeStruct((batch, CONV_W - 1, conv_ch), F32)]
    o, state, conv = pl.pallas_call(
        functools.partial(_ssd_prompt_body, n_heads=n_heads, head_dim=head_dim,
                          n_groups=n_groups, d_state=d_state, n_pad=n_pad, n_sub=n_sub),
        grid=(batch // n_seq, n_steps), in_specs=in_specs, out_specs=out_specs,
        out_shape=out_shape,
        scratch_shapes=[pltpu.VMEM((n_seq, rows + CONV_HALO, conv_ch), F32)],
        compiler_params=pltpu.CompilerParams(dimension_semantics=("arbitrary", "arbitrary"),
                                             vmem_limit_bytes=VMEM_LIMIT),
        name="ssd_prompt",
    )(*views, *params)
    return o.reshape(t, width), state, conv


def _gdn_post_head(o, z, norm_w):
    return o * lax.rsqrt(jnp.mean(o * o, axis=-1, keepdims=True) + GDN_NORM_EPS) * norm_w * _silu(z)


def _gdn_prompt_body(qkv_ref, small_ref, z_ref, cw_ref, dtb_row_ref, dtb_col_ref, alog_row_ref,
                     alog_col_ref, nw_ref, o_ref, s_ref, conv_ref, ext_scr,
                     *, n_heads, head_dim, a_off, b_off, n_sub):
    first = pl.program_id(1) == 0
    width = n_heads * head_dim
    n_seq, rows, _ = qkv_ref.shape
    c = rows // n_sub
    n_chunks = n_seq * n_sub

    @pl.when(first)
    def _():
        s_ref[...] = jnp.zeros(s_ref.shape, F32)

    convs = []
    for s in range(n_seq):
        conv, tail = _conv_chunk(ext_scr.at[s], qkv_ref[s], cw_ref, first)
        conv_ref[s] = tail
        convs.append(conv)
    qkv = _silu(jnp.concatenate(convs, axis=0))
    small = small_ref[...].reshape(n_seq * rows, SMALL_W)
    rsl = [slice(j * c, (j + 1) * c) for j in range(n_chunks)]
    small_t = [small[sl].T for sl in rsl]
    lane = _iota2((1, SMALL_W), 1)
    sub = _iota2((SMALL_W, 1), 0)
    g = jnp.where(jnp.logical_and(lane >= a_off, lane < a_off + n_heads),
                  -jnp.exp(alog_row_ref[...]) * _softplus(small + dtb_row_ref[...]), 0.0)
    g_t = [jnp.where(jnp.logical_and(sub >= a_off, sub < a_off + n_heads),
                     -jnp.exp(alog_col_ref[...]) * _softplus(st + dtb_col_ref[...]), 0.0)
           for st in small_t]
    beta = _sigmoid(small)
    incl, strict = _chunk_masks(c)
    triu = jnp.where(_iota2((c, c), 0) <= _iota2((c, c), 1), 1.0, 0.0).astype(F32)
    tril = _block_tril(rows, c)
    gcs = jnp.concatenate([_mm(tril, g[s * rows:(s + 1) * rows], exact=True)
                           for s in range(n_seq)], axis=0)
    gcs_t = [_mm(gt, triu, exact=True) for gt in g_t]
    z = z_ref[...].reshape(n_seq * rows, width)
    scale = head_dim ** -0.5

    hs = range(n_heads)
    sls = [slice(h * head_dim, (h + 1) * head_dim) for h in hs]
    units = [(j, h) for j in range(n_chunks) for h in hs]
    q_h = [_l2norm(qkv[:, sl]) * scale for sl in sls]
    k_h = [_l2norm(qkv[:, width + h * head_dim:width + (h + 1) * head_dim]) for h in hs]
    v_h = [qkv[:, 2 * width + h * head_dim:2 * width + (h + 1) * head_dim] for h in hs]
    g_c = {(j, h): gcs[rsl[j], a_off + h:a_off + h + 1] for j, h in units}
    g_r = {(j, h): gcs_t[j][a_off + h:a_off + h + 1, :] for j, h in units}
    g_last = {(j, h): gcs[(j + 1) * c - 1:(j + 1) * c, a_off + h:a_off + h + 1] for j, h in units}
    b_c = {(j, h): beta[rsl[j], b_off + h:b_off + h + 1] for j, h in units}
    q = {(j, h): q_h[h][rsl[j]] for j, h in units}
    k = {(j, h): k_h[h][rsl[j]] for j, h in units}
    v = {(j, h): v_h[h][rsl[j]] for j, h in units}

    decay = {u: jnp.exp(jnp.where(incl, g_c[u] - g_r[u], -jnp.inf)) for u in units}
    kb = {u: k[u] * b_c[u] for u in units}
    e_c = {u: jnp.exp(g_c[u]) for u in units}
    sc = {u: _mm_nt(jnp.concatenate([kb[u], q[u]], axis=0), k[u]) for u in units}
    a_mat = {u: jnp.where(strict, sc[u][:c] * decay[u], 0.0) for u in units}
    qk = {u: jnp.where(incl, sc[u][c:] * decay[u], 0.0) for u in units}
    inv = dict(zip(units, _unit_lower_inverse([-a_mat[u] for u in units])))
    rhs = {u: jnp.concatenate([v[u] * b_c[u], kb[u] * e_c[u]], axis=1) for u in units}
    uw = {u: rhs[u] + _mm(inv[u], rhs[u]) for u in units}
    qo = {u: _mm(qk[u], uw[u]) for u in units}
    mn = {u: _mm_tn(k[u] * jnp.exp(g_last[u] - g_c[u]), uw[u]) for u in units}

    sh = [(s, h) for s in range(n_seq) for h in hs]
    states = {(s, h): s_ref[s, h] for s, h in sh}
    os_ = {k_: [] for k_ in sh}
    for jj in range(n_sub):
        ju = {(s, h): (s * n_sub + jj, h) for s, h in sh}
        o_dot = {k_: _mm(q[ju[k_]] * e_c[ju[k_]] - qo[ju[k_]][:, head_dim:], states[k_])
                 for k_ in sh}
        s_dot = {k_: _mm(mn[ju[k_]][:, head_dim:], states[k_]) for k_ in sh}
        for k_ in sh:
            u = ju[k_]
            os_[k_].append(o_dot[k_] + qo[u][:, :head_dim])
            states[k_] = states[k_] * jnp.exp(g_last[u]) - s_dot[k_] + mn[u][:, :head_dim]
    for s, h in sh:
        s_ref[s, h] = states[(s, h)]
    outs = [_gdn_post_head(jnp.concatenate([t for s in range(n_seq) for t in os_[(s, h)]], axis=0),
                           z[:, sls[h]], nw_ref[...]) for h in hs]
    o_ref[...] = jnp.concatenate(outs, axis=-1).reshape(n_seq, rows, width)


def _gdn_prompt(qkv, small, z, prm, *, batch, n_steps, n_sub, n_seq, n_heads, head_dim, a_off,
                b_off):
    t, conv_ch = qkv.shape
    width = n_heads * head_dim
    rows = n_sub * CHUNK
    fixed = lambda b, c: (0, 0)
    names = ("conv_w", "dtb_row", "dtb_col", "alog_row", "alog_col", "norm_w")
    params = [prm[n] for n in names]
    views, in_specs = _seq_specs(n_seq, rows, n_steps, [qkv, small, z])
    in_specs += [pl.BlockSpec(a.shape, fixed) for a in params]
    out_specs = [pl.BlockSpec((n_seq, rows, width), lambda b, c: (b, c, 0)),
                 pl.BlockSpec((n_seq, n_heads, head_dim, head_dim), lambda b, c: (b, 0, 0, 0)),
                 pl.BlockSpec((n_seq, CONV_W - 1, conv_ch), lambda b, c: (b, 0, 0))]
    out_shape = [jax.ShapeDtypeStruct((batch, n_steps * rows, width), F32),
                 jax.ShapeDtypeStruct((batch, n_heads, head_dim, head_dim), F32),
                 jax.ShapeDtypeStruct((batch, CONV_W - 1, conv_ch), F32)]
    o, state, conv = pl.pallas_call(
        functools.partial(_gdn_prompt_body, n_heads=n_heads, head_dim=head_dim,
                          a_off=a_off, b_off=b_off, n_sub=n_sub),
        grid=(batch // n_seq, n_steps), in_specs=in_specs, out_specs=out_specs,
        out_shape=out_shape,
        scratch_shapes=[pltpu.VMEM((n_seq, rows + CONV_HALO, conv_ch), F32)],
        compiler_params=pltpu.CompilerParams(dimension_semantics=("arbitrary", "arbitrary"),
                                             vmem_limit_bytes=VMEM_LIMIT),
        name="gdn_prompt",
    )(*views, *params)
    return o.reshape(t, width), state, conv


def _conv_step(u, conv_ref, convo_ref, w_ref):
    out = u * w_ref[CONV_W - 1:CONV_W, :]
    for i in range(CONV_W - 1):
        out = out + conv_ref[i] * w_ref[i:i + 1, :]
    for i in range(CONV_W - 2):
        convo_ref[i] = conv_ref[i + 1]
    convo_ref[CONV_W - 2] = u
    return out


def _lane_pick(x, lane, idx):
    return jnp.sum(jnp.where(lane == idx, x, 0.0), axis=1, keepdims=True)


def _rwkv_decode_body(p_ref, z_ref, prev_ref, s_ref, mu_ref, w0_ref, w2_ref, a0_ref, a2_ref,
                      kk_ref, ka_ref, rk_ref, gnw_ref, gnb_ref, o_ref, so_ref,
                      t_scr, y_scr, row_scr, *, n_heads, head_dim):
    h = pl.program_id(0)
    width = n_heads * head_dim
    lora = w2_ref.shape[0]
    bsz = p_ref.shape[0]

    @pl.when(h == 0)
    def _():
        r, k_mod, v, log_decay, a, kk_raw = _rwkv_pre(
            p_ref[...], prev_ref[...], mu_ref[...], w0_ref[...], w2_ref[...], a0_ref[...],
            a2_ref[...], kk_ref[...], ka_ref[...], width, lora)
        kk = jnp.concatenate(
            [_l2norm(kk_raw[:, i * head_dim:(i + 1) * head_dim]) for i in range(n_heads)], axis=-1)
        row_scr[0] = r
        row_scr[1] = k_mod
        row_scr[2] = v
        for i, arr in enumerate((r, jnp.exp(log_decay), k_mod, v, -kk, kk * a)):
            t_scr[i] = arr.T.reshape(n_heads, head_dim, bsz)

    r_t = t_scr[0, h]
    w_t = t_scr[1, h]
    k_t = t_scr[2, h]
    a_t = t_scr[4, h]
    b_t = t_scr[5, h]

    def step(i, carry):
        s_v = s_ref[i]
        sa = jnp.sum(s_v * a_t, axis=0, keepdims=True)
        v_i = t_scr[3, h, pl.ds(i, 1), :]
        s_new = s_v * w_t + sa * b_t + v_i * k_t
        so_ref[i] = s_new
        y_scr[h, pl.ds(i, 1), :] = jnp.sum(s_new * r_t, axis=0, keepdims=True)
        return carry

    lax.fori_loop(0, head_dim, step, 0, unroll=DECODE_UNROLL)

    @pl.when(h == n_heads - 1)
    def _():
        y = y_scr[...].reshape(width, bsz).T
        r, k_mod, v, z = row_scr[0], row_scr[1], row_scr[2], z_ref[...]
        outs = []
        for i in range(n_heads):
            sl = slice(i * head_dim, (i + 1) * head_dim)
            outs.append(_rwkv_post_head(y[:, sl], r[:, sl], k_mod[:, sl], v[:, sl], z[:, sl],
                                        rk_ref[:, sl], gnw_ref[:, sl], gnb_ref[:, sl]))
        o_ref[...] = jnp.concatenate(outs, axis=-1)


def _layer_state_call(body, name, row_inputs, layer_inputs, state_all, new_all, params, *,
                      layer, n_row_out, scratch_shapes, batch_minor=False, tap_inputs=()):
    bsz = row_inputs[0].shape[0]
    fixed = lambda h: (0, 0)
    at_layer = lambda h: (layer, 0)
    if batch_minor:
        n_heads = state_all.shape[1]
        state_spec = pl.BlockSpec((None, None) + state_all.shape[2:],
                                  lambda h: (layer, h, 0, 0, 0))
    else:
        n_heads = state_all.shape[2]
        state_spec = pl.BlockSpec((None, bsz, None) + state_all.shape[3:],
                                  lambda h: (layer, 0, h, 0, 0))
    in_specs = [pl.BlockSpec(a.shape, fixed) for a in row_inputs]
    in_specs += [pl.BlockSpec((bsz, a.shape[1]), at_layer) for a in layer_inputs]
    in_specs += [pl.BlockSpec((None,) + a.shape[1:], lambda h: (layer, 0, 0, 0))
                 for a in tap_inputs]
    in_specs += [state_spec]
    in_specs += [pl.BlockSpec(a.shape, fixed) for a in params]
    in_specs.append(pl.BlockSpec(memory_space=pl.ANY))
    operands = (list(row_inputs) + list(layer_inputs) + list(tap_inputs) + [state_all]
                + list(params) + [new_all])
    n_in = len(operands)
    aliases = {n_in - 1: len(n_row_out) + len(tap_inputs)}
    inner = body

    def body(*refs):
        return inner(*refs[:n_in - 1], *refs[n_in:])

    out_specs = [pl.BlockSpec((bsz, w), fixed) for w in n_row_out]
    out_specs += [pl.BlockSpec(a.shape[1:], lambda h: (0, 0, 0)) for a in tap_inputs]
    out_specs += [state_spec]
    out_shape = [jax.ShapeDtypeStruct((bsz, w), F32) for w in n_row_out]
    out_shape += [jax.ShapeDtypeStruct(a.shape[1:], F32) for a in tap_inputs]
    out_shape += [jax.ShapeDtypeStruct(state_all.shape, F32)]
    return pl.pallas_call(
        body, grid=(n_heads,), in_specs=in_specs, out_specs=out_specs, out_shape=out_shape,
        scratch_shapes=scratch_shapes, input_output_aliases=aliases,
        compiler_params=pltpu.CompilerParams(dimension_semantics=("arbitrary",),
                                             vmem_limit_bytes=VMEM_LIMIT),
        name=name,
    )(*operands)


def _rwkv_decode(p_rw, z, prev_all, state_all, new_all, prm, *, layer):
    bsz = p_rw.shape[0]
    n_heads, head_dim = state_all.shape[1], state_all.shape[2]
    width = n_heads * head_dim
    names = ("mu", "w0", "w2", "a0", "a2", "k_k", "k_a", "r_k", "gn_w", "gn_b")
    return _layer_state_call(
        functools.partial(_rwkv_decode_body, n_heads=n_heads, head_dim=head_dim), "rwkv_decode",
        [p_rw, z], [prev_all], state_all, new_all, [prm[n] for n in names],
        layer=layer, n_row_out=[width], batch_minor=True,
        scratch_shapes=[pltpu.VMEM((6, n_heads, head_dim, bsz), F32),
                        pltpu.VMEM((n_heads, head_dim, bsz), F32),
                        pltpu.VMEM((3, bsz, width), F32)])


def _ssd_decode_body(xbc_ref, small_ref, z_ref, conv_ref, s_ref, cw_ref, cb_ref, dtb_row_ref,
                     alog_row_ref, d_ref, nw_ref, o_ref, convo_ref, so_ref,
                     xt_scr, bc_scr, dec_scr, y_scr, skip_scr,
                     *, n_heads, head_dim, n_groups, d_state):
    h = pl.program_id(0)
    width = n_heads * head_dim
    bsz = xbc_ref.shape[0]
    hpg = n_heads // n_groups

    @pl.when(h == 0)
    def _():
        conv = _conv_step(xbc_ref[...], conv_ref, convo_ref, cw_ref)
        xbc = _silu(conv + cb_ref[...])
        lane_s = _iota2((1, SMALL_W), 1)
        dt = _softplus(small_ref[...] + dtb_row_ref[...])
        a_row = jnp.where(lane_s < n_heads, -jnp.exp(alog_row_ref[...]), 0.0)
        dec_scr[...] = jnp.exp(dt * a_row).T
        expand = jnp.where((_iota2((SMALL_W, width), 1) // head_dim) == _iota2((SMALL_W, width), 0),
                           1.0, 0.0).astype(F32)
        xs = xbc[:, :width]
        xt_scr[...] = (xs * _mm(dt, expand, exact=True)).T.reshape(n_heads, head_dim, bsz)
        skip_scr[...] = xs * d_ref[...]
        for i in range(2 * n_groups):
            bc_scr[i] = xbc[:, width + i * d_state:width + (i + 1) * d_state]

    g = h // hpg
    xdt_t = xt_scr[h]
    dec_row = dec_scr[pl.ds(h, 1), :]
    lane = _iota2((head_dim, bsz), 1)
    lane1 = _iota2((1, bsz), 1)

    def step(b, y_t):
        b_row = bc_scr[g, pl.ds(b, 1), :]
        c_row = bc_scr[n_groups + g, pl.ds(b, 1), :]
        s_new = (s_ref[b] * _lane_pick(dec_row, lane1, b)
                 + _lane_pick(xdt_t, lane, b) * b_row)
        so_ref[b] = s_new
        y_col = jnp.sum(s_new * c_row, axis=1, keepdims=True)
        return jnp.where(lane == b, y_col, y_t)

    y_scr[h] = lax.fori_loop(0, bsz, step, jnp.zeros((head_dim, bsz), F32), unroll=DECODE_UNROLL)

    @pl.when(h == n_heads - 1)
    def _():
        y = y_scr[...].reshape(width, bsz).T + skip_scr[...]
        o_ref[...] = _ssm_gated_norm(y, z_ref[...], nw_ref[...], n_groups)


def _ssd_decode(xbc, small, z, conv_all, state_all, new_all, prm, *, layer, n_groups):
    bsz = xbc.shape[0]
    n_heads, head_dim, d_state = state_all.shape[2:]
    width = n_heads * head_dim
    names = ("conv_w", "conv_b", "dtb_row", "alog_row", "d", "norm_w")
    return _layer_state_call(
        functools.partial(_ssd_decode_body, n_heads=n_heads, head_dim=head_dim,
                          n_groups=n_groups, d_state=d_state), "ssd_decode",
        [xbc, small, z], [], state_all, new_all, [prm[n] for n in names],
        layer=layer, n_row_out=[width], tap_inputs=[conv_all],
        scratch_shapes=[pltpu.VMEM((n_heads, head_dim, bsz), F32),
                        pltpu.VMEM((2 * n_groups, bsz, d_state), F32),
                        pltpu.VMEM((SMALL_W, bsz), F32),
                        pltpu.VMEM((n_heads, head_dim, bsz), F32),
                        pltpu.VMEM((bsz, width), F32)])


def _gdn_decode_body(qkv_ref, small_ref, z_ref, conv_ref, s_ref, cw_ref, dtb_row_ref,
                     alog_row_ref, nw_ref, o_ref, convo_ref, so_ref,
                     t_scr, v_scr, gate_scr, y_scr, *, n_heads, head_dim, a_off, b_off):
    h = pl.program_id(0)
    width = n_heads * head_dim
    bsz = qkv_ref.shape[0]

    @pl.when(h == 0)
    def _():
        qkv = _silu(_conv_step(qkv_ref[...], conv_ref, convo_ref, cw_ref))
        scale = head_dim ** -0.5
        q = jnp.concatenate([_l2norm(qkv[:, i * head_dim:(i + 1) * head_dim]) * scale
                             for i in range(n_heads)], axis=-1)
        k = jnp.concatenate([_l2norm(qkv[:, width + i * head_dim:width + (i + 1) * head_dim])
                             for i in range(n_heads)], axis=-1)
        for i, arr in enumerate((q, k)):
            t_scr[i] = arr.T.reshape(n_heads, head_dim, bsz)
        for i in range(n_heads):
            v_scr[i] = qkv[:, 2 * width + i * head_dim:2 * width + (i + 1) * head_dim]
        small = small_ref[...]
        lane_s = _iota2((1, SMALL_W), 1)
        g = jnp.where(jnp.logical_and(lane_s >= a_off, lane_s < a_off + n_heads),
                      -jnp.exp(alog_row_ref[...]) * _softplus(small + dtb_row_ref[...]), 0.0)
        gate_scr[0] = jnp.exp(g).T
        gate_scr[1] = _sigmoid(small).T
        gate_scr[2, 0:n_heads, :] = jnp.concatenate(
            [jnp.sum(q[:, i * head_dim:(i + 1) * head_dim] * k[:, i * head_dim:(i + 1) * head_dim],
                     axis=1, keepdims=True) for i in range(n_heads)], axis=1).T

    q_t = t_scr[0, h]
    k_t = t_scr[1, h]
    eg_row = gate_scr[0, pl.ds(a_off + h, 1), :]
    beta_row = gate_scr[1, pl.ds(b_off + h, 1), :]
    qk_row = gate_scr[2, pl.ds(h, 1), :]
    lane = _iota2((head_dim, bsz), 1)
    lane1 = _iota2((1, bsz), 1)

    def step(b, carry):
        s_b = s_ref[b]
        k_col = _lane_pick(k_t, lane, b)
        q_col = _lane_pick(q_t, lane, b)
        eg = _lane_pick(eg_row, lane1, b)
        k_s = jnp.sum(s_b * k_col, axis=0, keepdims=True)
        q_s = jnp.sum(s_b * q_col, axis=0, keepdims=True)
        v_new = _lane_pick(beta_row, lane1, b) * (v_scr[h, pl.ds(b, 1), :] - eg * k_s)
        y_scr[h, pl.ds(b, 1), :] = eg * q_s + _lane_pick(qk_row, lane1, b) * v_new
        so_ref[b] = s_b * eg + k_col * v_new
        return carry

    lax.fori_loop(0, bsz, step, 0, unroll=DECODE_UNROLL)

    @pl.when(h == n_heads - 1)
    def _():
        z = z_ref[...]
        outs = []
        for i in range(n_heads):
            sl = slice(i * head_dim, (i + 1) * head_dim)
            outs.append(_gdn_post_head(y_scr[i], z[:, sl], nw_ref[...]))
        o_ref[...] = jnp.concatenate(outs, axis=-1)


def _gdn_decode(qkv, small, z, conv_all, state_all, new_all, prm, *, layer, a_off, b_off):
    bsz = qkv.shape[0]
    n_heads, head_dim = state_all.shape[2], state_all.shape[3]
    width = n_heads * head_dim
    names = ("conv_w", "dtb_row", "alog_row", "norm_w")
    return _layer_state_call(
        functools.partial(_gdn_decode_body, n_heads=n_heads, head_dim=head_dim,
                          a_off=a_off, b_off=b_off), "gdn_decode",
        [qkv, small, z], [], state_all, new_all, [prm[n] for n in names],
        layer=layer, n_row_out=[width], tap_inputs=[conv_all],
        scratch_shapes=[pltpu.VMEM((2, n_heads, head_dim, bsz), F32),
                        pltpu.VMEM((n_heads, bsz, head_dim), F32),
                        pltpu.VMEM((3, SMALL_W, bsz), F32),
                        pltpu.VMEM((n_heads, bsz, head_dim), F32)])


def _segment_ones(width, seg):
    i = jnp.arange(2 * width)[:, None] % width
    j = jnp.arange(width)[None, :]
    return ((i // seg) == (j // seg)).astype(BF16)


def _pad_lanes(row, offset):
    out = jnp.zeros((SMALL_W,), F32)
    return lax.dynamic_update_slice(out, row.astype(F32), (offset,)).reshape(1, SMALL_W)


def _row_tile(seq_len, cap):
    best = 8
    for tm in range(8, cap + 1, 8):
        if seq_len % tm == 0:
            best = tm
    return best


def kernel(x_prompt, x_sample, state_rwkv_wkv, state_rwkv_shift, state_ssm, state_ssm_conv,
           state_gdn, state_gdn_conv, meta_tokens, norm_w, w_in, rw_mu, rw_w0, rw_w2, rw_a0,
           rw_a2, rw_k_k, rw_k_a, rw_r_k, rw_gn_w, rw_gn_b, ssm_conv_w, ssm_conv_b,
           ssm_dt_bias, ssm_a_log, ssm_d, ssm_norm_w, gdn_conv_w, gdn_dt_bias, gdn_a_log,
           gdn_norm_w, w_rw_out, w_ssm_out, w_gdn_out, w_out, final_norm_w):
    depth = w_in.shape[0]
    bp, seq, d_model = x_prompt.shape
    bd = x_sample.shape[0]
    n_meta = meta_tokens.shape[0]
    rw_heads, rw_hd = rw_r_k.shape[1], rw_r_k.shape[2]
    rw_w = rw_heads * rw_hd
    rw_shift_w = rw_mu.shape[1]
    ssm_heads, ssm_hd, ssm_state = state_ssm.shape[2], state_ssm.shape[3], state_ssm.shape[4]
    ssm_w = ssm_heads * ssm_hd
    ssm_conv_ch = ssm_conv_w.shape[2]
    ssm_groups = (ssm_conv_ch - ssm_w) // (2 * ssm_state)
    gdn_heads, gdn_hd = state_gdn.shape[2], state_gdn.shape[3]
    gdn_w = gdn_heads * gdn_hd
    gdn_conv_ch = gdn_conv_w.shape[2]
    a_off, b_off = ssm_heads, ssm_heads + gdn_heads
    assert b_off + gdn_heads <= SMALL_W

    off = 0
    cols = {}
    for name, wdt in (("rw", rw_shift_w), ("rw_z", rw_w), ("ssm_z", ssm_w), ("xbc", ssm_conv_ch),
                      ("dt", ssm_heads), ("qkv", gdn_conv_ch), ("gdn_z", gdn_w),
                      ("ga", gdn_heads), ("gb", gdn_heads), ("gate", 3 * d_model)):
        cols[name] = (off, off + wdt)
        off += wdt
    assert off == w_in.shape[2]

    seg_names = ("rw", "rw_z", "ssm_z", "xbc", "qkv", "gdn_z", "gate", "dt", "ga", "gb")
    n_small = ssm_heads + 2 * gdn_heads
    w_all = _pack_weights(jnp.swapaxes(w_in, 1, 2), [cols[n] for n in seg_names],
                          SMALL_W - n_small, kb=PACK_KB)
    widths = [cols[n][1] - cols[n][0] for n in seg_names[:7]] + [SMALL_W]
    assert all(w % LANE == 0 for w in widths)
    w_r, w_s, w_g, w_o = (w.astype(BF16) for w in (w_rw_out, w_ssm_out, w_gdn_out, w_out))

    def layer_params(l):
        rw = dict(mu=rw_mu[l][None], w0=rw_w0[l][None], w2=rw_w2[l], a0=rw_a0[l][None],
                  a2=rw_a2[l], k_k=rw_k_k[l][None], k_a=rw_k_a[l][None],
                  r_k=rw_r_k[l].reshape(1, rw_w), gn_w=rw_gn_w[l][None], gn_b=rw_gn_b[l][None])
        dtb = _pad_lanes(ssm_dt_bias[l], 0)
        alog = _pad_lanes(ssm_a_log[l], 0)
        sm = dict(conv_w=ssm_conv_w[l], conv_b=ssm_conv_b[l][None], dtb_row=dtb,
                  dtb_col=dtb.reshape(SMALL_W, 1), alog_row=alog,
                  alog_col=alog.reshape(SMALL_W, 1),
                  d=jnp.repeat(ssm_d[l], ssm_hd)[None], norm_w=ssm_norm_w[l][None])
        gdtb = _pad_lanes(gdn_dt_bias[l], a_off)
        galog = _pad_lanes(gdn_a_log[l], a_off)
        gd = dict(conv_w=gdn_conv_w[l], dtb_row=gdtb, dtb_col=gdtb.reshape(SMALL_W, 1),
                  alog_row=galog, alog_col=galog.reshape(SMALL_W, 1),
                  norm_w=gdn_norm_w[l][None])
        return rw, sm, gd

    fw = final_norm_w[None]

    real_len = n_meta + seq
    seq_p = -(-real_len // CHUNK) * CHUNK
    n_pad = seq_p - real_len
    n_chunks = seq_p // CHUNK
    n_sub = max(d for d in range(1, MAX_CHUNKS_PER_STEP + 1) if n_chunks % d == 0)
    n_steps = n_chunks // n_sub
    seqs_per_step = lambda cap: max(d for d in range(1, cap + 1) if bp % d == 0)
    meta = jnp.broadcast_to(meta_tokens.astype(F32)[None], (bp, n_meta, d_model))
    xp = jnp.concatenate([jnp.zeros((bp, n_pad, d_model), F32), meta, x_prompt], axis=1)
    xp = xp.reshape(bp * seq_p, d_model)
    tm_in = tm_out = _row_tile(seq_p, ROW_TILE_CAP)

    p_states = [[] for _ in range(6)]
    y_p = None
    for l in range(depth):
        rw_p, sm_p, gd_p = layer_params(l)
        p_rw, rw_z, ssm_z, xbc, qkv, gdn_z, gate, small = _inproj(
            xp, norm_w[l][None], w_all, widths, layer=l, tm=tm_in, n_pad=n_pad, seq_len=seq_p)
        rw_o, wkv, shift = _rwkv_prompt(p_rw, rw_z, rw_p, batch=bp, n_steps=n_steps,
                                        n_sub=n_sub, n_seq=seqs_per_step(RWKV_SEQS_PER_STEP),
                                        n_heads=rw_heads,
                                        head_dim=rw_hd)
        sm_o, ssm, sconv = _ssd_prompt(xbc, small, ssm_z, sm_p, batch=bp, n_steps=n_steps,
                                       n_sub=n_sub, n_seq=seqs_per_step(SSD_SEQS_PER_STEP),
                                       n_heads=ssm_heads,
                                       head_dim=ssm_hd, n_groups=ssm_groups, d_state=ssm_state,
                                       n_pad=n_pad)
        gd_o, gdn, gconv = _gdn_prompt(qkv, small, gdn_z, gd_p, batch=bp, n_steps=n_steps,
                                       n_sub=n_sub, n_seq=seqs_per_step(GDN_SEQS_PER_STEP),
                                       n_heads=gdn_heads,
                                       head_dim=gdn_hd, a_off=a_off, b_off=b_off)
        last = l == depth - 1
        res = _outproj(rw_o, sm_o, gd_o, gate, xp, w_r[l], w_s[l], w_g[l], w_o[l], fw,
                       tm=tm_out, with_final=last)
        xp = res[0]
        if last:
            y_p = res[1]
        for acc, s in zip(p_states, (wkv, shift.reshape(bp, rw_shift_w), ssm, sconv, gdn, gconv)):
            acc.append(s)
    y_prompt = y_p.reshape(bp, seq_p, d_model)[:, n_pad + n_meta:]

    xs = x_sample.reshape(bd, d_model)
    shift_all = state_rwkv_shift.reshape(depth * bd, -1)
    sconv_all = jnp.swapaxes(state_ssm_conv, 1, 2)
    gconv_all = jnp.swapaxes(state_gdn_conv, 1, 2)
    wkv_all = jnp.transpose(state_rwkv_wkv, (0, 2, 3, 4, 1))
    wkv_new, ssm_new, gdn_new = (jnp.zeros(a.shape, F32) for a in (wkv_all, state_ssm, state_gdn))
    small_states = [[] for _ in range(3)]
    y_s = None
    for l in range(depth):
        rw_p, sm_p, gd_p = layer_params(l)
        p_rw, rw_z, ssm_z, xbc, qkv, gdn_z, gate, small = _inproj(
            xs, norm_w[l][None], w_all, widths, layer=l, tm=bd, n_pad=0, seq_len=bd)
        rw_o, wkv_new = _rwkv_decode(p_rw, rw_z, shift_all, wkv_all, wkv_new, rw_p, layer=l)
        sm_o, sconv, ssm_new = _ssd_decode(xbc, small, ssm_z, sconv_all, state_ssm, ssm_new, sm_p,
                                           layer=l, n_groups=ssm_groups)
        gd_o, gconv, gdn_new = _gdn_decode(qkv, small, gdn_z, gconv_all, state_gdn, gdn_new, gd_p,
                                           layer=l, a_off=a_off, b_off=b_off)
        last = l == depth - 1
        res = _outproj(rw_o, sm_o, gd_o, gate, xs, w_r[l], w_s[l], w_g[l], w_o[l], fw,
                       tm=bd, with_final=last)
        xs = res[0]
        if last:
            y_s = res[1]
        for acc, s in zip(small_states, (p_rw, sconv, gconv)):
            acc.append(s)
    y_sample = y_s.reshape(bd, 1, d_model)
    s_shift, s_sconv, s_gconv = (jnp.stack(a) for a in small_states)
    s_states = (jnp.transpose(wkv_new, (0, 4, 1, 2, 3)), s_shift, ssm_new,
                jnp.swapaxes(s_sconv, 1, 2), gdn_new, jnp.swapaxes(s_gconv, 1, 2))

    return (y_prompt, y_sample) + tuple(jnp.stack(a) for a in p_states) + s_states
```
